```python
import jax
import jax.numpy as jnp
from jax import lax
import numpy as np

D_MODEL = 2048
BATCH = 8
SEQ = 8192
DEPTH = 2

GRID_W = 64
CTX_LEN = 256
EPS = 1e-6

N_EVEN = (DEPTH + 1) // 2
N_ODD = DEPTH // 2

A_WIDTH = D_MODEL // 2
A_HEAD_DIM = 128
A_HEADS = A_WIDTH // A_HEAD_DIM
A_CHUNK = 32
B_WIDTH = D_MODEL // 2
POOL_WINDOWS = (2, 4, 8, 16)
B_GROUPS = len(POOL_WINDOWS)
B_GROUP_DIM = B_WIDTH // B_GROUPS
EVEN_IN = 5 * A_WIDTH + B_WIDTH
EVEN_MIX = A_WIDTH + B_WIDTH

C_HEAD_DIM = 128
C_KEY_HEADS = D_MODEL // C_HEAD_DIM
C_VALUE_HEADS = 2 * C_KEY_HEADS
C_KEY_WIDTH = C_KEY_HEADS * C_HEAD_DIM
C_VALUE_WIDTH = C_VALUE_HEADS * C_HEAD_DIM
C_QKV = 2 * C_KEY_WIDTH + C_VALUE_WIDTH
C_CONV = 4
C_CHUNK = 64
ODD_IN = C_QKV + C_VALUE_WIDTH + 4 * C_VALUE_HEADS

FFN_HIDDEN = -(-8 * D_MODEL // (3 * 256)) * 256

kernel_name = "hybrid_hgrn2_pool_gdn_diffusion_trunk"


def rmsnorm(x, w):
    xf = x.astype(jnp.float32)
    y = xf * lax.rsqrt(jnp.mean(xf * xf, axis=-1, keepdims=True) + EPS)
    return (y * w.astype(jnp.float32)).astype(x.dtype)


def l2norm(x):
    return x * lax.rsqrt(jnp.sum(x * x, axis=-1, keepdims=True) + EPS)


def modulate(h, shift, scale):
    return h * (1.0 + scale) + shift


def split_heads(t, d):
    return t.reshape(t.shape[:-1] + (t.shape[-1] // d, d))


def swiglu(h, w13, w2):
    gate, up = jnp.split(h @ w13, 2, axis=-1)
    return (jax.nn.silu(gate) * up) @ w2


def centred_short_conv(u, w):
    length = u.shape[1]
    left = C_CONV // 2
    up = jnp.pad(u, ((0, 0), (left, C_CONV - 1 - left), (0, 0)))
    out = up[:, 0:length] * w[0]
    for j in range(1, C_CONV):
        out = out + up[:, j:j + length] * w[j]
    return out


def hgrn2_chunk_scan(q, k, v, log_f, s0):
    bsz, length, heads, _ = q.shape
    dv = v.shape[-1]
    n = length // A_CHUNK
    rs = lambda t: t.reshape(bsz, n, A_CHUNK, heads, t.shape[-1]).transpose(1, 0, 3, 2, 4)
    q, k, v, log_f = rs(q), rs(k), rs(v), rs(log_f)
    b = jnp.cumsum(log_f, axis=-2)
    b_last = b[..., -1:, :]
    b_mid = b[..., A_CHUNK // 2 - 1:A_CHUNK // 2, :]
    causal = jnp.tril(jnp.ones((A_CHUNK, A_CHUNK), bool))
    scores = jnp.einsum('nbhtd,nbhsd->nbhts', q * jnp.exp(b - b_mid), k * jnp.exp(b_mid - b))
    o_intra = jnp.einsum('nbhts,nbhsv->nbhtv', jnp.where(causal, scores, 0.0), v)
    q_in = q * jnp.exp(b)
    k_out = k * jnp.exp(b_last - b)
    decay = jnp.exp(b_last)

    def step(state, inp):
        qi, ki, vi, dl = inp
        o = jnp.einsum('bhtd,bhdv->bhtv', qi, state)
        state = state * dl[..., 0, :, None] + jnp.einsum('bhsd,bhsv->bhdv', ki, vi)
        return state, o

    s_final, o_inter = lax.scan(step, s0, (q_in, k_out, v, decay))
    o = (o_intra + o_inter).transpose(1, 0, 3, 2, 4).reshape(bsz, length, heads, dv)
    return o, s_final


def gated_delta_chunk_scan(q, k, v, g, beta, s0):
    bsz, length, heads, _ = q.shape
    dv = v.shape[-1]
    n = length // C_CHUNK
    rs = lambda t: t.reshape(bsz, n, C_CHUNK, heads, t.shape[-1]).transpose(1, 0, 3, 2, 4)
    q, k, v = rs(q), rs(k), rs(v)
    g = rs(g[..., None])[..., 0]
    beta = rs(beta[..., None])[..., 0]
    gc = jnp.cumsum(g, axis=-1)
    causal = jnp.tril(jnp.ones((C_CHUNK, C_CHUNK), bool))
    strict = jnp.tril(jnp.ones((C_CHUNK, C_CHUNK), bool), -1)
    diff = gc[..., :, None] - gc[..., None, :]
    decay = jnp.where(causal, jnp.exp(jnp.where(causal, diff, 0.0)), 0.0)
    k_beta = k * beta[..., None]
    v_beta = v * beta[..., None]
    a_low = jnp.where(strict, jnp.einsum('nbhtd,nbhsd->nbhts', k_beta, k) * decay, 0.0)
    t_mat = jnp.eye(C_CHUNK, dtype=a_low.dtype) + a_low
    u = lax.linalg.triangular_solve(t_mat, v_beta, left_side=True, lower=True, unit_diagonal=True)
    w = lax.linalg.triangular_solve(t_mat, k_beta * jnp.exp(gc)[..., None], left_side=True,
                                    lower=True, unit_diagonal=True)
    qk = jnp.where(causal, jnp.einsum('nbhtd,nbhsd->nbhts', q, k) * decay, 0.0)
    q_dec = q * jnp.exp(gc)[..., None]
    k_dec = k * jnp.exp(gc[..., -1:] - gc)[..., None]
    g_last = jnp.exp(gc[..., -1])

    def step(state, inp):
        u_i, w_i, qk_i, qd_i, kd_i, gl_i = inp
        v_new = u_i - jnp.einsum('bhtd,bhdv->bhtv', w_i, state)
        o = jnp.einsum('bhtd,bhdv->bhtv', qd_i, state) + jnp.einsum('bhts,bhsv->bhtv', qk_i, v_new)
        state = state * gl_i[..., None, None] + jnp.einsum('bhsd,bhsv->bhdv', kd_i, v_new)
        return state, o

    s_final, o = lax.scan(step, s0, (u, w, qk, q_dec, k_dec, g_last))
    o = o.transpose(1, 0, 3, 2, 4).reshape(bsz, length, heads, dv)
    return o, s_final


def bidirectional_prefix_scan(scan_fn, ctx_f, lat_f, ctx_b, lat_b, s0):
    flip = lambda args: tuple(jnp.flip(t, axis=1) for t in args)
    oc_f, sc_f = scan_fn(*ctx_f, s0)
    ol_f, _ = scan_fn(*lat_f, sc_f)
    oc_b, sc_b = scan_fn(*flip(ctx_b), s0)
    ol_b, _ = scan_fn(*flip(lat_b), sc_b)
    return ol_f + jnp.flip(ol_b, axis=1), oc_f + jnp.flip(oc_b, axis=1)


def multiscale_pool(u, pool_w, pool_scale):
    length = u.shape[-2]
    uf = u.astype(jnp.float32).reshape(u.shape[:-1] + (B_GROUPS, B_GROUP_DIM))
    cs = jnp.concatenate([jnp.zeros_like(uf[..., :1, :, :]), jnp.cumsum(uf, axis=-3)], axis=-3)
    pos = jnp.arange(length)
    mixed = []
    for gi, win in enumerate(POOL_WINDOWS):
        lo = jnp.clip(pos - win // 2, 0, length - 1)
        hi = jnp.clip(pos + win - 1 - win // 2, 0, length - 1)
        cnt = (hi - lo + 1).astype(jnp.float32)[:, None]
        csg = cs[..., gi, :]
        mean = (jnp.take(csg, hi + 1, axis=-2) - jnp.take(csg, lo, axis=-2)) / cnt
        mixed.append(mean - uf[..., gi, :])
    d = jnp.stack(mixed, axis=-2)
    y = jnp.einsum('...lgc,gcd->...lgd', d, pool_w.astype(jnp.float32))
    return y.reshape(u.shape) * pool_scale.astype(jnp.float32)


def hgrn2_gates(pre_f, lb):
    s = pre_f.astype(jnp.float32)
    log_f = jnp.log(lb + (1.0 - lb) * jax.nn.sigmoid(s))
    k = (1.0 - lb) * jax.nn.sigmoid(-s)
    return split_heads(k, A_HEAD_DIM), split_heads(log_f, A_HEAD_DIM)


def even_mixer(h_lat, h_ctx, rows, lb, w_in, a_norm, pool_w, pool_scale, w_out, need_ctx):
    bsz, seq_len, _ = h_lat.shape

    def project(h):
        p = h @ w_in
        q, f_f, f_b, i, g, u = jnp.split(
            p, [A_WIDTH, 2 * A_WIDTH, 3 * A_WIDTH, 4 * A_WIDTH, 5 * A_WIDTH], axis=-1)
        q = split_heads(jax.nn.silu(q.astype(jnp.float32)), A_HEAD_DIM)
        i = split_heads(i.astype(jnp.float32), A_HEAD_DIM)
        k_f, logf_f = hgrn2_gates(f_f, lb[0])
        k_b, logf_b = hgrn2_gates(f_b, lb[1])
        return (q, k_f, i, logf_f), (q, k_b, i, logf_b), g, u

    def readout(h, o, g, pooled):
        a_out = rmsnorm(o, a_norm) * jax.nn.silu(split_heads(g.astype(jnp.float32), A_HEAD_DIM))
        a_out = a_out.reshape(o.shape[:-2] + (A_WIDTH,))
        return jnp.concatenate([a_out, pooled], axis=-1).astype(h.dtype) @ w_out

    fwd_l, bwd_l, g_l, u_l = project(h_lat)
    fwd_c, bwd_c, g_c, u_c = project(h_ctx)
    s0 = jnp.zeros((bsz, A_HEADS, A_HEAD_DIM, A_HEAD_DIM), jnp.float32)
    o_l, o_c = bidirectional_prefix_scan(hgrn2_chunk_scan, fwd_c, fwd_l, bwd_c, bwd_l, s0)
    pooled_l = multiscale_pool(u_l.reshape(bsz, rows, GRID_W, B_WIDTH), pool_w, pool_scale)
    y_l = readout(h_lat, o_l, g_l, pooled_l.reshape(bsz, seq_len, B_WIDTH))
    if not need_ctx:
        return y_l, None
    y_c = readout(h_ctx, o_c, g_c, multiscale_pool(u_c, pool_w, pool_scale))
    return y_l, y_c


def odd_mixer(h_lat, h_ctx, w_in, conv_w, a_log, dt_bias, norm_w, w_out, need_ctx):
    bsz = h_lat.shape[0]
    rep = C_VALUE_HEADS // C_KEY_HEADS
    a_rate = jnp.exp(a_log.astype(jnp.float32))
    dtb = dt_bias.astype(jnp.float32)

    def project(h):
        p = h @ w_in
        qkv, z, gates = jnp.split(p, [C_QKV, C_QKV + C_VALUE_WIDTH], axis=-1)
        qkv = jax.nn.silu(centred_short_conv(qkv, conv_w).astype(jnp.float32))
        q, k, v = jnp.split(qkv, [C_KEY_WIDTH, 2 * C_KEY_WIDTH], axis=-1)
        q = jnp.repeat(l2norm(split_heads(q, C_HEAD_DIM)) * C_HEAD_DIM ** -0.5, rep, axis=2)
        k = jnp.repeat(l2norm(split_heads(k, C_HEAD_DIM)), rep, axis=2)
        v = split_heads(v, C_HEAD_DIM)
        a_f, a_b, b_f, b_b = jnp.split(gates.astype(jnp.float32), 4, axis=-1)
        g_f = -a_rate[0] * jax.nn.softplus(a_f + dtb[0])
        g_b = -a_rate[1] * jax.nn.softplus(a_b + dtb[1])
        return (q, k, v, g_f, jax.nn.sigmoid(b_f)), (q, k, v, g_b, jax.nn.sigmoid(b_b)), z

    def readout(h, o, z):
        y = rmsnorm(o, norm_w) * jax.nn.silu(split_heads(z.astype(jnp.float32), C_HEAD_DIM))
        return y.reshape(o.shape[:-2] + (C_VALUE_WIDTH,)).astype(h.dtype) @ w_out

    fwd_l, bwd_l, z_l = project(h_lat)
    fwd_c, bwd_c, z_c = project(h_ctx)
    s0 = jnp.zeros((bsz, C_VALUE_HEADS, C_HEAD_DIM, C_HEAD_DIM), jnp.float32)
    o_l, o_c = bidirectional_prefix_scan(gated_delta_chunk_scan, fwd_c, fwd_l, bwd_c, bwd_l, s0)
    y_l = readout(h_lat, o_l, z_l)
    if not need_ctx:
        return y_l, None
    return y_l, readout(h_ctx, o_c, z_c)


def _fwd_setup_inputs(seed: int = 0) -> dict:
    key = jax.random.key(seed)
    ks = jax.random.split(key, 24)
    nrm = lambda k, shape, s: jax.random.normal(k, shape, jnp.float32) * s
    dt = jnp.exp(jax.random.uniform(ks[15], (N_ODD, 2, C_VALUE_HEADS), jnp.float32,
                                    minval=float(np.log(1e-3)), maxval=float(np.log(1e-1))))
    return {
        "x": nrm(ks[0], (BATCH, SEQ, D_MODEL), 1.0),
        "c": nrm(ks[1], (BATCH, D_MODEL), 1.0),
        "ctx": nrm(ks[2], (BATCH, CTX_LEN, D_MODEL), 1.0),
        "c_ctx": nrm(ks[3], (D_MODEL,), 1.0),
        "w_ada": nrm(ks[4], (DEPTH, D_MODEL, 6 * D_MODEL), 0.5 * D_MODEL ** -0.5),
        "b_ada": nrm(ks[5], (DEPTH, 6 * D_MODEL), 0.02),
        "norm_w": 1.0 + nrm(ks[6], (DEPTH, 4, D_MODEL), 0.02),
        "ev_w_in": nrm(ks[7], (N_EVEN, D_MODEL, EVEN_IN), D_MODEL ** -0.5),
        "ev_lb": nrm(ks[8], (2, DEPTH + 1, A_WIDTH), 0.1),
        "ev_a_norm": 1.0 + nrm(ks[9], (N_EVEN, A_HEAD_DIM), 0.02),
        "ev_pool_w": nrm(ks[10], (N_EVEN, B_GROUPS, B_GROUP_DIM, B_GROUP_DIM), B_GROUP_DIM ** -0.5),
        "ev_pool_scale": 1.0 + nrm(ks[11], (N_EVEN, B_WIDTH), 0.02),
        "ev_w_out": nrm(ks[12], (N_EVEN, EVEN_MIX, D_MODEL), EVEN_MIX ** -0.5),
        "od_w_in": nrm(ks[13], (N_ODD, D_MODEL, ODD_IN), D_MODEL ** -0.5),
        "od_conv": nrm(ks[14], (N_ODD, C_CONV, C_QKV), C_CONV ** -0.5),
        "od_A_log": jnp.log(jax.random.uniform(ks[16], (N_ODD, 2, C_VALUE_HEADS), jnp.float32,
                                               minval=1.0, maxval=16.0)),
        "od_dt_bias": dt + jnp.log(-jnp.expm1(-dt)),
        "od_norm": 1.0 + nrm(ks[17], (N_ODD, C_HEAD_DIM), 0.02),
        "od_w_out": nrm(ks[18], (N_ODD, C_VALUE_WIDTH, D_MODEL), C_VALUE_WIDTH ** -0.5),
        "ffn_w13": nrm(ks[19], (DEPTH, D_MODEL, 2 * FFN_HIDDEN), D_MODEL ** -0.5),
        "ffn_w2": nrm(ks[20], (DEPTH, FFN_HIDDEN, D_MODEL), FFN_HIDDEN ** -0.5),
    }


def _fwd_reference(x, c, ctx, c_ctx, w_ada, b_ada, norm_w, ev_w_in, ev_lb, ev_a_norm, ev_pool_w,
              ev_pool_scale, ev_w_out, od_w_in, od_conv, od_A_log, od_dt_bias, od_norm, od_w_out,
              ffn_w13, ffn_w2):
    rows = x.shape[1] // GRID_W
    lb_all = jnp.cumsum(jax.nn.softmax(ev_lb.astype(jnp.float32), axis=1), axis=1)
    silu_c = jax.nn.silu(c)
    silu_cc = jax.nn.silu(c_ctx)[None, :]
    for layer in range(DEPTH):
        need_ctx = layer < DEPTH - 1
        j = layer // 2
        m_l = [t[:, None, :] for t in jnp.split(silu_c @ w_ada[layer] + b_ada[layer], 6, axis=-1)]
        m_c = [t[:, None, :] for t in jnp.split(silu_cc @ w_ada[layer] + b_ada[layer], 6, axis=-1)]
        h_l = modulate(rmsnorm(x, norm_w[layer, 0]), m_l[0], m_l[1])
        h_c = modulate(rmsnorm(ctx, norm_w[layer, 0]), m_c[0], m_c[1])
        if layer % 2 == 0:
            y_l, y_c = even_mixer(h_l, h_c, rows, lb_all[:, layer], ev_w_in[j], ev_a_norm[j],
                                  ev_pool_w[j], ev_pool_scale[j], ev_w_out[j], need_ctx)
        else:
            y_l, y_c = odd_mixer(h_l, h_c, od_w_in[j], od_conv[j], od_A_log[j], od_dt_bias[j],
                                 od_norm[j], od_w_out[j], need_ctx)
        x = x + m_l[2] * rmsnorm(y_l, norm_w[layer, 1])
        f_l = swiglu(modulate(rmsnorm(x, norm_w[layer, 2]), m_l[3], m_l[4]), ffn_w13[layer], ffn_w2[layer])
        x = x + m_l[5] * rmsnorm(f_l, norm_w[layer, 3])
        if need_ctx:
            ctx = ctx + m_c[2] * rmsnorm(y_c, norm_w[layer, 1])
            f_c = swiglu(modulate(rmsnorm(ctx, norm_w[layer, 2]), m_c[3], m_c[4]), ffn_w13[layer], ffn_w2[layer])
            ctx = ctx + m_c[5] * rmsnorm(f_c, norm_w[layer, 3])
    return x


import jax as _jax
import jax.numpy as _jnp

TWIN_FORMAT = 'train_step'
FWD_PARAMS = ['x', 'c', 'ctx', 'c_ctx', 'w_ada', 'b_ada', 'norm_w', 'ev_w_in', 'ev_lb', 'ev_a_norm', 'ev_pool_w', 'ev_pool_scale', 'ev_w_out', 'od_w_in', 'od_conv', 'od_A_log', 'od_dt_bias', 'od_norm', 'od_w_out', 'ffn_w13', 'ffn_w2']
TWIN_WEIGHTS = ['c_ctx', 'w_ada', 'b_ada', 'norm_w', 'ev_w_in', 'ev_lb', 'ev_a_norm', 'ev_pool_w', 'ev_pool_scale', 'ev_w_out', 'od_w_in', 'od_conv', 'od_A_log', 'od_dt_bias', 'od_norm', 'od_w_out', 'ffn_w13', 'ffn_w2']
TWIN_DIFF_INPUT = 'x'
TWIN_INPUTS = ['x', 'c', 'ctx', 'c_ctx', 'w_ada', 'b_ada', 'norm_w', 'ev_w_in', 'ev_lb', 'ev_a_norm', 'ev_pool_w', 'ev_pool_scale', 'ev_w_out', 'od_w_in', 'od_conv', 'od_A_log', 'od_dt_bias', 'od_norm', 'od_w_out', 'ffn_w13', 'ffn_w2', 'loss_target', 'm_c_ctx', 'm_w_ada', 'm_b_ada', 'm_norm_w', 'm_ev_w_in', 'm_ev_lb', 'm_ev_a_norm', 'm_ev_pool_w', 'm_ev_pool_scale', 'm_ev_w_out', 'm_od_w_in', 'm_od_conv', 'm_od_A_log', 'm_od_dt_bias', 'm_od_norm', 'm_od_w_out', 'm_ffn_w13', 'm_ffn_w2', 'v_c_ctx', 'v_w_ada', 'v_b_ada', 'v_norm_w', 'v_ev_w_in', 'v_ev_lb', 'v_ev_a_norm', 'v_ev_pool_w', 'v_ev_pool_scale', 'v_ev_w_out', 'v_od_w_in', 'v_od_conv', 'v_od_A_log', 'v_od_dt_bias', 'v_od_norm', 'v_od_w_out', 'v_ffn_w13', 'v_ffn_w2']
TWIN_OUTPUTS = ['loss', 'grad_x', 'grad_c_ctx', 'grad_w_ada', 'grad_b_ada', 'grad_norm_w', 'grad_ev_w_in', 'grad_ev_lb', 'grad_ev_a_norm', 'grad_ev_pool_w', 'grad_ev_pool_scale', 'grad_ev_w_out', 'grad_od_w_in', 'grad_od_conv', 'grad_od_A_log', 'grad_od_dt_bias', 'grad_od_norm', 'grad_od_w_out', 'grad_ffn_w13', 'grad_ffn_w2', 'delta_c_ctx', 'delta_w_ada', 'delta_b_ada', 'delta_norm_w', 'delta_ev_w_in', 'delta_ev_lb', 'delta_ev_a_norm', 'delta_ev_pool_w', 'delta_ev_pool_scale', 'delta_ev_w_out', 'delta_od_w_in', 'delta_od_conv', 'delta_od_A_log', 'delta_od_dt_bias', 'delta_od_norm', 'delta_od_w_out', 'delta_ffn_w13', 'delta_ffn_w2', 'new_m_c_ctx', 'new_m_w_ada', 'new_m_b_ada', 'new_m_norm_w', 'new_m_ev_w_in', 'new_m_ev_lb', 'new_m_ev_a_norm', 'new_m_ev_pool_w', 'new_m_ev_pool_scale', 'new_m_ev_w_out', 'new_m_od_w_in', 'new_m_od_conv', 'new_m_od_A_log', 'new_m_od_dt_bias', 'new_m_od_norm', 'new_m_od_w_out', 'new_m_ffn_w13', 'new_m_ffn_w2', 'new_v_c_ctx', 'new_v_w_ada', 'new_v_b_ada', 'new_v_norm_w', 'new_v_ev_w_in', 'new_v_ev_lb', 'new_v_ev_a_norm', 'new_v_ev_pool_w', 'new_v_ev_pool_scale', 'new_v_ev_w_out', 'new_v_od_w_in', 'new_v_od_conv', 'new_v_od_A_log', 'new_v_od_dt_bias', 'new_v_od_norm', 'new_v_od_w_out', 'new_v_ffn_w13', 'new_v_ffn_w2']
TWIN_LEAF_KINDS = {'loss': 'loss', 'grad_x': 'grad_x', 'grad_c_ctx': 'grad_w', 'grad_w_ada': 'grad_w', 'grad_b_ada': 'grad_w', 'grad_norm_w': 'grad_w', 'grad_ev_w_in': 'grad_w', 'grad_ev_lb': 'grad_w', 'grad_ev_a_norm': 'grad_w', 'grad_ev_pool_w': 'grad_w', 'grad_ev_pool_scale': 'grad_w', 'grad_ev_w_out': 'grad_w', 'grad_od_w_in': 'grad_w', 'grad_od_conv': 'grad_w', 'grad_od_A_log': 'grad_w', 'grad_od_dt_bias': 'grad_w', 'grad_od_norm': 'grad_w', 'grad_od_w_out': 'grad_w', 'grad_ffn_w13': 'grad_w', 'grad_ffn_w2': 'grad_w', 'delta_c_ctx': 'delta_w', 'delta_w_ada': 'delta_w', 'delta_b_ada': 'delta_w', 'delta_norm_w': 'delta_w', 'delta_ev_w_in': 'delta_w', 'delta_ev_lb': 'delta_w', 'delta_ev_a_norm': 'delta_w', 'delta_ev_pool_w': 'delta_w', 'delta_ev_pool_scale': 'delta_w', 'delta_ev_w_out': 'delta_w', 'delta_od_w_in': 'delta_w', 'delta_od_conv': 'delta_w', 'delta_od_A_log': 'delta_w', 'delta_od_dt_bias': 'delta_w', 'delta_od_norm': 'delta_w', 'delta_od_w_out': 'delta_w', 'delta_ffn_w13': 'delta_w', 'delta_ffn_w2': 'delta_w', 'new_m_c_ctx': 'new_m', 'new_m_w_ada': 'new_m', 'new_m_b_ada': 'new_m', 'new_m_norm_w': 'new_m', 'new_m_ev_w_in': 'new_m', 'new_m_ev_lb': 'new_m', 'new_m_ev_a_norm': 'new_m', 'new_m_ev_pool_w': 'new_m', 'new_m_ev_pool_scale': 'new_m', 'new_m_ev_w_out': 'new_m', 'new_m_od_w_in': 'new_m', 'new_m_od_conv': 'new_m', 'new_m_od_A_log': 'new_m', 'new_m_od_dt_bias': 'new_m', 'new_m_od_norm': 'new_m', 'new_m_od_w_out': 'new_m', 'new_m_ffn_w13': 'new_m', 'new_m_ffn_w2': 'new_m', 'new_v_c_ctx': 'new_v', 'new_v_w_ada': 'new_v', 'new_v_b_ada': 'new_v', 'new_v_norm_w': 'new_v', 'new_v_ev_w_in': 'new_v', 'new_v_ev_lb': 'new_v', 'new_v_ev_a_norm': 'new_v', 'new_v_ev_pool_w': 'new_v', 'new_v_ev_pool_scale': 'new_v', 'new_v_ev_w_out': 'new_v', 'new_v_od_w_in': 'new_v', 'new_v_od_conv': 'new_v', 'new_v_od_A_log': 'new_v', 'new_v_od_dt_bias': 'new_v', 'new_v_od_norm': 'new_v', 'new_v_od_w_out': 'new_v', 'new_v_ffn_w13': 'new_v', 'new_v_ffn_w2': 'new_v'}


def _forward(args):
    return _fwd_reference(*[args[k] for k in FWD_PARAMS])


def _output_shape():
    def fwd():
        inp = _fwd_setup_inputs(0)
        return _fwd_reference(*[inp[k] for k in FWD_PARAMS])
    out = _jax.eval_shape(fwd)
    return out.shape, out.dtype

N_MICROBATCH = 1
ADAM_LR = 0.001
ADAM_B1 = 0.9
ADAM_B2 = 0.999
ADAM_EPS = 1e-08
ADAM_WD = 0.01
ADAM_STEP = 10
PER_EXAMPLE_BATCH_AXIS = {'x': 0, 'c': 0, 'ctx': 0, 'loss_target': 0}
SHARED_INPUTS = []
_WEIGHT_DTYPES = {'c_ctx': _jnp.float32, 'w_ada': _jnp.float32, 'b_ada': _jnp.float32, 'norm_w': _jnp.float32, 'ev_w_in': _jnp.float32, 'ev_lb': _jnp.float32, 'ev_a_norm': _jnp.float32, 'ev_pool_w': _jnp.float32, 'ev_pool_scale': _jnp.float32, 'ev_w_out': _jnp.float32, 'od_w_in': _jnp.float32, 'od_conv': _jnp.float32, 'od_A_log': _jnp.float32, 'od_dt_bias': _jnp.float32, 'od_norm': _jnp.float32, 'od_w_out': _jnp.float32, 'ffn_w13': _jnp.float32, 'ffn_w2': _jnp.float32}
MOMENT_SCALE = {'c_ctx': 1.031522e-02, 'w_ada': 1.164683e+00, 'b_ada': 2.527591e+00, 'norm_w': 2.274091e+00, 'ev_w_in': 6.155291e-02, 'ev_lb': 1.371831e-03, 'ev_a_norm': 2.382065e-01, 'ev_pool_w': 1.061375e-01, 'ev_pool_scale': 1.094828e-01, 'ev_w_out': 9.777415e-02, 'od_w_in': 5.341375e-02, 'od_conv': 5.836232e-02, 'od_A_log': 1.557995e-01, 'od_dt_bias': 1.507577e-01, 'od_norm': 4.014864e-01, 'od_w_out': 1.330085e-01, 'ffn_w13': 4.381075e-02, 'ffn_w2': 7.651385e-02}


def _to_microbatches(a, axis):
    t = _jnp.moveaxis(a, axis, 0)
    t = t.reshape((N_MICROBATCH, t.shape[0] // N_MICROBATCH) + t.shape[1:])
    return _jnp.moveaxis(t, 1, axis + 1)


def setup_inputs(seed: int = 0) -> dict:
    inp = _fwd_setup_inputs(seed)
    key = _jax.random.fold_in(_jax.random.key(seed), 7919)
    shape, _ = _output_shape()
    out = dict(inp)
    out["loss_target"] = _jax.random.normal(_jax.random.fold_in(key, 0), shape, _jnp.float32)
    for i, name in enumerate(TWIN_WEIGHTS):
        w = inp[name].astype(_jnp.float32)
        if MOMENT_SCALE is None:
            s = _jnp.sqrt(_jnp.mean(_jnp.square(w)) + 1e-30)
        else:
            s = MOMENT_SCALE[name]
        km, kv = _jax.random.split(_jax.random.fold_in(key, i + 1))
        out[name] = w
        out["m_" + name] = s * _jax.random.normal(km, w.shape, _jnp.float32)
        out["v_" + name] = (s * s) * _jax.random.uniform(kv, w.shape, _jnp.float32, 0.5, 1.5)
    if N_MICROBATCH > 1:
        for name, axis in PER_EXAMPLE_BATCH_AXIS.items():
            out[name] = _to_microbatches(out[name], axis)
    return {'x': out['x'], 'c': out['c'], 'ctx': out['ctx'], 'c_ctx': out['c_ctx'], 'w_ada': out['w_ada'], 'b_ada': out['b_ada'], 'norm_w': out['norm_w'], 'ev_w_in': out['ev_w_in'], 'ev_lb': out['ev_lb'], 'ev_a_norm': out['ev_a_norm'], 'ev_pool_w': out['ev_pool_w'], 'ev_pool_scale': out['ev_pool_scale'], 'ev_w_out': out['ev_w_out'], 'od_w_in': out['od_w_in'], 'od_conv': out['od_conv'], 'od_A_log': out['od_A_log'], 'od_dt_bias': out['od_dt_bias'], 'od_norm': out['od_norm'], 'od_w_out': out['od_w_out'], 'ffn_w13': out['ffn_w13'], 'ffn_w2': out['ffn_w2'], 'loss_target': out['loss_target'], 'm_c_ctx': out['m_c_ctx'], 'm_w_ada': out['m_w_ada'], 'm_b_ada': out['m_b_ada'], 'm_norm_w': out['m_norm_w'], 'm_ev_w_in': out['m_ev_w_in'], 'm_ev_lb': out['m_ev_lb'], 'm_ev_a_norm': out['m_ev_a_norm'], 'm_ev_pool_w': out['m_ev_pool_w'], 'm_ev_pool_scale': out['m_ev_pool_scale'], 'm_ev_w_out': out['m_ev_w_out'], 'm_od_w_in': out['m_od_w_in'], 'm_od_conv': out['m_od_conv'], 'm_od_A_log': out['m_od_A_log'], 'm_od_dt_bias': out['m_od_dt_bias'], 'm_od_norm': out['m_od_norm'], 'm_od_w_out': out['m_od_w_out'], 'm_ffn_w13': out['m_ffn_w13'], 'm_ffn_w2': out['m_ffn_w2'], 'v_c_ctx': out['v_c_ctx'], 'v_w_ada': out['v_w_ada'], 'v_b_ada': out['v_b_ada'], 'v_norm_w': out['v_norm_w'], 'v_ev_w_in': out['v_ev_w_in'], 'v_ev_lb': out['v_ev_lb'], 'v_ev_a_norm': out['v_ev_a_norm'], 'v_ev_pool_w': out['v_ev_pool_w'], 'v_ev_pool_scale': out['v_ev_pool_scale'], 'v_ev_w_out': out['v_ev_w_out'], 'v_od_w_in': out['v_od_w_in'], 'v_od_conv': out['v_od_conv'], 'v_od_A_log': out['v_od_A_log'], 'v_od_dt_bias': out['v_od_dt_bias'], 'v_od_norm': out['v_od_norm'], 'v_od_w_out': out['v_od_w_out'], 'v_ffn_w13': out['v_ffn_w13'], 'v_ffn_w2': out['v_ffn_w2']}


def _loss(weights, diff, rest, loss_target):
    with _jax.named_scope("forward"):
        args = {**rest, TWIN_DIFF_INPUT: diff, **{k: w.astype(_WEIGHT_DTYPES[k]) for k, w in weights.items()}}
        y = _forward(args)
    with _jax.named_scope("loss_head"):
        err = _jnp.square(y.astype(_jnp.float32) - loss_target)
        return 0.5 * _jnp.sum(_jnp.mean(err, axis=-1)) if err.ndim else 0.5 * err


def _adamw(w, g, m, v):
    m = ADAM_B1 * m + (1.0 - ADAM_B1) * g
    v = ADAM_B2 * v + (1.0 - ADAM_B2) * _jnp.square(g)
    m_hat = m / (1.0 - ADAM_B1 ** ADAM_STEP)
    v_hat = v / (1.0 - ADAM_B2 ** ADAM_STEP)
    delta = -ADAM_LR * (m_hat / (_jnp.sqrt(v_hat) + ADAM_EPS) + ADAM_WD * w)
    return delta, m, v


def reference(x, c, ctx, c_ctx, w_ada, b_ada, norm_w, ev_w_in, ev_lb, ev_a_norm, ev_pool_w, ev_pool_scale, ev_w_out, od_w_in, od_conv, od_A_log, od_dt_bias, od_norm, od_w_out, ffn_w13, ffn_w2, loss_target, m_c_ctx, m_w_ada, m_b_ada, m_norm_w, m_ev_w_in, m_ev_lb, m_ev_a_norm, m_ev_pool_w, m_ev_pool_scale, m_ev_w_out, m_od_w_in, m_od_conv, m_od_A_log, m_od_dt_bias, m_od_norm, m_od_w_out, m_ffn_w13, m_ffn_w2, v_c_ctx, v_w_ada, v_b_ada, v_norm_w, v_ev_w_in, v_ev_lb, v_ev_a_norm, v_ev_pool_w, v_ev_pool_scale, v_ev_w_out, v_od_w_in, v_od_conv, v_od_A_log, v_od_dt_bias, v_od_norm, v_od_w_out, v_ffn_w13, v_ffn_w2):
    given = dict(x=x, c=c, ctx=ctx, c_ctx=c_ctx, w_ada=w_ada, b_ada=b_ada, norm_w=norm_w, ev_w_in=ev_w_in, ev_lb=ev_lb, ev_a_norm=ev_a_norm, ev_pool_w=ev_pool_w, ev_pool_scale=ev_pool_scale, ev_w_out=ev_w_out, od_w_in=od_w_in, od_conv=od_conv, od_A_log=od_A_log, od_dt_bias=od_dt_bias, od_norm=od_norm, od_w_out=od_w_out, ffn_w13=ffn_w13, ffn_w2=ffn_w2, loss_target=loss_target, m_c_ctx=m_c_ctx, m_w_ada=m_w_ada, m_b_ada=m_b_ada, m_norm_w=m_norm_w, m_ev_w_in=m_ev_w_in, m_ev_lb=m_ev_lb, m_ev_a_norm=m_ev_a_norm, m_ev_pool_w=m_ev_pool_w, m_ev_pool_scale=m_ev_pool_scale, m_ev_w_out=m_ev_w_out, m_od_w_in=m_od_w_in, m_od_conv=m_od_conv, m_od_A_log=m_od_A_log, m_od_dt_bias=m_od_dt_bias, m_od_norm=m_od_norm, m_od_w_out=m_od_w_out, m_ffn_w13=m_ffn_w13, m_ffn_w2=m_ffn_w2, v_c_ctx=v_c_ctx, v_w_ada=v_w_ada, v_b_ada=v_b_ada, v_norm_w=v_norm_w, v_ev_w_in=v_ev_w_in, v_ev_lb=v_ev_lb, v_ev_a_norm=v_ev_a_norm, v_ev_pool_w=v_ev_pool_w, v_ev_pool_scale=v_ev_pool_scale, v_ev_w_out=v_ev_w_out, v_od_w_in=v_od_w_in, v_od_conv=v_od_conv, v_od_A_log=v_od_A_log, v_od_dt_bias=v_od_dt_bias, v_od_norm=v_od_norm, v_od_w_out=v_od_w_out, v_ffn_w13=v_ffn_w13, v_ffn_w2=v_ffn_w2)
    weights = {n: given[n] for n in TWIN_WEIGHTS}
    shared = {n: given[n] for n in SHARED_INPUTS}
    per_example = {n: given[n] for n in ['x', 'c', 'ctx']}
    grad_fn = _jax.value_and_grad(_loss, argnums=(0, 1))

    def one_microbatch(ex, loss_target):
        ex = dict(ex)
        diff = ex.pop(TWIN_DIFF_INPUT)
        return grad_fn(weights, diff, {**shared, **ex}, loss_target)

    if N_MICROBATCH == 1:
        loss, (grad_w, grad_x) = one_microbatch(per_example, given["loss_target"])
    else:
        def body(carry, xs):
            loss_sum, grad_sum = carry
            l_k, (gw_k, gx_k) = one_microbatch(xs[0], xs[1])
            with _jax.named_scope("update"):
                return (loss_sum + l_k, _jax.tree.map(_jnp.add, grad_sum, gw_k)), gx_k

        init = (_jnp.zeros((), _jnp.float32), _jax.tree.map(_jnp.zeros_like, weights))
        (loss, grad_w), grad_x = _jax.lax.scan(body, init, (per_example, given["loss_target"]))
    with _jax.named_scope("update"):
        delta_w, new_m, new_v = {}, {}, {}
        for n in TWIN_WEIGHTS:
            delta_w[n], new_m[n], new_v[n] = _adamw(weights[n], grad_w[n], given["m_" + n], given["v_" + n])
    return (loss, grad_x, *[grad_w[n] for n in TWIN_WEIGHTS], *[delta_w[n] for n in TWIN_WEIGHTS],
            *[new_m[n] for n in TWIN_WEIGHTS], *[new_v[n] for n in TWIN_WEIGHTS])
```

```python
import functools

import numpy as np
import jax
import jax.numpy as jnp
from jax import lax
from jax.experimental import pallas as pl
from jax.experimental.pallas import tpu as pltpu

F32 = jnp.float32
BF16 = jnp.bfloat16
MESH = pl.DeviceIdType.MESH

EPS = 1e-6
GRID_W = 64
A_HEAD_DIM = 128
A_CHUNK = 32
POOL_WINDOWS = (2, 4, 8, 16)
C_HEAD_DIM = 128
C_CONV = 4
C_CHUNK = 64

ADAM_LR = 0.001
ADAM_B1 = 0.9
ADAM_B2 = 0.999
ADAM_EPS = 1e-08
ADAM_WD = 0.01
ADAM_STEP = 10

N_DEV = 8
N_CHIP = 4
LANE = 128
SUBLANE = 8
VMEM_LIMIT = 48 * 1024 * 1024

NN = (((1,), (0,)), ((), ()))
NT = (((1,), (1,)), ((), ()))
TN = (((0,), (0,)), ((), ()))


def _tile(n, cap, mult=LANE):
    best = 0
    for d in range(mult, min(n, cap) + 1, mult):
        if n % d == 0:
            best = d
    return best or n


def _div_le(n, cap):
    return max(d for d in range(1, cap + 1) if n % d == 0)


def _params(*sem):
    return pltpu.CompilerParams(dimension_semantics=sem or None, vmem_limit_bytes=VMEM_LIMIT)


def _dot(a, b, dims):
    return lax.dot_general(a, b, dims, preferred_element_type=F32)


def _matmul(a, b, mode, name):
    if mode == "nn":
        (m, k), n = a.shape, b.shape[1]
    elif mode == "nt":
        (m, k), n = a.shape, b.shape[0]
    else:
        (k, m), n = a.shape, b.shape[1]
    tm, tn, tk = _tile(m, 1024), _tile(n, 1024), _tile(k, 2048)
    nk = k // tk
    dims = {"nn": NN, "nt": NT, "tn": TN}[mode]
    if mode == "tn":
        a_spec = pl.BlockSpec((tk, tm), lambda i, j, kk: (kk, i))
    else:
        a_spec = pl.BlockSpec((tm, tk), lambda i, j, kk: (i, kk))
    if mode == "nt":
        b_spec = pl.BlockSpec((tn, tk), lambda i, j, kk: (j, kk))
    else:
        b_spec = pl.BlockSpec((tk, tn), lambda i, j, kk: (kk, j))

    def body(a_ref, b_ref, o_ref):
        part = _dot(a_ref[...], b_ref[...], dims)
        if nk == 1:
            o_ref[...] = part
        else:
            kk = pl.program_id(2)

            @pl.when(kk == 0)
            def _():
                o_ref[...] = part

            @pl.when(kk > 0)
            def _():
                o_ref[...] += part

    return pl.pallas_call(
        body,
        out_shape=jax.ShapeDtypeStruct((m, n), F32),
        grid=(m // tm, n // tn, nk),
        in_specs=[a_spec, b_spec],
        out_specs=pl.BlockSpec((tm, tn), lambda i, j, kk: (i, j)),
        compiler_params=_params("parallel", "parallel", "arbitrary"),
        name=name,
    )(a, b)


@jax.custom_vjp
def _dense(h, w, hook):
    return _matmul(h.astype(BF16), w, "nn", "dense_fwd")


def _dense_fwd(h, w, hook):
    hb = h.astype(BF16)
    return _matmul(hb, w, "nn", "dense_fwd"), (hb, w)


def _dense_bwd(res, dy):
    hb, w = res
    dyb = dy.astype(BF16)
    return _matmul(dyb, w, "nt", "dense_dx"), None, _matmul(hb, dyb, "tn", "dense_dw")


_dense.defvjp(_dense_fwd, _dense_bwd)


def _rec_plan(n, nc, anti, backward):
    nb = _div_le(int(np.gcd(n, nc)), 12)
    ng, ncb = n // nb, nc // nb
    if not anti:
        order = (lambda g: ng - 1 - g) if backward else (lambda g: g)
        return nb, ng, order, backward
    if backward:
        return nb, ng, (lambda g: jnp.where(g < ng - ncb, ncb + g, g - (ng - ncb))), False
    return nb, ng, (lambda g: jnp.where(g < ncb, ncb - 1 - g, ng - 1 - (g - ncb))), True


def _rec_specs(shape, nb, order):
    return pl.BlockSpec((1, nb) + tuple(shape[2:]), lambda h, g: (h, order(g), 0, 0))


def _hgrn_rec_fwd(anti, nc, q_in, k_out, v, decay):
    hh, n, _, dk = q_in.shape
    dv = v.shape[-1]
    nb, ng, order, descending = _rec_plan(n, nc, anti, False)

    def body(q_ref, k_ref, v_ref, d_ref, o_ref, s_ref, st):
        @pl.when(pl.program_id(1) == 0)
        def _():
            st[...] = jnp.zeros_like(st)

        def step(i, carry):
            c = nb - 1 - i if descending else i
            s = st[...]
            s_ref[0, c] = s
            o_ref[0, c] = _dot(q_ref[0, c].astype(BF16), s.astype(BF16), NT)
            st[...] = s * d_ref[0, c] + _dot(v_ref[0, c].astype(BF16), k_ref[0, c].astype(BF16), TN)
            return carry

        lax.fori_loop(0, nb, step, 0)

    ins = (q_in, k_out, v, decay)
    outs = (jax.ShapeDtypeStruct(v.shape, F32), jax.ShapeDtypeStruct((hh, n, dv, dk), F32))
    return pl.pallas_call(
        body,
        out_shape=outs,
        grid=(hh, ng),
        in_specs=[_rec_specs(t.shape, nb, order) for t in ins],
        out_specs=[_rec_specs(t.shape, nb, order) for t in outs],
        scratch_shapes=[pltpu.VMEM((dv, dk), F32)],
        compiler_params=_params("parallel", "arbitrary"),
        name="hgrn_rec_fwd",
    )(*ins)


def _hgrn_rec_bwd(anti, nc, q_in, k_out, v, decay, states, do):
    hh, n, _, dk = q_in.shape
    dv = v.shape[-1]
    nb, ng, order, descending = _rec_plan(n, nc, anti, True)

    def body(q_ref, k_ref, v_ref, d_ref, s_ref, do_ref, dq_ref, dk_ref, dv_ref, dd_ref, dst):
        @pl.when(pl.program_id(1) == 0)
        def _():
            dst[...] = jnp.zeros_like(dst)

        def step(i, carry):
            c = nb - 1 - i if descending else i
            ds = dst[...]
            dsb = ds.astype(BF16)
            s = s_ref[0, c]
            dob = do_ref[0, c].astype(BF16)
            dq_ref[0, c] = _dot(dob, s.astype(BF16), NN)
            dk_ref[0, c] = _dot(v_ref[0, c].astype(BF16), dsb, NN)
            dv_ref[0, c] = _dot(k_ref[0, c].astype(BF16), dsb, NT)
            dd_ref[0, c] = jnp.sum(ds * s, axis=0, keepdims=True)
            dst[...] = ds * d_ref[0, c] + _dot(dob, q_ref[0, c].astype(BF16), TN)
            return carry

        lax.fori_loop(0, nb, step, 0)

    ins = (q_in, k_out, v, decay, states, do)
    outs = tuple(jax.ShapeDtypeStruct(t.shape, F32) for t in (q_in, k_out, v, decay))
    return pl.pallas_call(
        body,
        out_shape=outs,
        grid=(hh, ng),
        in_specs=[_rec_specs(t.shape, nb, order) for t in ins],
        out_specs=[_rec_specs(t.shape, nb, order) for t in outs],
        scratch_shapes=[pltpu.VMEM((dv, dk), F32)],
        compiler_params=_params("parallel", "arbitrary"),
        name="hgrn_rec_bwd",
    )(*ins)


@functools.partial(jax.custom_vjp, nondiff_argnums=(0, 1))
def _hgrn_rec(anti, nc, q_in, k_out, v, decay):
    return _hgrn_rec_fwd(anti, nc, q_in, k_out, v, decay)[0]


def _hgrn_rec_vjp_fwd(anti, nc, q_in, k_out, v, decay):
    o, states = _hgrn_rec_fwd(anti, nc, q_in, k_out, v, decay)
    return o, (q_in, k_out, v, decay, states)


def _hgrn_rec_vjp_bwd(anti, nc, res, do):
    return _hgrn_rec_bwd(anti, nc, *res, do)


_hgrn_rec.defvjp(_hgrn_rec_vjp_fwd, _hgrn_rec_vjp_bwd)


def _gdn_rec_fwd(anti, nc, u, w, qk, qd, kd, gl):
    hh, n, _, dk = w.shape
    dv = u.shape[-1]
    nb, ng, order, descending = _rec_plan(n, nc, anti, False)

    def body(u_ref, w_ref, qk_ref, qd_ref, kd_ref, gl_ref, o_ref, s_ref, vn_ref, st):
        @pl.when(pl.program_id(1) == 0)
        def _():
            st[...] = jnp.zeros_like(st)

        def step(i, carry):
            c = nb - 1 - i if descending else i
            s = st[...]
            sb = s.astype(BF16)
            s_ref[0, c] = s
            vn = u_ref[0, c] - _dot(w_ref[0, c].astype(BF16), sb, NT)
            vnb = vn.astype(BF16)
            vn_ref[0, c] = vn
            o_ref[0, c] = _dot(qd_ref[0, c].astype(BF16), sb, NT) + _dot(qk_ref[0, c].astype(BF16), vnb, NN)
            st[...] = s * gl_ref[0, c] + _dot(vnb, kd_ref[0, c].astype(BF16), TN)
            return carry

        lax.fori_loop(0, nb, step, 0)

    ins = (u, w, qk, qd, kd, gl)
    outs = (
        jax.ShapeDtypeStruct(u.shape, F32),
        jax.ShapeDtypeStruct((hh, n, dv, dk), F32),
        jax.ShapeDtypeStruct(u.shape, F32),
    )
    return pl.pallas_call(
        body,
        out_shape=outs,
        grid=(hh, ng),
        in_specs=[_rec_specs(t.shape, nb, order) for t in ins],
        out_specs=[_rec_specs(t.shape, nb, order) for t in outs],
        scratch_shapes=[pltpu.VMEM((dv, dk), F32)],
        compiler_params=_params("parallel", "arbitrary"),
        name="gdn_rec_fwd",
    )(*ins)


def _gdn_rec_bwd(anti, nc, w, qk, qd, kd, gl, states, vnew, do):
    hh, n, _, dk = w.shape
    dv = vnew.shape[-1]
    nb, ng, order, descending = _rec_plan(n, nc, anti, True)

    def body(w_ref, qk_ref, qd_ref, kd_ref, gl_ref, s_ref, vn_ref, do_ref,
             du_ref, dw_ref, dqk_ref, dqd_ref, dkd_ref, dgl_ref, dst):
        @pl.when(pl.program_id(1) == 0)
        def _():
            dst[...] = jnp.zeros_like(dst)

        def step(i, carry):
            c = nb - 1 - i if descending else i
            ds = dst[...]
            dsb = ds.astype(BF16)
            s = s_ref[0, c]
            sb = s.astype(BF16)
            vnb = vn_ref[0, c].astype(BF16)
            dob = do_ref[0, c].astype(BF16)
            dvn = _dot(qk_ref[0, c].astype(BF16), dob, TN) + _dot(kd_ref[0, c].astype(BF16), dsb, NT)
            dvnb = dvn.astype(BF16)
            du_ref[0, c] = dvn
            dw_ref[0, c] = -_dot(dvnb, sb, NN)
            dqk_ref[0, c] = _dot(dob, vnb, NT)
            dqd_ref[0, c] = _dot(dob, sb, NN)
            dkd_ref[0, c] = _dot(vnb, dsb, NN)
            dgl_ref[0, c] = jnp.sum(ds * s, axis=0, keepdims=True)
            dst[...] = (ds * gl_ref[0, c] + _dot(dob, qd_ref[0, c].astype(BF16), TN)
                        - _dot(dvnb, w_ref[0, c].astype(BF16), TN))
            return carry

        lax.fori_loop(0, nb, step, 0)

    ins = (w, qk, qd, kd, gl, states, vnew, do)
    outs = tuple(jax.ShapeDtypeStruct(t.shape, F32) for t in (vnew, w, qk, qd, kd, gl))
    return pl.pallas_call(
        body,
        out_shape=outs,
        grid=(hh, ng),
        in_specs=[_rec_specs(t.shape, nb, order) for t in ins],
        out_specs=[_rec_specs(t.shape, nb, order) for t in outs],
        scratch_shapes=[pltpu.VMEM((dv, dk), F32)],
        compiler_params=_params("parallel", "arbitrary"),
        name="gdn_rec_bwd",
    )(*ins)


@functools.partial(jax.custom_vjp, nondiff_argnums=(0, 1))
def _gdn_rec(anti, nc, u, w, qk, qd, kd, gl):
    return _gdn_rec_fwd(anti, nc, u, w, qk, qd, kd, gl)[0]


def _gdn_rec_vjp_fwd(anti, nc, u, w, qk, qd, kd, gl):
    o, states, vnew = _gdn_rec_fwd(anti, nc, u, w, qk, qd, kd, gl)
    return o, (w, qk, qd, kd, gl, states, vnew)


def _gdn_rec_vjp_bwd(anti, nc, res, do):
    return _gdn_rec_bwd(anti, nc, *res, do)


_gdn_rec.defvjp(_gdn_rec_vjp_fwd, _gdn_rec_vjp_bwd)


def _split(a):
    hi = a.astype(BF16)
    return hi, (a - hi.astype(F32)).astype(BF16)


def _dot3(a, b, dims=NN):
    ah, al = _split(a)
    bh, bl = _split(b)
    return _dot(ah, bh, dims) + (_dot(ah, bl, dims) + _dot(al, bh, dims))


INV_BASE = 8


def _tri_inv_call(body, ins, name):
    nb = _div_le(ins[0].shape[1], 4)
    spec = pl.BlockSpec((1, nb) + tuple(ins[0].shape[2:]), lambda h, g: (h, g, 0, 0))
    return pl.pallas_call(
        functools.partial(body, nb),
        out_shape=jax.ShapeDtypeStruct(ins[0].shape, F32),
        grid=(ins[0].shape[0], ins[0].shape[1] // nb),
        in_specs=[spec] * len(ins),
        out_specs=spec,
        compiler_params=_params("parallel", "parallel"),
        name=name,
    )(*ins)


def _tri_inv_fwd(a):
    size = a.shape[-1]

    def body(nb, a_ref, t_ref):
        row = lax.broadcasted_iota(jnp.int32, (size, size), 0)
        col = lax.broadcasted_iota(jnp.int32, (size, size), 1)
        eye = jnp.where(row == col, 1.0, 0.0).astype(F32)

        def same_block(width):
            shift = width.bit_length() - 1
            return jnp.right_shift(row, shift) == jnp.right_shift(col, shift)

        for c in range(nb):
            m = a_ref[0, c]
            base = jnp.where(same_block(INV_BASE), m, 0.0)
            sq = _dot3(base, base)
            inv = _dot3(eye - base, eye + sq)
            width = 4
            while width < INV_BASE:
                sq = _dot3(sq, sq)
                inv = _dot3(inv, eye + sq)
                width *= 2
            width = INV_BASE
            while width < size:
                off = jnp.where(same_block(2 * width) & jnp.logical_not(same_block(width)), m, 0.0)
                inv = inv - _dot3(_dot3(inv, off), inv)
                width *= 2
            t_ref[0, c] = inv

    return _tri_inv_call(body, (a,), "tri_inv_fwd")


def _tri_inv_bwd(t, dt):
    def body(nb, t_ref, dt_ref, da_ref):
        for c in range(nb):
            tm = t_ref[0, c]
            da_ref[0, c] = -_dot3(_dot3(tm, dt_ref[0, c], TN), tm, NT)

    return _tri_inv_call(body, (t, dt), "tri_inv_bwd")


@jax.custom_vjp
def _tri_inv(a):
    return _tri_inv_fwd(a)


def _tri_inv_vjp_fwd(a):
    t = _tri_inv_fwd(a)
    return t, t


def _tri_inv_vjp_bwd(t, dt):
    return (_tri_inv_bwd(t, dt),)


_tri_inv.defvjp(_tri_inv_vjp_fwd, _tri_inv_vjp_bwd)


def _loss_head(y, target):
    rows, d = y.shape
    tr = _tile(rows, 256, SUBLANE)

    def body(y_ref, t_ref, dy_ref, sq_ref):
        diff = y_ref[...] - t_ref[...]
        dy_ref[...] = diff * (1.0 / d)
        part = jnp.sum((diff * diff).reshape(tr // SUBLANE, SUBLANE, d), axis=0)

        @pl.when(pl.program_id(0) == 0)
        def _():
            sq_ref[...] = part

        @pl.when(pl.program_id(0) > 0)
        def _():
            sq_ref[...] += part

    row = pl.BlockSpec((tr, d), lambda i: (i, 0))
    return pl.pallas_call(
        body,
        out_shape=(jax.ShapeDtypeStruct((rows, d), F32), jax.ShapeDtypeStruct((SUBLANE, d), F32)),
        grid=(rows // tr,),
        in_specs=[row, row],
        out_specs=[row, pl.BlockSpec((SUBLANE, d), lambda i: (0, 0))],
        compiler_params=_params("arbitrary"),
        name="loss_head",
    )(y, target)


def _add_half(g, r1, my_c):
    nblk, half, cols = r1.shape
    tr = _tile(half, 128, SUBLANE)
    per_half = half // tr

    def body(c_ref, g_ref, r_ref, o_ref):
        o_ref[...] = g_ref[...] + r_ref[...]

    grid_spec = pltpu.PrefetchScalarGridSpec(
        num_scalar_prefetch=1,
        grid=(nblk, per_half),
        in_specs=[
            pl.BlockSpec((1, tr, cols), lambda b, i, c_ref: (b, c_ref[0] * per_half + i, 0)),
            pl.BlockSpec((1, tr, cols), lambda b, i, c_ref: (b, i, 0)),
        ],
        out_specs=pl.BlockSpec((1, tr, cols), lambda b, i, c_ref: (b, i, 0)),
    )
    return pl.pallas_call(
        body,
        out_shape=jax.ShapeDtypeStruct(r1.shape, F32),
        grid_spec=grid_spec,
        compiler_params=_params("parallel", "parallel"),
        name="grad_add_half",
    )(my_c.reshape(1).astype(jnp.int32), g, r1)


def _sum_leading(x, name):
    k, rows, cols = x.shape
    tr = _tile(rows, 128, SUBLANE)

    def body(x_ref, o_ref):
        parts = [x_ref[i] for i in range(k)]
        while len(parts) > 1:
            parts = [parts[i] + parts[i + 1] for i in range(0, len(parts), 2)]
        o_ref[...] = parts[0]

    return pl.pallas_call(
        body,
        out_shape=jax.ShapeDtypeStruct((rows, cols), F32),
        grid=(rows // tr,),
        in_specs=[pl.BlockSpec((k, tr, cols), lambda i: (0, i, 0))],
        out_specs=pl.BlockSpec((tr, cols), lambda i: (i, 0)),
        compiler_params=_params("parallel"),
        name=name,
    )(x)


def _adamw(w, g, m, v, name):
    rows, cols = w.shape
    tr = _tile(rows, 128, SUBLANE)
    m_scale = 1.0 / (1.0 - ADAM_B1 ** ADAM_STEP)
    v_scale = 1.0 / (1.0 - ADAM_B2 ** ADAM_STEP)

    def body(w_ref, g_ref, m_ref, v_ref, d_ref, nm_ref, nv_ref):
        gg = g_ref[...]
        nm = ADAM_B1 * m_ref[...] + (1.0 - ADAM_B1) * gg
        nv = ADAM_B2 * v_ref[...] + (1.0 - ADAM_B2) * (gg * gg)
        nm_ref[...] = nm
        nv_ref[...] = nv
        d_ref[...] = -ADAM_LR * ((nm * m_scale) / (jnp.sqrt(nv * v_scale) + ADAM_EPS) + ADAM_WD * w_ref[...])

    spec = pl.BlockSpec((tr, cols), lambda i: (i, 0))
    out = jax.ShapeDtypeStruct((rows, cols), F32)
    return pl.pallas_call(
        body,
        out_shape=(out, out, out),
        grid=(rows // tr,),
        in_specs=[spec] * 4,
        out_specs=[spec] * 3,
        compiler_params=_params("parallel"),
        name=name,
    )(w, g, m, v)


def _ada_fwd(a16, w):
    layers, d, n = w.shape
    tn = _tile(n, 512)

    def body(a_ref, w_ref, o_ref):
        o_ref[0] = _dot(a_ref[...].astype(BF16), w_ref[0].astype(BF16), NN)

    return pl.pallas_call(
        body,
        out_shape=jax.ShapeDtypeStruct((layers, a16.shape[0], n), F32),
        grid=(layers, n // tn),
        in_specs=[pl.BlockSpec(a16.shape, lambda l, j: (0, 0)), pl.BlockSpec((1, d, tn), lambda l, j: (l, 0, j))],
        out_specs=pl.BlockSpec((1, a16.shape[0], tn), lambda l, j: (l, 0, j)),
        compiler_params=_params("parallel", "parallel"),
        name="ada_fwd",
    )(a16, w)


def _ada_bwd(a16, dm, w):
    layers, d, n = w.shape
    tn = _tile(n, 512)
    rows = a16.shape[0]

    def body(a_ref, dm_ref, w_ref, gw_ref, ga_ref):
        dmb = dm_ref[0].astype(BF16)
        gw_ref[0] = _dot(a_ref[...].astype(BF16), dmb, TN)
        part = _dot(dmb, w_ref[0].astype(BF16), NT)
        first = (pl.program_id(0) == 0) & (pl.program_id(1) == 0)

        @pl.when(first)
        def _():
            ga_ref[...] = part

        @pl.when(jnp.logical_not(first))
        def _():
            ga_ref[...] += part

    return pl.pallas_call(
        body,
        out_shape=(jax.ShapeDtypeStruct(w.shape, F32), jax.ShapeDtypeStruct((rows, d), F32)),
        grid=(layers, n // tn),
        in_specs=[
            pl.BlockSpec((rows, d), lambda l, j: (0, 0)),
            pl.BlockSpec((1, rows, tn), lambda l, j: (l, 0, j)),
            pl.BlockSpec((1, d, tn), lambda l, j: (l, 0, j)),
        ],
        out_specs=[pl.BlockSpec((1, d, tn), lambda l, j: (l, 0, j)), pl.BlockSpec((rows, d), lambda l, j: (0, 0))],
        compiler_params=_params("arbitrary", "arbitrary"),
        name="ada_bwd",
    )(a16, dm, w)


def _place():
    ix, iy, ic = lax.axis_index("x"), lax.axis_index("y"), lax.axis_index("c")
    return ix, iy, ic


def _flip(coord, bit):
    return 1 - coord if bit else coord


def _remote(src, dst, send_sem, recv_sem, to):
    return pltpu.make_async_remote_copy(
        src_ref=src, dst_ref=dst, send_sem=send_sem, recv_sem=recv_sem, device_id=to, device_id_type=MESH)


def _all_gather8(x, name):
    rows, cols = x.shape

    def body(x_ref, o_ref, send_sems, recv_sems):
        ix, iy, ic = _place()
        me = 4 * ix + 2 * iy + ic
        o_ref[me] = x_ref[...]
        peers = []
        for p in range(1, N_DEV):
            to = (_flip(ix, p & 4), _flip(iy, p & 2), _flip(ic, p & 1))
            peers.append((to, 4 * to[0] + 2 * to[1] + to[2]))
        sends = [_remote(x_ref, o_ref.at[me], send_sems.at[p], recv_sems.at[p], to)
                 for p, (to, _) in enumerate(peers)]
        for cp in sends:
            cp.start()
        for p, (to, slot) in enumerate(peers):
            _remote(x_ref, o_ref.at[slot], send_sems.at[p], recv_sems.at[p], to).wait_recv()
        for cp in sends:
            cp.wait_send()

    return pl.pallas_call(
        body,
        out_shape=jax.ShapeDtypeStruct((N_DEV, rows, cols), F32),
        in_specs=[pl.BlockSpec(memory_space=pltpu.VMEM)],
        out_specs=pl.BlockSpec(memory_space=pltpu.VMEM),
        scratch_shapes=[pltpu.SemaphoreType.DMA((N_DEV - 1,)), pltpu.SemaphoreType.DMA((N_DEV - 1,))],
        compiler_params=pltpu.CompilerParams(vmem_limit_bytes=VMEM_LIMIT),
        name=name,
    )(x)


def _other_chips(ix, iy):
    chips = [(1 - ix, iy), (ix, 1 - iy), (1 - ix, 1 - iy)]
    return [(cx, cy, 2 * cx + cy) for cx, cy in chips]


def _hbm_call(body, ins, out_shapes, scratch, name):
    hbm = pl.BlockSpec(memory_space=pltpu.HBM)
    return pl.pallas_call(
        body,
        out_shape=out_shapes,
        in_specs=[hbm] * len(ins),
        out_specs=[hbm] * len(out_shapes),
        scratch_shapes=scratch,
        compiler_params=pltpu.CompilerParams(vmem_limit_bytes=VMEM_LIMIT),
        name=name,
    )(*ins)


def _gather_weights(shards):
    nt = len(shards)

    def body(*refs):
        ins, outs = refs[:nt], refs[nt:2 * nt]
        local_sems, ici_send, ici_recv, d2d_send, d2d_recv = refs[2 * nt:]
        ix, iy, ic = _place()
        j = 2 * ix + iy
        sibling = (ix, iy, 1 - ic)
        chips = _other_chips(ix, iy)
        locals_, sends, waits = [], [], []
        for t in range(nt):
            half = ins[t].shape[0] // 2
            mine = pl.ds(ic * half, half)
            theirs = pl.ds((1 - ic) * half, half)
            cp = pltpu.make_async_copy(ins[t], outs[t].at[j], local_sems.at[t])
            cp.start()
            locals_.append(cp)
            for q, (cx, cy, _) in enumerate(chips):
                cp = _remote(ins[t].at[mine], outs[t].at[j, mine], ici_send.at[3 * t + q], ici_recv.at[3 * t + q],
                             (cx, cy, ic))
                cp.start()
                sends.append(cp)
        for q, (cx, cy, jq) in enumerate(chips):
            for t in range(nt):
                half = ins[t].shape[0] // 2
                mine = pl.ds(ic * half, half)
                theirs = pl.ds((1 - ic) * half, half)
                k = 3 * t + q
                _remote(ins[t].at[mine], outs[t].at[jq, mine], ici_send.at[k], ici_recv.at[k], (cx, cy, ic)).wait_recv()
                cp = _remote(outs[t].at[jq, mine], outs[t].at[jq, mine], d2d_send.at[k], d2d_recv.at[k], sibling)
                cp.start()
                sends.append(cp)
                waits.append(_remote(outs[t].at[jq, theirs], outs[t].at[jq, theirs], d2d_send.at[k], d2d_recv.at[k],
                                     sibling))
        for cp in waits:
            cp.wait_recv()
        for cp in sends:
            cp.wait_send()
        for cp in locals_:
            cp.wait()

    dma = pltpu.SemaphoreType.DMA
    outs = tuple(jax.ShapeDtypeStruct((N_CHIP,) + s.shape, s.dtype) for s in shards)
    return _hbm_call(body, shards, outs, [dma((nt,))] + [dma((3 * nt,))] * 4, "gather_weights")


def _exchange_halves(grads):
    nt = len(grads)

    def body(*refs):
        ins, outs = refs[:nt], refs[nt:2 * nt]
        send_sems, recv_sems = refs[2 * nt:]
        ix, iy, ic = _place()
        sibling = (ix, iy, 1 - ic)
        copies = []
        for t in range(nt):
            half = ins[t].shape[1] // 2
            theirs = pl.ds((1 - ic) * half, half)
            cp = _remote(ins[t].at[:, theirs, :], outs[t], send_sems.at[t], recv_sems.at[t], sibling)
            cp.start()
            copies.append(cp)
        for cp in copies:
            cp.wait_recv()
        for cp in copies:
            cp.wait_send()

    dma = pltpu.SemaphoreType.DMA
    outs = tuple(jax.ShapeDtypeStruct((g.shape[0], g.shape[1] // 2, g.shape[2]), F32) for g in grads)
    return _hbm_call(body, grads, outs, [dma((nt,)), dma((nt,))], "grad_exchange_halves")


def _exchange_blocks(parts):
    nt = len(parts)

    def body(*refs):
        ins, outs = refs[:nt], refs[nt:2 * nt]
        local_sems, send_sems, recv_sems = refs[2 * nt:]
        ix, iy, ic = _place()
        j = 2 * ix + iy
        chips = _other_chips(ix, iy)
        locals_, sends, waits = [], [], []
        for t in range(nt):
            cp = pltpu.make_async_copy(ins[t].at[j], outs[t].at[j], local_sems.at[t])
            cp.start()
            locals_.append(cp)
            for q, (cx, cy, jq) in enumerate(chips):
                k = 3 * t + q
                cp = _remote(ins[t].at[jq], outs[t].at[j], send_sems.at[k], recv_sems.at[k], (cx, cy, ic))
                cp.start()
                sends.append(cp)
                waits.append(_remote(ins[t].at[jq], outs[t].at[jq], send_sems.at[k], recv_sems.at[k], (cx, cy, ic)))
        for cp in waits:
            cp.wait_recv()
        for cp in sends:
            cp.wait_send()
        for cp in locals_:
            cp.wait()

    dma = pltpu.SemaphoreType.DMA
    outs = tuple(jax.ShapeDtypeStruct(p.shape, F32) for p in parts)
    return _hbm_call(body, parts, outs, [dma((nt,)), dma((3 * nt,)), dma((3 * nt,))], "grad_exchange_blocks")


def _share_halves(halves):
    nt = len(halves)

    def body(*refs):
        ins, outs = refs[:nt], refs[nt:2 * nt]
        local_sems, send_sems, recv_sems = refs[2 * nt:]
        ix, iy, ic = _place()
        sibling = (ix, iy, 1 - ic)
        locals_, sends, waits = [], [], []
        for t in range(nt):
            half = ins[t].shape[0]
            mine = pl.ds(ic * half, half)
            theirs = pl.ds((1 - ic) * half, half)
            cp = pltpu.make_async_copy(ins[t], outs[t].at[mine], local_sems.at[t])
            cp.start()
            locals_.append(cp)
            cp = _remote(ins[t], outs[t].at[mine], send_sems.at[t], recv_sems.at[t], sibling)
            cp.start()
            sends.append(cp)
            waits.append(_remote(ins[t], outs[t].at[theirs], send_sems.at[t], recv_sems.at[t], sibling))
        for cp in waits:
            cp.wait_recv()
        for cp in sends:
            cp.wait_send()
        for cp in locals_:
            cp.wait()

    dma = pltpu.SemaphoreType.DMA
    outs = tuple(jax.ShapeDtypeStruct((2 * h.shape[0], h.shape[1]), F32) for h in halves)
    return _hbm_call(body, halves, outs, [dma((nt,)), dma((nt,)), dma((nt,))], "grad_share_halves")


def _reduce_grads(grads, my_c):
    recv = _exchange_halves(grads)
    chip_sums = [_add_half(g, r1, my_c) for g, r1 in zip(grads, recv)]
    slots = _exchange_blocks(chip_sums)
    halves = [_sum_leading(s, "grad_sum_chips") for s in slots]
    return _share_halves(halves)


def _rmsnorm(x, w):
    return x * lax.rsqrt(jnp.mean(x * x, axis=-1, keepdims=True) + EPS) * w


def _l2norm(x):
    return x * lax.rsqrt(jnp.sum(x * x, axis=-1, keepdims=True) + EPS)


def _heads(t, d):
    return t.reshape(t.shape[:-1] + (t.shape[-1] // d, d))


def _chunks(t, chunk):
    n = t.shape[0] // chunk
    return t.reshape(n, chunk, t.shape[1], t.shape[2]).transpose(2, 0, 1, 3)


def _unchunks(t):
    hh, n, chunk, d = t.shape
    return t.transpose(1, 2, 0, 3).reshape(n * chunk, hh, d)


def _triangle(size, anti, strict=False):
    ones = jnp.ones((size, size), bool)
    return jnp.triu(ones, 1 if strict else 0) if anti else jnp.tril(ones, -1 if strict else 0)


def _hgrn2_scan(q, k, v, log_f, n_ctx, anti):
    q, k, v, log_f = (_chunks(t, A_CHUNK) for t in (q, k, v, log_f))
    b = lax.cumsum(log_f, axis=2, reverse=anti)
    end = 0 if anti else A_CHUNK - 1
    mid = A_CHUNK // 2 if anti else A_CHUNK // 2 - 1
    b_last = b[:, :, end:end + 1, :]
    b_mid = b[:, :, mid:mid + 1, :]
    scores = jnp.einsum("hntd,hnsd->hnts", q * jnp.exp(b - b_mid), k * jnp.exp(b_mid - b))
    o_intra = jnp.einsum("hnts,hnsv->hntv", jnp.where(_triangle(A_CHUNK, anti), scores, 0.0), v)
    o_inter = _hgrn_rec(anti, n_ctx // A_CHUNK, q * jnp.exp(b), k * jnp.exp(b_last - b), v, jnp.exp(b_last))
    return _unchunks(o_intra + o_inter)


def _gdn_scan(q, k, v, g, beta, n_ctx, anti):
    q, k, v = (_chunks(t, C_CHUNK) for t in (q, k, v))
    g = _chunks(g[..., None], C_CHUNK)[..., 0]
    beta = _chunks(beta[..., None], C_CHUNK)[..., 0]
    gc = lax.cumsum(g, axis=2, reverse=anti)
    end = 0 if anti else C_CHUNK - 1
    causal = _triangle(C_CHUNK, anti)
    diff = gc[..., :, None] - gc[..., None, :]
    decay = jnp.where(causal, jnp.exp(jnp.where(causal, diff, 0.0)), 0.0)
    k_beta = k * beta[..., None]
    v_beta = v * beta[..., None]
    a_tri = jnp.where(_triangle(C_CHUNK, anti, strict=True), jnp.einsum("hntd,hnsd->hnts", k_beta, k) * decay, 0.0)
    t_inv = _tri_inv(a_tri)
    u = jnp.einsum("hnts,hnsv->hntv", t_inv, v_beta)
    w = jnp.einsum("hnts,hnsd->hntd", t_inv, k_beta * jnp.exp(gc)[..., None])
    qk = jnp.where(causal, jnp.einsum("hntd,hnsd->hnts", q, k) * decay, 0.0)
    q_dec = q * jnp.exp(gc)[..., None]
    k_dec = k * jnp.exp(gc[..., end:end + 1] - gc)[..., None]
    g_last = jnp.exp(gc[..., end])
    gl = jnp.broadcast_to(g_last[..., None, None], g_last.shape + (1, q.shape[-1]))
    return _unchunks(_gdn_rec(anti, n_ctx // C_CHUNK, u, w, qk, q_dec, k_dec, gl))


def _bidirectional(scan_fn, fwd_args, bwd_args, n_ctx):
    return scan_fn(*fwd_args, n_ctx, False) + scan_fn(*bwd_args, n_ctx, True)


def _multiscale_pool(u, pool_w, pool_scale):
    rows, length, width = u.shape
    groups = len(POOL_WINDOWS)
    gdim = width // groups
    uf = u.reshape(rows, length, groups, gdim)
    cs = jnp.concatenate([jnp.zeros_like(uf[:, :1]), jnp.cumsum(uf, axis=1)], axis=1)
    pos = np.arange(length)
    mixed = []
    for gi, win in enumerate(POOL_WINDOWS):
        below, above = win // 2, win - win // 2
        lo = np.clip(pos - below, 0, length - 1)
        hi = np.clip(pos + win - 1 - below, 0, length - 1)
        cnt = jnp.asarray((hi - lo + 1).astype(np.float32))[None, :, None]
        csg = cs[:, :, gi, :]
        upper = jnp.concatenate([csg[:, above:]] + [csg[:, length:]] * (above - 1), axis=1)
        lower = jnp.concatenate([jnp.zeros_like(csg[:, :below]), csg[:, :length - below]], axis=1)
        mixed.append((upper - lower) / cnt - uf[:, :, gi, :])
    d = jnp.stack(mixed, axis=2)
    y = jnp.einsum("rlgc,gcd->rlgd", d, pool_w)
    return y.reshape(rows, length, width) * pool_scale


def _short_conv(u, w):
    length = u.shape[0]
    left = C_CONV // 2
    up = jnp.pad(u, ((left, C_CONV - 1 - left), (0, 0)))
    out = up[0:length] * w[0]
    for j in range(1, C_CONV):
        out = out + up[j:j + length] * w[j]
    return out


def _hgrn2_gates(pre_f, lb):
    log_f = jnp.log(lb + (1.0 - lb) * jax.nn.sigmoid(pre_f))
    k = (1.0 - lb) * jax.nn.sigmoid(-pre_f)
    return _heads(k, A_HEAD_DIM), _heads(log_f, A_HEAD_DIM)


def _even_mixer(h, n_ctx, lb, w_in, w_in_hook, a_norm, pool_w, pool_scale, w_out, w_out_hook):
    a_width = w_out.shape[0] // 2
    p = _dense(h, w_in, w_in_hook)
    q, f_f, f_b, i, g, u = jnp.split(p, [a_width * s for s in range(1, 6)], axis=-1)
    q = _heads(jax.nn.silu(q), A_HEAD_DIM)
    i = _heads(i, A_HEAD_DIM)
    k_f, logf_f = _hgrn2_gates(f_f, lb[0])
    k_b, logf_b = _hgrn2_gates(f_b, lb[1])
    o = _bidirectional(_hgrn2_scan, (q, k_f, i, logf_f), (q, k_b, i, logf_b), n_ctx)
    b_width = u.shape[-1]
    u_l = u[n_ctx:].reshape(-1, GRID_W, b_width)
    pooled = jnp.concatenate([
        _multiscale_pool(u[None, :n_ctx], pool_w, pool_scale)[0],
        _multiscale_pool(u_l, pool_w, pool_scale).reshape(-1, b_width),
    ], axis=0)
    a_out = _rmsnorm(o, a_norm) * jax.nn.silu(_heads(g, A_HEAD_DIM))
    a_out = a_out.reshape(a_out.shape[0], a_width)
    return _dense(jnp.concatenate([a_out, pooled], axis=-1), w_out, w_out_hook)


def _odd_mixer(h, n_ctx, w_main, w_main_hook, w_gate, w_gate_hook, conv_w, a_log, dt_bias, norm_w, w_out, w_out_hook):
    value_width = w_out.shape[0]
    key_width = value_width // 2
    rep = 2
    a_rate = jnp.exp(a_log)
    p = _dense(h, w_main, w_main_hook)
    gates = _dense(h, w_gate, w_gate_hook)
    qkv, z = p[:, :2 * key_width + value_width], p[:, 2 * key_width + value_width:]
    qkv = jnp.concatenate([_short_conv(qkv[:n_ctx], conv_w), _short_conv(qkv[n_ctx:], conv_w)], axis=0)
    qkv = jax.nn.silu(qkv)
    q, k, v = jnp.split(qkv, [key_width, 2 * key_width], axis=-1)
    q = jnp.repeat(_l2norm(_heads(q, C_HEAD_DIM)) * C_HEAD_DIM ** -0.5, rep, axis=1)
    k = jnp.repeat(_l2norm(_heads(k, C_HEAD_DIM)), rep, axis=1)
    v = _heads(v, C_HEAD_DIM)
    a_f, a_b, b_f, b_b = jnp.split(gates, 4, axis=-1)
    g_f = -a_rate[0] * jax.nn.softplus(a_f + dt_bias[0])
    g_b = -a_rate[1] * jax.nn.softplus(a_b + dt_bias[1])
    o = _bidirectional(_gdn_scan, (q, k, v, g_f, jax.nn.sigmoid(b_f)), (q, k, v, g_b, jax.nn.sigmoid(b_b)), n_ctx)
    y = _rmsnorm(o, norm_w) * jax.nn.silu(_heads(z, C_HEAD_DIM))
    return _dense(y.reshape(y.shape[0], value_width), w_out, w_out_hook)


def _swiglu(h, w13, w13_hook, w2, w2_hook):
    gate, up = jnp.split(_dense(h, w13, w13_hook), 2, axis=-1)
    return _dense(jax.nn.silu(gate) * up, w2, w2_hook)


def _forward(x, ctx, mods, big, hooks, small):
    n_ctx = ctx.shape[0]
    d = x.shape[-1]
    stream = jnp.concatenate([ctx, x], axis=0)
    is_ctx = (jnp.arange(stream.shape[0]) < n_ctx)[:, None]
    lb_all = jnp.cumsum(jax.nn.softmax(small["ev_lb"], axis=1), axis=1)
    for layer in range(2):
        m = [jnp.where(is_ctx, mods[layer, 1, s * d:(s + 1) * d][None, :], mods[layer, 0, s * d:(s + 1) * d][None, :])
             for s in range(6)]
        nw = small["norm_w"][layer]
        h = _rmsnorm(stream, nw[0]) * (1.0 + m[1]) + m[0]
        if layer == 0:
            y = _even_mixer(h, n_ctx, lb_all[:, layer], big["ev_w_in"], hooks["ev_w_in"], small["ev_a_norm"][0],
                            small["ev_pool_w"][0], small["ev_pool_scale"][0], big["ev_w_out"], hooks["ev_w_out"])
        else:
            y = _odd_mixer(h, n_ctx, big["od_w_main"], hooks["od_w_main"], big["od_w_gate"], hooks["od_w_gate"],
                           small["od_conv"][0], small["od_A_log"][0], small["od_dt_bias"][0], small["od_norm"][0],
                           big["od_w_out"], hooks["od_w_out"])
        stream = stream + m[2] * _rmsnorm(y, nw[1])
        h = _rmsnorm(stream, nw[2]) * (1.0 + m[4]) + m[3]
        f = _swiglu(h, big["ffn_w13_%d" % layer], hooks["ffn_w13_%d" % layer],
                    big["ffn_w2_%d" % layer], hooks["ffn_w2_%d" % layer])
        stream = stream + m[5] * _rmsnorm(f, nw[3])
    return stream[n_ctx:]


def _pack(arrays):
    flat = jnp.concatenate([a.reshape(-1) for a in arrays])
    pad = (-flat.shape[0]) % (SUBLANE * LANE)
    return jnp.pad(flat, (0, pad)).reshape(-1, LANE)


def _unpack(packed, shapes, lead=()):
    flat = packed.reshape(lead + (-1,))
    out, at = [], 0
    for shape in shapes:
        size = int(np.prod(shape))
        out.append(flat[..., at:at + size].reshape(lead + tuple(shape)))
        at += size
    return out


def _from_chips(gathered, axis):
    return jnp.concatenate([gathered[2 * j] for j in range(N_CHIP)], axis=axis)


def _col_natural(blocks):
    nblk, k, n = blocks.shape
    return blocks.transpose(1, 0, 2).reshape(k, nblk * n)


def _col_blocked(nat):
    k, n4 = nat.shape
    return nat.reshape(k, N_CHIP, n4 // N_CHIP).transpose(1, 0, 2)


def kernel(x, c, ctx, c_ctx, w_ada, b_ada, norm_w, ev_w_in, ev_lb, ev_a_norm, ev_pool_w, ev_pool_scale, ev_w_out, od_w_in, od_conv, od_A_log, od_dt_bias, od_norm, od_w_out, ffn_w13, ffn_w2, loss_target, m_c_ctx, m_w_ada, m_b_ada, m_norm_w, m_ev_w_in, m_ev_lb, m_ev_a_norm, m_ev_pool_w, m_ev_pool_scale, m_ev_w_out, m_od_w_in, m_od_conv, m_od_A_log, m_od_dt_bias, m_od_norm, m_od_w_out, m_ffn_w13, m_ffn_w2, v_c_ctx, v_w_ada, v_b_ada, v_norm_w, v_ev_w_in, v_ev_lb, v_ev_a_norm, v_ev_pool_w, v_ev_pool_scale, v_ev_w_out, v_od_w_in, v_od_conv, v_od_A_log, v_od_dt_bias, v_od_norm, v_od_w_out, v_ffn_w13, v_ffn_w2):
    ix, iy, ic = _place()
    me = 4 * ix + 2 * iy + ic
    chip = 2 * ix + iy
    d = x.shape[-1]
    layers = w_ada.shape[0]
    n_ada = w_ada.shape[-1]

    pre_parts = [c[0], norm_w, ev_lb, ev_pool_w, od_conv]
    pre = _all_gather8(_pack(pre_parts), "gather_small_inputs")
    c_all, norm_w_g, ev_lb_g, pool_w_g, od_conv_g = _unpack(pre, [p.shape for p in pre_parts], lead=(N_DEV,))
    small = {
        "norm_w": _from_chips(norm_w_g, 2),
        "ev_lb": _from_chips(ev_lb_g, 2),
        "ev_a_norm": ev_a_norm,
        "ev_pool_w": _from_chips(pool_w_g, 2),
        "ev_pool_scale": ev_pool_scale,
        "od_conv": _from_chips(od_conv_g, 2),
        "od_A_log": od_A_log,
        "od_dt_bias": od_dt_bias,
        "od_norm": od_norm,
    }

    silu_cc, silu_cc_vjp = jax.vjp(jax.nn.silu, c_ctx)
    a16 = jnp.concatenate([jax.nn.silu(c_all), silu_cc[None], jnp.zeros((2 * SUBLANE - N_DEV - 1, d), F32)], axis=0)
    mods_local = _ada_fwd(a16, w_ada)
    mods_g = _all_gather8(mods_local.reshape(-1, n_ada), "gather_modulation")
    mods_g = mods_g.reshape(N_DEV, layers, 2 * SUBLANE, n_ada)
    mods_full = _from_chips(mods_g, 2) + b_ada[:, None, :]
    mods = jnp.stack([lax.dynamic_index_in_dim(mods_full, me, axis=1, keepdims=False), mods_full[:, N_DEV]], axis=1)

    ffn13_rows = ffn_w13.shape[1]
    ffn2_rows = ffn_w2.shape[1]
    shards = [ev_w_in[0], ev_w_out[0], od_w_in[0], od_w_out[0],
              ffn_w13.reshape(-1, ffn_w13.shape[-1]), ffn_w2.reshape(-1, ffn_w2.shape[-1])]
    g_ev_in, g_ev_out, g_od_in, g_od_out, g_w13, g_w2 = _gather_weights([s.astype(BF16) for s in shards])
    od_nat = _col_natural(g_od_in)
    n_gate = 4 * (od_w_out.shape[1] * N_CHIP // C_HEAD_DIM)
    n_main = od_nat.shape[1] - n_gate
    big = {
        "ev_w_in": _col_natural(g_ev_in),
        "ev_w_out": g_ev_out.reshape(-1, g_ev_out.shape[-1]),
        "od_w_main": od_nat[:, :n_main],
        "od_w_gate": od_nat[:, n_main:],
        "od_w_out": g_od_out.reshape(-1, g_od_out.shape[-1]),
    }
    for layer in range(layers):
        big["ffn_w13_%d" % layer] = _col_natural(g_w13[:, layer * ffn13_rows:(layer + 1) * ffn13_rows])
        big["ffn_w2_%d" % layer] = g_w2[:, layer * ffn2_rows:(layer + 1) * ffn2_rows].reshape(-1, g_w2.shape[-1])
    hooks = {name: jnp.zeros(w.shape, F32) for name, w in big.items()}

    def local_forward(x_, mods_, hooks_, small_):
        return _forward(x_, ctx[0], mods_, big, hooks_, small_)

    y, pullback = jax.vjp(local_forward, x[0], mods, hooks, small)
    dy, sq = _loss_head(y, loss_target[0])
    loss = lax.psum(jnp.sum(sq) * (0.5 / d), ("x", "y", "c"))
    grad_x, d_mods, d_big, d_small = pullback(dy)

    d_od_in = jnp.concatenate([d_big["od_w_main"], d_big["od_w_gate"]], axis=1)
    blocked = [
        _col_blocked(d_big["ev_w_in"]),
        d_big["ev_w_out"].reshape(N_CHIP, -1, d_big["ev_w_out"].shape[-1]),
        _col_blocked(d_od_in),
        d_big["od_w_out"].reshape(N_CHIP, -1, d_big["od_w_out"].shape[-1]),
        jnp.concatenate([_col_blocked(d_big["ffn_w13_%d" % layer]) for layer in range(layers)], axis=1),
        jnp.concatenate([d_big["ffn_w2_%d" % layer].reshape(N_CHIP, -1, d_big["ffn_w2_%d" % layer].shape[-1])
                         for layer in range(layers)], axis=1),
    ]
    r_ev_in, r_ev_out, r_od_in, r_od_out, r_w13, r_w2 = _reduce_grads(blocked, ic)

    small_names = ["norm_w", "ev_lb", "ev_a_norm", "ev_pool_w", "ev_pool_scale", "od_conv", "od_A_log",
                   "od_dt_bias", "od_norm"]
    post_parts = [d_mods[:, 0], d_mods[:, 1]] + [d_small[n] for n in small_names]
    post = _all_gather8(_pack(post_parts), "gather_small_grads")
    post_sum = _sum_leading(post, "sum_small_grads")
    dm_l_all = _unpack(post, [post_parts[0].shape], lead=(N_DEV,))[0]
    summed = _unpack(post_sum, [p.shape for p in post_parts])
    dm_l_sum, dm_c_sum = summed[0], summed[1]
    g_small = dict(zip(small_names, summed[2:]))
    grad_b_ada = dm_l_sum + dm_c_sum

    def my_cols(full, width):
        return lax.dynamic_slice_in_dim(full, chip * width, width, axis=full.ndim - 1)

    dm_rows = jnp.concatenate([
        my_cols(dm_l_all, n_ada).transpose(1, 0, 2),
        my_cols(dm_c_sum, n_ada)[:, None, :],
        jnp.zeros((layers, 2 * SUBLANE - N_DEV - 1, n_ada), F32),
    ], axis=1)
    grad_w_ada, ga = _ada_bwd(a16, dm_rows, w_ada)
    ga_g = _all_gather8(jnp.pad(ga[N_DEV][None], ((0, SUBLANE - 1), (0, 0))), "gather_c_ctx_grad")
    g_silu_cc = _sum_leading(jnp.stack([ga_g[2 * j] for j in range(N_CHIP)]), "sum_c_ctx_grad")[0]
    grad_c_ctx = silu_cc_vjp(g_silu_cc)[0]

    grads = {
        "c_ctx": grad_c_ctx,
        "w_ada": grad_w_ada,
        "b_ada": grad_b_ada,
        "norm_w": my_cols(g_small["norm_w"], norm_w.shape[-1]),
        "ev_w_in": r_ev_in[None],
        "ev_lb": my_cols(g_small["ev_lb"], ev_lb.shape[-1]),
        "ev_a_norm": g_small["ev_a_norm"],
        "ev_pool_w": lax.dynamic_slice_in_dim(g_small["ev_pool_w"], chip * ev_pool_w.shape[2], ev_pool_w.shape[2], axis=2),
        "ev_pool_scale": g_small["ev_pool_scale"],
        "ev_w_out": r_ev_out[None],
        "od_w_in": r_od_in[None],
        "od_conv": my_cols(g_small["od_conv"], od_conv.shape[-1]),
        "od_A_log": g_small["od_A_log"],
        "od_dt_bias": g_small["od_dt_bias"],
        "od_norm": g_small["od_norm"],
        "od_w_out": r_od_out[None],
        "ffn_w13": r_w13.reshape(ffn_w13.shape),
        "ffn_w2": r_w2.reshape(ffn_w2.shape),
    }

    weights = dict(c_ctx=c_ctx, w_ada=w_ada, b_ada=b_ada, norm_w=norm_w, ev_w_in=ev_w_in, ev_lb=ev_lb,
                   ev_a_norm=ev_a_norm, ev_pool_w=ev_pool_w, ev_pool_scale=ev_pool_scale, ev_w_out=ev_w_out,
                   od_w_in=od_w_in, od_conv=od_conv, od_A_log=od_A_log, od_dt_bias=od_dt_bias, od_norm=od_norm,
                   od_w_out=od_w_out, ffn_w13=ffn_w13, ffn_w2=ffn_w2)
    first = dict(c_ctx=m_c_ctx, w_ada=m_w_ada, b_ada=m_b_ada, norm_w=m_norm_w, ev_w_in=m_ev_w_in, ev_lb=m_ev_lb,
                 ev_a_norm=m_ev_a_norm, ev_pool_w=m_ev_pool_w, ev_pool_scale=m_ev_pool_scale, ev_w_out=m_ev_w_out,
                 od_w_in=m_od_w_in, od_conv=m_od_conv, od_A_log=m_od_A_log, od_dt_bias=m_od_dt_bias,
                 od_norm=m_od_norm, od_w_out=m_od_w_out, ffn_w13=m_ffn_w13, ffn_w2=m_ffn_w2)
    second = dict(c_ctx=v_c_ctx, w_ada=v_w_ada, b_ada=v_b_ada, norm_w=v_norm_w, ev_w_in=v_ev_w_in, ev_lb=v_ev_lb,
                  ev_a_norm=v_ev_a_norm, ev_pool_w=v_ev_pool_w, ev_pool_scale=v_ev_pool_scale, ev_w_out=v_ev_w_out,
                  od_w_in=v_od_w_in, od_conv=v_od_conv, od_A_log=v_od_A_log, od_dt_bias=v_od_dt_bias,
                  od_norm=v_od_norm, od_w_out=v_od_w_out, ffn_w13=v_ffn_w13, ffn_w2=v_ffn_w2)
    names = list(weights)
    large = ["w_ada", "ev_w_in", "ev_w_out", "od_w_in", "od_w_out", "ffn_w13", "ffn_w2"]
    little = [n for n in names if n not in large]
    delta, new_m, new_v = {}, {}, {}
    for n in large:
        as2d = lambda t: t.reshape(-1, t.shape[-1])
        out = _adamw(as2d(weights[n]), as2d(grads[n]), as2d(first[n]), as2d(second[n]), "adamw_" + n)
        delta[n], new_m[n], new_v[n] = (t.reshape(weights[n].shape) for t in out)
    shapes = [weights[n].shape for n in little]
    out = _adamw(*(_pack([src[n] for n in little]) for src in (weights, grads, first, second)), "adamw_small")
    for res, packed in zip((delta, new_m, new_v), out):
        res.update(zip(little, _unpack(packed, shapes)))

    return (loss, grad_x[None], *[grads[n] for n in names], *[delta[n] for n in names],
            *[new_m[n] for n in names], *[new_v[n] for n in names])
```

```python
import functools

import numpy as np
import jax
import jax.numpy as jnp
from jax import lax
from jax.experimental import pallas as pl
from jax.experimental.pallas import tpu as pltpu

F32 = jnp.float32
BF16 = jnp.bfloat16
MESH = pl.DeviceIdType.MESH

EPS = 1e-6
GRID_W = 64
A_HEAD_DIM = 128
A_CHUNK = 32
POOL_WINDOWS = (2, 4, 8, 16)
C_HEAD_DIM = 128
C_CONV = 4
C_CHUNK = 64

ADAM_LR = 0.001
ADAM_B1 = 0.9
ADAM_B2 = 0.999
ADAM_EPS = 1e-08
ADAM_WD = 0.01
ADAM_STEP = 10

N_DEV = 8
N_CHIP = 4
LANE = 128
SUBLANE = 8
VMEM_LIMIT = 48 * 1024 * 1024

NN = (((1,), (0,)), ((), ()))
NT = (((1,), (1,)), ((), ()))
TN = (((0,), (0,)), ((), ()))


def _tile(n, cap, mult=LANE):
    best = 0
    for d in range(mult, min(n, cap) + 1, mult):
        if n % d == 0:
            best = d
    return best or n


def _div_le(n, cap):
    return max(d for d in range(1, cap + 1) if n % d == 0)


def _params(*sem):
    return pltpu.CompilerParams(dimension_semantics=sem or None, vmem_limit_bytes=VMEM_LIMIT)


def _dot(a, b, dims):
    return lax.dot_general(a, b, dims, preferred_element_type=F32)


def _matmul(a, b, mode, name):
    if mode == "nn":
        (m, k), n = a.shape, b.shape[1]
    elif mode == "nt":
        (m, k), n = a.shape, b.shape[0]
    else:
        (k, m), n = a.shape, b.shape[1]
    tm, tn, tk = _tile(m, 1024), _tile(n, 1024), _tile(k, 2048)
    nk = k // tk
    dims = {"nn": NN, "nt": NT, "tn": TN}[mode]
    if mode == "tn":
        a_spec = pl.BlockSpec((tk, tm), lambda i, j, kk: (kk, i))
    else:
        a_spec = pl.BlockSpec((tm, tk), lambda i, j, kk: (i, kk))
    if mode == "nt":
        b_spec = pl.BlockSpec((tn, tk), lambda i, j, kk: (j, kk))
    else:
        b_spec = pl.BlockSpec((tk, tn), lambda i, j, kk: (kk, j))

    def body(a_ref, b_ref, o_ref):
        part = _dot(a_ref[...], b_ref[...], dims)
        if nk == 1:
            o_ref[...] = part
        else:
            kk = pl.program_id(2)

            @pl.when(kk == 0)
            def _():
                o_ref[...] = part

            @pl.when(kk > 0)
            def _():
                o_ref[...] += part

    return pl.pallas_call(
        body,
        out_shape=jax.ShapeDtypeStruct((m, n), F32),
        grid=(m // tm, n // tn, nk),
        in_specs=[a_spec, b_spec],
        out_specs=pl.BlockSpec((tm, tn), lambda i, j, kk: (i, j)),
        compiler_params=_params("parallel", "parallel", "arbitrary"),
        name=name,
    )(a, b)


@jax.custom_vjp
def _dense(h, w, hook):
    return _matmul(h.astype(BF16), w, "nn", "dense_fwd")


def _dense_fwd(h, w, hook):
    hb = h.astype(BF16)
    return _matmul(hb, w, "nn", "dense_fwd"), (hb, w)


def _dense_bwd(res, dy):
    hb, w = res
    dyb = dy.astype(BF16)
    return _matmul(dyb, w, "nt", "dense_dx"), None, _matmul(hb, dyb, "tn", "dense_dw")


_dense.defvjp(_dense_fwd, _dense_bwd)


def _rec_plan(n, nc, anti, backward):
    nb = _div_le(int(np.gcd(n, nc)), 12)
    ng, ncb = n // nb, nc // nb
    if not anti:
        order = (lambda g: ng - 1 - g) if backward else (lambda g: g)
        return nb, ng, order, backward
    if backward:
        return nb, ng, (lambda g: jnp.where(g < ng - ncb, ncb + g, g - (ng - ncb))), False
    return nb, ng, (lambda g: jnp.where(g < ncb, ncb - 1 - g, ng - 1 - (g - ncb))), True


def _rec_specs(shape, nb, order):
    return pl.BlockSpec((1, nb) + tuple(shape[2:]), lambda h, g: (h, order(g), 0, 0))


def _within_chunk(scores, anti):
    row = lax.broadcasted_iota(jnp.int32, scores.shape, 0)
    col = lax.broadcasted_iota(jnp.int32, scores.shape, 1)
    return jnp.where(row <= col if anti else row >= col, scores, 0.0)


def _hgrn_rec_fwd(anti, nc, q_in, k_out, v, decay, qs, ks):
    hh, n, _, dk = q_in.shape
    dv = v.shape[-1]
    nb, ng, order, descending = _rec_plan(n, nc, anti, False)

    def body(q_ref, k_ref, v_ref, d_ref, qs_ref, ks_ref, o_ref, s_ref, st):
        @pl.when(pl.program_id(1) == 0)
        def _():
            st[...] = jnp.zeros_like(st)

        def step(i, carry):
            c = nb - 1 - i if descending else i
            s = st[...]
            s_ref[0, c] = s
            vb = v_ref[0, c].astype(BF16)
            scores = _within_chunk(_dot(qs_ref[0, c].astype(BF16), ks_ref[0, c].astype(BF16), NT), anti)
            o_ref[0, c] = _dot(q_ref[0, c].astype(BF16), s.astype(BF16), NT) + _dot(scores.astype(BF16), vb, NN)
            st[...] = s * d_ref[0, c] + _dot(vb, k_ref[0, c].astype(BF16), TN)
            return carry

        lax.fori_loop(0, nb, step, 0)

    ins = (q_in, k_out, v, decay, qs, ks)
    outs = (jax.ShapeDtypeStruct(v.shape, F32), jax.ShapeDtypeStruct((hh, n, dv, dk), F32))
    return pl.pallas_call(
        body,
        out_shape=outs,
        grid=(hh, ng),
        in_specs=[_rec_specs(t.shape, nb, order) for t in ins],
        out_specs=[_rec_specs(t.shape, nb, order) for t in outs],
        scratch_shapes=[pltpu.VMEM((dv, dk), F32)],
        compiler_params=_params("parallel", "arbitrary"),
        name="hgrn_rec_fwd",
    )(*ins)


def _hgrn_rec_bwd(anti, nc, q_in, k_out, v, decay, qs, ks, states, do):
    hh, n, _, dk = q_in.shape
    dv = v.shape[-1]
    nb, ng, order, descending = _rec_plan(n, nc, anti, True)

    def body(q_ref, k_ref, v_ref, d_ref, qs_ref, ks_ref, s_ref, do_ref,
             dq_ref, dk_ref, dv_ref, dd_ref, dqs_ref, dks_ref, dst):
        @pl.when(pl.program_id(1) == 0)
        def _():
            dst[...] = jnp.zeros_like(dst)

        def step(i, carry):
            c = nb - 1 - i if descending else i
            ds = dst[...]
            dsb = ds.astype(BF16)
            s = s_ref[0, c]
            dob = do_ref[0, c].astype(BF16)
            vb = v_ref[0, c].astype(BF16)
            qsb, ksb = qs_ref[0, c].astype(BF16), ks_ref[0, c].astype(BF16)
            scores = _within_chunk(_dot(qsb, ksb, NT), anti).astype(BF16)
            dscores = _within_chunk(_dot(dob, vb, NT), anti).astype(BF16)
            dqs_ref[0, c] = _dot(dscores, ksb, NN)
            dks_ref[0, c] = _dot(dscores, qsb, TN)
            dq_ref[0, c] = _dot(dob, s.astype(BF16), NN)
            dk_ref[0, c] = _dot(vb, dsb, NN)
            dv_ref[0, c] = _dot(k_ref[0, c].astype(BF16), dsb, NT) + _dot(scores, dob, TN)
            dd_ref[0, c] = jnp.sum(ds * s, axis=0, keepdims=True)
            dst[...] = ds * d_ref[0, c] + _dot(dob, q_ref[0, c].astype(BF16), TN)
            return carry

        lax.fori_loop(0, nb, step, 0)

    ins = (q_in, k_out, v, decay, qs, ks, states, do)
    outs = tuple(jax.ShapeDtypeStruct(t.shape, F32) for t in (q_in, k_out, v, decay, qs, ks))
    return pl.pallas_call(
        body,
        out_shape=outs,
        grid=(hh, ng),
        in_specs=[_rec_specs(t.shape, nb, order) for t in ins],
        out_specs=[_rec_specs(t.shape, nb, order) for t in outs],
        scratch_shapes=[pltpu.VMEM((dv, dk), F32)],
        compiler_params=_params("parallel", "arbitrary"),
        name="hgrn_rec_bwd",
    )(*ins)


@functools.partial(jax.custom_vjp, nondiff_argnums=(0, 1))
def _hgrn_rec(anti, nc, q_in, k_out, v, decay, qs, ks):
    return _hgrn_rec_fwd(anti, nc, q_in, k_out, v, decay, qs, ks)[0]


def _hgrn_rec_vjp_fwd(anti, nc, q_in, k_out, v, decay, qs, ks):
    o, states = _hgrn_rec_fwd(anti, nc, q_in, k_out, v, decay, qs, ks)
    return o, (q_in, k_out, v, decay, qs, ks, states)


def _hgrn_rec_vjp_bwd(anti, nc, res, do):
    return _hgrn_rec_bwd(anti, nc, *res, do)


_hgrn_rec.defvjp(_hgrn_rec_vjp_fwd, _hgrn_rec_vjp_bwd)


def _gdn_rec_fwd(anti, nc, u, w, qk, qd, kd, gl):
    hh, n, _, dk = w.shape
    dv = u.shape[-1]
    nb, ng, order, descending = _rec_plan(n, nc, anti, False)

    def body(u_ref, w_ref, qk_ref, qd_ref, kd_ref, gl_ref, o_ref, s_ref, vn_ref, st):
        @pl.when(pl.program_id(1) == 0)
        def _():
            st[...] = jnp.zeros_like(st)

        def step(i, carry):
            c = nb - 1 - i if descending else i
            s = st[...]
            sb = s.astype(BF16)
            s_ref[0, c] = s
            vn = u_ref[0, c] - _dot(w_ref[0, c].astype(BF16), sb, NT)
            vnb = vn.astype(BF16)
            vn_ref[0, c] = vn
            o_ref[0, c] = _dot(qd_ref[0, c].astype(BF16), sb, NT) + _dot(qk_ref[0, c].astype(BF16), vnb, NN)
            st[...] = s * gl_ref[0, c] + _dot(vnb, kd_ref[0, c].astype(BF16), TN)
            return carry

        lax.fori_loop(0, nb, step, 0)

    ins = (u, w, qk, qd, kd, gl)
    outs = (
        jax.ShapeDtypeStruct(u.shape, F32),
        jax.ShapeDtypeStruct((hh, n, dv, dk), F32),
        jax.ShapeDtypeStruct(u.shape, F32),
    )
    return pl.pallas_call(
        body,
        out_shape=outs,
        grid=(hh, ng),
        in_specs=[_rec_specs(t.shape, nb, order) for t in ins],
        out_specs=[_rec_specs(t.shape, nb, order) for t in outs],
        scratch_shapes=[pltpu.VMEM((dv, dk), F32)],
        compiler_params=_params("parallel", "arbitrary"),
        name="gdn_rec_fwd",
    )(*ins)


def _gdn_rec_bwd(anti, nc, w, qk, qd, kd, gl, states, vnew, do):
    hh, n, _, dk = w.shape
    dv = vnew.shape[-1]
    nb, ng, order, descending = _rec_plan(n, nc, anti, True)

    def body(w_ref, qk_ref, qd_ref, kd_ref, gl_ref, s_ref, vn_ref, do_ref,
             du_ref, dw_ref, dqk_ref, dqd_ref, dkd_ref, dgl_ref, dst):
        @pl.when(pl.program_id(1) == 0)
        def _():
            dst[...] = jnp.zeros_like(dst)

        def step(i, carry):
            c = nb - 1 - i if descending else i
            ds = dst[...]
            dsb = ds.astype(BF16)
            s = s_ref[0, c]
            sb = s.astype(BF16)
            vnb = vn_ref[0, c].astype(BF16)
            dob = do_ref[0, c].astype(BF16)
            dvn = _dot(qk_ref[0, c].astype(BF16), dob, TN) + _dot(kd_ref[0, c].astype(BF16), dsb, NT)
            dvnb = dvn.astype(BF16)
            du_ref[0, c] = dvn
            dw_ref[0, c] = -_dot(dvnb, sb, NN)
            dqk_ref[0, c] = _dot(dob, vnb, NT)
            dqd_ref[0, c] = _dot(dob, sb, NN)
            dkd_ref[0, c] = _dot(vnb, dsb, NN)
            dgl_ref[0, c] = jnp.sum(ds * s, axis=0, keepdims=True)
            dst[...] = (ds * gl_ref[0, c] + _dot(dob, qd_ref[0, c].astype(BF16), TN)
                        - _dot(dvnb, w_ref[0, c].astype(BF16), TN))
            return carry

        lax.fori_loop(0, nb, step, 0)

    ins = (w, qk, qd, kd, gl, states, vnew, do)
    outs = tuple(jax.ShapeDtypeStruct(t.shape, F32) for t in (vnew, w, qk, qd, kd, gl))
    return pl.pallas_call(
        body,
        out_shape=outs,
        grid=(hh, ng),
        in_specs=[_rec_specs(t.shape, nb, order) for t in ins],
        out_specs=[_rec_specs(t.shape, nb, order) for t in outs],
        scratch_shapes=[pltpu.VMEM((dv, dk), F32)],
        compiler_params=_params("parallel", "arbitrary"),
        name="gdn_rec_bwd",
    )(*ins)


@functools.partial(jax.custom_vjp, nondiff_argnums=(0, 1))
def _gdn_rec(anti, nc, u, w, qk, qd, kd, gl):
    return _gdn_rec_fwd(anti, nc, u, w, qk, qd, kd, gl)[0]


def _gdn_rec_vjp_fwd(anti, nc, u, w, qk, qd, kd, gl):
    o, states, vnew = _gdn_rec_fwd(anti, nc, u, w, qk, qd, kd, gl)
    return o, (w, qk, qd, kd, gl, states, vnew)


def _gdn_rec_vjp_bwd(anti, nc, res, do):
    return _gdn_rec_bwd(anti, nc, *res, do)


_gdn_rec.defvjp(_gdn_rec_vjp_fwd, _gdn_rec_vjp_bwd)


def _split(a):
    hi = a.astype(BF16)
    return hi, (a - hi.astype(F32)).astype(BF16)


def _dot3(a, b, dims=NN):
    ah, al = _split(a)
    bh, bl = _split(b)
    return _dot(ah, bh, dims) + (_dot(ah, bl, dims) + _dot(al, bh, dims))


INV_BASE = 8


def _inv_unit_triangular(mats):
    size = mats[0].shape[0]
    row = lax.broadcasted_iota(jnp.int32, (size, size), 0)
    col = lax.broadcasted_iota(jnp.int32, (size, size), 1)
    eye = jnp.where(row == col, 1.0, 0.0).astype(F32)

    def same_block(width):
        shift = width.bit_length() - 1
        return jnp.right_shift(row, shift) == jnp.right_shift(col, shift)

    base = [jnp.where(same_block(INV_BASE), m, 0.0) for m in mats]
    sq = [_dot3(b, b) for b in base]
    inv = [_dot3(eye - b, eye + s) for b, s in zip(base, sq)]
    width = 4
    while width < INV_BASE:
        sq = [_dot3(s, s) for s in sq]
        inv = [_dot3(i, eye + s) for i, s in zip(inv, sq)]
        width *= 2
    while width < size:
        outer = same_block(2 * width) & jnp.logical_not(same_block(width))
        part = [_dot3(i, jnp.where(outer, m, 0.0)) for i, m in zip(inv, mats)]
        inv = [i - _dot3(p, i) for i, p in zip(inv, part)]
        width *= 2
    return inv


PREP_CHUNKS = 4


def _prep_call(body, ins, outs, name):
    nb = _div_le(ins[0].shape[1], PREP_CHUNKS)

    def spec(shape):
        return pl.BlockSpec((1, nb) + tuple(shape[2:]), lambda h, g: (h, g, 0, 0))

    return pl.pallas_call(
        functools.partial(body, nb),
        out_shape=outs,
        grid=(ins[0].shape[0], ins[0].shape[1] // nb),
        in_specs=[spec(t.shape) for t in ins],
        out_specs=[spec(t.shape) for t in outs],
        compiler_params=_params("parallel", "parallel"),
        name=name,
    )(*ins)


def _off_diagonal(x):
    row = lax.broadcasted_iota(jnp.int32, x.shape, 0)
    col = lax.broadcasted_iota(jnp.int32, x.shape, 1)
    return jnp.where(row == col, 0.0, x)


def _gdn_prep_fwd(kb, k, vb, kbg, q, decay):
    def body(nb, kb_ref, k_ref, vb_ref, kbg_ref, q_ref, dec_ref, u_ref, w_ref, qk_ref, t_ref):
        chunks = range(nb)
        kc = [k_ref[0, c].astype(BF16) for c in chunks]
        dec = [dec_ref[0, c] for c in chunks]
        a = [_off_diagonal(_dot(kb_ref[0, c].astype(BF16), kc[c], NT) * dec[c]) for c in chunks]
        t = _inv_unit_triangular(a)
        for c in chunks:
            tb = t[c].astype(BF16)
            t_ref[0, c] = t[c]
            u_ref[0, c] = _dot(tb, vb_ref[0, c].astype(BF16), NN)
            w_ref[0, c] = _dot(tb, kbg_ref[0, c].astype(BF16), NN)
            qk_ref[0, c] = _dot(q_ref[0, c].astype(BF16), kc[c], NT) * dec[c]

    ins = (kb, k, vb, kbg, q, decay)
    outs = tuple(jax.ShapeDtypeStruct(t.shape, F32) for t in (vb, kbg, decay, decay))
    return _prep_call(body, ins, outs, "gdn_prep_fwd")


def _gdn_prep_bwd(kb, k, vb, kbg, q, decay, t, du, dw, dqk):
    def body(nb, kb_ref, k_ref, vb_ref, kbg_ref, q_ref, dec_ref, t_ref, du_ref, dw_ref, dqk_ref,
             dkb_ref, dk_ref, dvb_ref, dkbg_ref, dq_ref, ddec_ref):
        for c in range(nb):
            kbb, kc, qc = kb_ref[0, c].astype(BF16), k_ref[0, c].astype(BF16), q_ref[0, c].astype(BF16)
            tb = t_ref[0, c].astype(BF16)
            dub, dwb = du_ref[0, c].astype(BF16), dw_ref[0, c].astype(BF16)
            dec = dec_ref[0, c]
            dvb_ref[0, c] = _dot(tb, dub, TN)
            dkbg_ref[0, c] = _dot(tb, dwb, TN)
            dt = _dot(dub, vb_ref[0, c].astype(BF16), NT) + _dot(dwb, kbg_ref[0, c].astype(BF16), NT)
            da = -_off_diagonal(_dot(_dot(tb, dt.astype(BF16), TN).astype(BF16), tb, NT))
            dqk = dqk_ref[0, c]
            ddec_ref[0, c] = da * _dot(kbb, kc, NT) + dqk * _dot(qc, kc, NT)
            dkk = (da * dec).astype(BF16)
            dqkd = (dqk * dec).astype(BF16)
            dkb_ref[0, c] = _dot(dkk, kc, NN)
            dk_ref[0, c] = _dot(dkk, kbb, TN) + _dot(dqkd, qc, TN)
            dq_ref[0, c] = _dot(dqkd, kc, NN)

    ins = (kb, k, vb, kbg, q, decay, t, du, dw, dqk)
    outs = tuple(jax.ShapeDtypeStruct(x.shape, F32) for x in (kb, k, vb, kbg, q, decay))
    return _prep_call(body, ins, outs, "gdn_prep_bwd")


@jax.custom_vjp
def _gdn_prep(kb, k, vb, kbg, q, decay):
    return _gdn_prep_fwd(kb, k, vb, kbg, q, decay)[:3]


def _gdn_prep_vjp_fwd(kb, k, vb, kbg, q, decay):
    u, w, qk, t = _gdn_prep_fwd(kb, k, vb, kbg, q, decay)
    return (u, w, qk), (kb, k, vb, kbg, q, decay, t)


def _gdn_prep_vjp_bwd(res, cot):
    return _gdn_prep_bwd(*res, *cot)


_gdn_prep.defvjp(_gdn_prep_vjp_fwd, _gdn_prep_vjp_bwd)


def _loss_head(y, target):
    rows, d = y.shape
    tr = _tile(rows, 256, SUBLANE)

    def body(y_ref, t_ref, dy_ref, sq_ref):
        diff = y_ref[...] - t_ref[...]
        dy_ref[...] = diff * (1.0 / d)
        part = jnp.sum((diff * diff).reshape(tr // SUBLANE, SUBLANE, d), axis=0)

        @pl.when(pl.program_id(0) == 0)
        def _():
            sq_ref[...] = part

        @pl.when(pl.program_id(0) > 0)
        def _():
            sq_ref[...] += part

    row = pl.BlockSpec((tr, d), lambda i: (i, 0))
    return pl.pallas_call(
        body,
        out_shape=(jax.ShapeDtypeStruct((rows, d), F32), jax.ShapeDtypeStruct((SUBLANE, d), F32)),
        grid=(rows // tr,),
        in_specs=[row, row],
        out_specs=[row, pl.BlockSpec((SUBLANE, d), lambda i: (0, 0))],
        compiler_params=_params("arbitrary"),
        name="loss_head",
    )(y, target)


def _add_half(g, r1, my_c):
    nblk, half, cols = r1.shape
    tr = _tile(half, 128, 2 * SUBLANE)
    per_half = half // tr

    def body(c_ref, g_ref, r_ref, o_ref):
        o_ref[...] = (g_ref[...] + r_ref[...]).astype(BF16)

    grid_spec = pltpu.PrefetchScalarGridSpec(
        num_scalar_prefetch=1,
        grid=(nblk, per_half),
        in_specs=[
            pl.BlockSpec((1, tr, cols), lambda b, i, c_ref: (b, c_ref[0] * per_half + i, 0)),
            pl.BlockSpec((1, tr, cols), lambda b, i, c_ref: (b, i, 0)),
        ],
        out_specs=pl.BlockSpec((1, tr, cols), lambda b, i, c_ref: (b, i, 0)),
    )
    return pl.pallas_call(
        body,
        out_shape=jax.ShapeDtypeStruct(r1.shape, BF16),
        grid_spec=grid_spec,
        compiler_params=_params("parallel", "parallel"),
        name="grad_add_half",
    )(my_c.reshape(1).astype(jnp.int32), g, r1)


def _sum_chips(own, slots, place):
    _, half, cols = own.shape
    tr = _tile(half, 128, 2 * SUBLANE)
    per_half = half // tr

    def body(p_ref, a_ref, b_ref, c_ref, d_ref, o_ref):
        f32 = lambda ref: ref[0].astype(F32)
        o_ref[...] = (f32(a_ref) + f32(b_ref)) + (f32(c_ref) + f32(d_ref))

    def block(k):
        return pl.BlockSpec((1, tr, cols), lambda i, p_ref: (p_ref[k], i, 0))

    grid_spec = pltpu.PrefetchScalarGridSpec(
        num_scalar_prefetch=1,
        grid=(per_half,),
        in_specs=[block(0), block(1), block(2), block(3)],
        out_specs=pl.BlockSpec((tr, cols), lambda i, p_ref: (p_ref[4] * per_half + i, 0)),
    )
    return pl.pallas_call(
        body,
        out_shape=jax.ShapeDtypeStruct((2 * half, cols), F32),
        grid_spec=grid_spec,
        compiler_params=_params("parallel"),
        name="grad_sum_chips",
    )(place, own, slots, slots, slots)


def _cast_into_block(w, chip):
    rows, cols = w.shape
    tr = _tile(rows, 128, 2 * SUBLANE)

    def body(j_ref, w_ref, o_ref):
        o_ref[0] = w_ref[...].astype(BF16)

    grid_spec = pltpu.PrefetchScalarGridSpec(
        num_scalar_prefetch=1,
        grid=(rows // tr,),
        in_specs=[pl.BlockSpec((tr, cols), lambda i, j_ref: (i, 0))],
        out_specs=pl.BlockSpec((1, tr, cols), lambda i, j_ref: (j_ref[0], i, 0)),
    )
    return pl.pallas_call(
        body,
        out_shape=jax.ShapeDtypeStruct((N_CHIP, rows, cols), BF16),
        grid_spec=grid_spec,
        compiler_params=_params("parallel"),
        name="cast_weight_shard",
    )(chip.reshape(1).astype(jnp.int32), w)


def _sum_leading(x, name):
    k, rows, cols = x.shape
    tr = _tile(rows, 128, SUBLANE)

    def body(x_ref, o_ref):
        parts = [x_ref[i] for i in range(k)]
        while len(parts) > 1:
            parts = [parts[i] + parts[i + 1] for i in range(0, len(parts), 2)]
        o_ref[...] = parts[0]

    return pl.pallas_call(
        body,
        out_shape=jax.ShapeDtypeStruct((rows, cols), F32),
        grid=(rows // tr,),
        in_specs=[pl.BlockSpec((k, tr, cols), lambda i: (0, i, 0))],
        out_specs=pl.BlockSpec((tr, cols), lambda i: (i, 0)),
        compiler_params=_params("parallel"),
        name=name,
    )(x)


def _adamw(w, g, m, v, name):
    rows, cols = w.shape
    tr = _tile(rows, 128, SUBLANE)
    m_scale = 1.0 / (1.0 - ADAM_B1 ** ADAM_STEP)
    v_scale = 1.0 / (1.0 - ADAM_B2 ** ADAM_STEP)

    def body(w_ref, g_ref, m_ref, v_ref, d_ref, nm_ref, nv_ref):
        gg = g_ref[...]
        nm = ADAM_B1 * m_ref[...] + (1.0 - ADAM_B1) * gg
        nv = ADAM_B2 * v_ref[...] + (1.0 - ADAM_B2) * (gg * gg)
        nm_ref[...] = nm
        nv_ref[...] = nv
        d_ref[...] = -ADAM_LR * ((nm * m_scale) / (jnp.sqrt(nv * v_scale) + ADAM_EPS) + ADAM_WD * w_ref[...])

    spec = pl.BlockSpec((tr, cols), lambda i: (i, 0))
    out = jax.ShapeDtypeStruct((rows, cols), F32)
    return pl.pallas_call(
        body,
        out_shape=(out, out, out),
        grid=(rows // tr,),
        in_specs=[spec] * 4,
        out_specs=[spec] * 3,
        compiler_params=_params("parallel"),
        name=name,
    )(w, g, m, v)


def _ada_fwd(a16, w):
    layers, d, n = w.shape
    tn = _tile(n, 512)

    def body(a_ref, w_ref, o_ref):
        o_ref[0] = _dot(a_ref[...].astype(BF16), w_ref[0].astype(BF16), NN)

    return pl.pallas_call(
        body,
        out_shape=jax.ShapeDtypeStruct((layers, a16.shape[0], n), F32),
        grid=(layers, n // tn),
        in_specs=[pl.BlockSpec(a16.shape, lambda l, j: (0, 0)), pl.BlockSpec((1, d, tn), lambda l, j: (l, 0, j))],
        out_specs=pl.BlockSpec((1, a16.shape[0], tn), lambda l, j: (l, 0, j)),
        compiler_params=_params("parallel", "parallel"),
        name="ada_fwd",
    )(a16, w)


def _ada_bwd(a16, dm, w):
    layers, d, n = w.shape
    tn = _tile(n, 512)
    rows = a16.shape[0]

    def body(a_ref, dm_ref, w_ref, gw_ref, ga_ref):
        dmb = dm_ref[0].astype(BF16)
        gw_ref[0] = _dot(a_ref[...].astype(BF16), dmb, TN)
        part = _dot(dmb, w_ref[0].astype(BF16), NT)
        first = (pl.program_id(0) == 0) & (pl.program_id(1) == 0)

        @pl.when(first)
        def _():
            ga_ref[...] = part

        @pl.when(jnp.logical_not(first))
        def _():
            ga_ref[...] += part

    return pl.pallas_call(
        body,
        out_shape=(jax.ShapeDtypeStruct(w.shape, F32), jax.ShapeDtypeStruct((rows, d), F32)),
        grid=(layers, n // tn),
        in_specs=[
            pl.BlockSpec((rows, d), lambda l, j: (0, 0)),
            pl.BlockSpec((1, rows, tn), lambda l, j: (l, 0, j)),
            pl.BlockSpec((1, d, tn), lambda l, j: (l, 0, j)),
        ],
        out_specs=[pl.BlockSpec((1, d, tn), lambda l, j: (l, 0, j)), pl.BlockSpec((rows, d), lambda l, j: (0, 0))],
        compiler_params=_params("arbitrary", "arbitrary"),
        name="ada_bwd",
    )(a16, dm, w)


def _place():
    ix, iy, ic = lax.axis_index("x"), lax.axis_index("y"), lax.axis_index("c")
    return ix, iy, ic


def _flip(coord, bit):
    return 1 - coord if bit else coord


def _remote(src, dst, send_sem, recv_sem, to):
    return pltpu.make_async_remote_copy(
        src_ref=src, dst_ref=dst, send_sem=send_sem, recv_sem=recv_sem, device_id=to, device_id_type=MESH)


def _all_gather8(x, name):
    rows, cols = x.shape

    def body(x_ref, o_ref, send_sems, recv_sems):
        ix, iy, ic = _place()
        me = 4 * ix + 2 * iy + ic
        o_ref[me] = x_ref[...]
        peers = []
        for p in range(1, N_DEV):
            to = (_flip(ix, p & 4), _flip(iy, p & 2), _flip(ic, p & 1))
            peers.append((to, 4 * to[0] + 2 * to[1] + to[2]))
        sends = [_remote(x_ref, o_ref.at[me], send_sems.at[p], recv_sems.at[p], to)
                 for p, (to, _) in enumerate(peers)]
        for cp in sends:
            cp.start()
        for p, (to, slot) in enumerate(peers):
            _remote(x_ref, o_ref.at[slot], send_sems.at[p], recv_sems.at[p], to).wait_recv()
        for cp in sends:
            cp.wait_send()

    return pl.pallas_call(
        body,
        out_shape=jax.ShapeDtypeStruct((N_DEV, rows, cols), F32),
        in_specs=[pl.BlockSpec(memory_space=pltpu.VMEM)],
        out_specs=pl.BlockSpec(memory_space=pltpu.VMEM),
        scratch_shapes=[pltpu.SemaphoreType.DMA((N_DEV - 1,)), pltpu.SemaphoreType.DMA((N_DEV - 1,))],
        compiler_params=pltpu.CompilerParams(vmem_limit_bytes=VMEM_LIMIT),
        name=name,
    )(x)


def _other_chips(ix, iy):
    chips = [(1 - ix, iy), (ix, 1 - iy), (1 - ix, 1 - iy)]
    return [(cx, cy, 2 * cx + cy) for cx, cy in chips]


def _hbm_call(body, ins, out_shapes, scratch, name, in_place=False):
    hbm = pl.BlockSpec(memory_space=pltpu.HBM)
    return pl.pallas_call(
        body,
        out_shape=out_shapes,
        in_specs=[hbm] * len(ins),
        out_specs=[hbm] * len(out_shapes),
        scratch_shapes=scratch,
        input_output_aliases={t: t for t in range(len(ins))} if in_place else {},
        compiler_params=pltpu.CompilerParams(vmem_limit_bytes=VMEM_LIMIT),
        name=name,
    )(*ins)


def _gather_weights(blocks):
    nt = len(blocks)

    def body(*refs):
        ins, outs = refs[:nt], refs[nt:2 * nt]
        ici_send, ici_recv, d2d_send, d2d_recv = refs[2 * nt:]
        ix, iy, ic = _place()
        j = 2 * ix + iy
        sibling = (ix, iy, 1 - ic)
        chips = _other_chips(ix, iy)
        sends, waits = [], []
        for t in range(nt):
            half = ins[t].shape[1] // 2
            mine = pl.ds(ic * half, half)
            for q, (cx, cy, _) in enumerate(chips):
                cp = _remote(ins[t].at[j, mine], outs[t].at[j, mine], ici_send.at[3 * t + q], ici_recv.at[3 * t + q],
                             (cx, cy, ic))
                cp.start()
                sends.append(cp)
        for q, (cx, cy, jq) in enumerate(chips):
            for t in range(nt):
                half = ins[t].shape[1] // 2
                mine = pl.ds(ic * half, half)
                theirs = pl.ds((1 - ic) * half, half)
                k = 3 * t + q
                _remote(ins[t].at[j, mine], outs[t].at[jq, mine], ici_send.at[k], ici_recv.at[k],
                        (cx, cy, ic)).wait_recv()
                cp = _remote(outs[t].at[jq, mine], outs[t].at[jq, mine], d2d_send.at[k], d2d_recv.at[k], sibling)
                cp.start()
                sends.append(cp)
                waits.append(_remote(outs[t].at[jq, theirs], outs[t].at[jq, theirs], d2d_send.at[k], d2d_recv.at[k],
                                     sibling))
        for cp in waits:
            cp.wait_recv()
        for cp in sends:
            cp.wait_send()

    dma = pltpu.SemaphoreType.DMA
    outs = tuple(jax.ShapeDtypeStruct(b.shape, b.dtype) for b in blocks)
    return _hbm_call(body, blocks, outs, [dma((3 * nt,))] * 4, "gather_weights", in_place=True)


def _exchange_halves(grads):
    nt = len(grads)

    def body(*refs):
        ins, outs = refs[:nt], refs[nt:2 * nt]
        send_sems, recv_sems = refs[2 * nt:]
        ix, iy, ic = _place()
        sibling = (ix, iy, 1 - ic)
        copies = []
        for t in range(nt):
            half = ins[t].shape[1] // 2
            theirs = pl.ds((1 - ic) * half, half)
            cp = _remote(ins[t].at[:, theirs, :], outs[t], send_sems.at[t], recv_sems.at[t], sibling)
            cp.start()
            copies.append(cp)
        for cp in copies:
            cp.wait_recv()
        for cp in copies:
            cp.wait_send()

    dma = pltpu.SemaphoreType.DMA
    outs = tuple(jax.ShapeDtypeStruct((g.shape[0], g.shape[1] // 2, g.shape[2]), F32) for g in grads)
    return _hbm_call(body, grads, outs, [dma((nt,)), dma((nt,))], "grad_exchange_halves")


def _exchange_blocks(parts):
    nt = len(parts)

    def body(*refs):
        ins, outs = refs[:nt], refs[nt:2 * nt]
        send_sems, recv_sems = refs[2 * nt:]
        ix, iy, ic = _place()
        j = 2 * ix + iy
        chips = _other_chips(ix, iy)
        sends, waits = [], []
        for t in range(nt):
            for q, (cx, cy, jq) in enumerate(chips):
                k = 3 * t + q
                cp = _remote(ins[t].at[jq], outs[t].at[j], send_sems.at[k], recv_sems.at[k], (cx, cy, ic))
                cp.start()
                sends.append(cp)
                waits.append(_remote(ins[t].at[jq], outs[t].at[jq], send_sems.at[k], recv_sems.at[k], (cx, cy, ic)))
        for cp in waits:
            cp.wait_recv()
        for cp in sends:
            cp.wait_send()

    dma = pltpu.SemaphoreType.DMA
    outs = tuple(jax.ShapeDtypeStruct(p.shape, p.dtype) for p in parts)
    return _hbm_call(body, parts, outs, [dma((3 * nt,)), dma((3 * nt,))], "grad_exchange_blocks")


def _share_halves(blocks):
    nt = len(blocks)

    def body(*refs):
        ins, outs = refs[:nt], refs[nt:2 * nt]
        send_sems, recv_sems = refs[2 * nt:]
        ix, iy, ic = _place()
        sibling = (ix, iy, 1 - ic)
        sends, waits = [], []
        for t in range(nt):
            half = ins[t].shape[0] // 2
            mine = pl.ds(ic * half, half)
            theirs = pl.ds((1 - ic) * half, half)
            cp = _remote(ins[t].at[mine], outs[t].at[mine], send_sems.at[t], recv_sems.at[t], sibling)
            cp.start()
            sends.append(cp)
            waits.append(_remote(ins[t].at[theirs], outs[t].at[theirs], send_sems.at[t], recv_sems.at[t], sibling))
        for cp in waits:
            cp.wait_recv()
        for cp in sends:
            cp.wait_send()

    dma = pltpu.SemaphoreType.DMA
    outs = tuple(jax.ShapeDtypeStruct(b.shape, F32) for b in blocks)
    return _hbm_call(body, blocks, outs, [dma((nt,)), dma((nt,))], "grad_share_halves", in_place=True)


def _reduce_grads(grads, ix, iy, ic):
    recv = _exchange_halves(grads)
    chip_sums = [_add_half(g, r1, ic) for g, r1 in zip(grads, recv)]
    slots = _exchange_blocks(chip_sums)
    place = jnp.stack([2 * ix + iy] + [jq for _, _, jq in _other_chips(ix, iy)] + [ic]).astype(jnp.int32)
    return _share_halves([_sum_chips(own, got, place) for own, got in zip(chip_sums, slots)])


def _rmsnorm(x, w):
    return x * lax.rsqrt(jnp.mean(x * x, axis=-1, keepdims=True) + EPS) * w


def _l2norm(x):
    return x * lax.rsqrt(jnp.sum(x * x, axis=-1, keepdims=True) + EPS)


def _heads(t, d):
    return t.reshape(t.shape[:-1] + (t.shape[-1] // d, d))


def _chunks(t, chunk):
    n = t.shape[0] // chunk
    return t.reshape(n, chunk, t.shape[1], t.shape[2]).transpose(2, 0, 1, 3)


def _unchunks(t):
    hh, n, chunk, d = t.shape
    return t.transpose(1, 2, 0, 3).reshape(n * chunk, hh, d)


def _triangle(size, anti):
    ones = jnp.ones((size, size), bool)
    return jnp.triu(ones) if anti else jnp.tril(ones)


def _hgrn2_scan(q, k, v, log_f, n_ctx, anti):
    q, k, v, log_f = (_chunks(t, A_CHUNK) for t in (q, k, v, log_f))
    b = lax.cumsum(log_f, axis=2, reverse=anti)
    end = 0 if anti else A_CHUNK - 1
    mid = A_CHUNK // 2 if anti else A_CHUNK // 2 - 1
    b_last = b[:, :, end:end + 1, :]
    b_mid = b[:, :, mid:mid + 1, :]
    return _unchunks(_hgrn_rec(anti, n_ctx // A_CHUNK, q * jnp.exp(b), k * jnp.exp(b_last - b), v, jnp.exp(b_last),
                               q * jnp.exp(b - b_mid), k * jnp.exp(b_mid - b)))


def _gdn_scan(q, k, v, g, beta, n_ctx, anti):
    q, k, v = (_chunks(t, C_CHUNK) for t in (q, k, v))
    g = _chunks(g[..., None], C_CHUNK)[..., 0]
    beta = _chunks(beta[..., None], C_CHUNK)[..., 0]
    gc = lax.cumsum(g, axis=2, reverse=anti)
    end = 0 if anti else C_CHUNK - 1
    causal = _triangle(C_CHUNK, anti)
    diff = gc[..., :, None] - gc[..., None, :]
    decay = jnp.where(causal, jnp.exp(jnp.where(causal, diff, 0.0)), 0.0)
    k_beta = k * beta[..., None]
    u, w, qk = _gdn_prep(k_beta, k, v * beta[..., None], k_beta * jnp.exp(gc)[..., None], q, decay)
    q_dec = q * jnp.exp(gc)[..., None]
    k_dec = k * jnp.exp(gc[..., end:end + 1] - gc)[..., None]
    g_last = jnp.exp(gc[..., end])
    gl = jnp.broadcast_to(g_last[..., None, None], g_last.shape + (1, q.shape[-1]))
    return _unchunks(_gdn_rec(anti, n_ctx // C_CHUNK, u, w, qk, q_dec, k_dec, gl))


def _bidirectional(scan_fn, fwd_args, bwd_args, n_ctx):
    return scan_fn(*fwd_args, n_ctx, False) + scan_fn(*bwd_args, n_ctx, True)


def _multiscale_pool(u, pool_w, pool_scale):
    rows, length, width = u.shape
    groups = len(POOL_WINDOWS)
    gdim = width // groups
    uf = u.reshape(rows, length, groups, gdim)
    cs = jnp.concatenate([jnp.zeros_like(uf[:, :1]), jnp.cumsum(uf, axis=1)], axis=1)
    pos = np.arange(length)
    mixed = []
    for gi, win in enumerate(POOL_WINDOWS):
        below, above = win // 2, win - win // 2
        lo = np.clip(pos - below, 0, length - 1)
        hi = np.clip(pos + win - 1 - below, 0, length - 1)
        cnt = jnp.asarray((hi - lo + 1).astype(np.float32))[None, :, None]
        csg = cs[:, :, gi, :]
        upper = jnp.concatenate([csg[:, above:]] + [csg[:, length:]] * (above - 1), axis=1)
        lower = jnp.concatenate([jnp.zeros_like(csg[:, :below]), csg[:, :length - below]], axis=1)
        mixed.append((upper - lower) / cnt - uf[:, :, gi, :])
    d = jnp.stack(mixed, axis=2)
    y = jnp.einsum("rlgc,gcd->rlgd", d, pool_w)
    return y.reshape(rows, length, width) * pool_scale


def _short_conv(u, w):
    length = u.shape[0]
    left = C_CONV // 2
    up = jnp.pad(u, ((left, C_CONV - 1 - left), (0, 0)))
    out = up[0:length] * w[0]
    for j in range(1, C_CONV):
        out = out + up[j:j + length] * w[j]
    return out


def _hgrn2_gates(pre_f, lb):
    log_f = jnp.log(lb + (1.0 - lb) * jax.nn.sigmoid(pre_f))
    k = (1.0 - lb) * jax.nn.sigmoid(-pre_f)
    return _heads(k, A_HEAD_DIM), _heads(log_f, A_HEAD_DIM)


def _even_mixer(h, n_ctx, lb, w_in, w_in_hook, a_norm, pool_w, pool_scale, w_out, w_out_hook):
    a_width = w_out.shape[0] // 2
    p = _dense(h, w_in, w_in_hook)
    q, f_f, f_b, i, g, u = jnp.split(p, [a_width * s for s in range(1, 6)], axis=-1)
    q = _heads(jax.nn.silu(q), A_HEAD_DIM)
    i = _heads(i, A_HEAD_DIM)
    k_f, logf_f = _hgrn2_gates(f_f, lb[0])
    k_b, logf_b = _hgrn2_gates(f_b, lb[1])
    o = _bidirectional(_hgrn2_scan, (q, k_f, i, logf_f), (q, k_b, i, logf_b), n_ctx)
    b_width = u.shape[-1]
    u_l = u[n_ctx:].reshape(-1, GRID_W, b_width)
    pooled = jnp.concatenate([
        _multiscale_pool(u[None, :n_ctx], pool_w, pool_scale)[0],
        _multiscale_pool(u_l, pool_w, pool_scale).reshape(-1, b_width),
    ], axis=0)
    a_out = _rmsnorm(o, a_norm) * jax.nn.silu(_heads(g, A_HEAD_DIM))
    a_out = a_out.reshape(a_out.shape[0], a_width)
    return _dense(jnp.concatenate([a_out, pooled], axis=-1), w_out, w_out_hook)


def _odd_mixer(h, n_ctx, w_main, w_main_hook, w_gate, w_gate_hook, conv_w, a_log, dt_bias, norm_w, w_out, w_out_hook):
    value_width = w_out.shape[0]
    key_width = value_width // 2
    rep = 2
    a_rate = jnp.exp(a_log)
    p = _dense(h, w_main, w_main_hook)
    gates = _dense(h, w_gate, w_gate_hook)
    qkv, z = p[:, :2 * key_width + value_width], p[:, 2 * key_width + value_width:]
    qkv = jnp.concatenate([_short_conv(qkv[:n_ctx], conv_w), _short_conv(qkv[n_ctx:], conv_w)], axis=0)
    qkv = jax.nn.silu(qkv)
    q, k, v = jnp.split(qkv, [key_width, 2 * key_width], axis=-1)
    q = jnp.repeat(_l2norm(_heads(q, C_HEAD_DIM)) * C_HEAD_DIM ** -0.5, rep, axis=1)
    k = jnp.repeat(_l2norm(_heads(k, C_HEAD_DIM)), rep, axis=1)
    v = _heads(v, C_HEAD_DIM)
    a_f, a_b, b_f, b_b = jnp.split(gates, 4, axis=-1)
    g_f = -a_rate[0] * jax.nn.softplus(a_f + dt_bias[0])
    g_b = -a_rate[1] * jax.nn.softplus(a_b + dt_bias[1])
    o = _bidirectional(_gdn_scan, (q, k, v, g_f, jax.nn.sigmoid(b_f)), (q, k, v, g_b, jax.nn.sigmoid(b_b)), n_ctx)
    y = _rmsnorm(o, norm_w) * jax.nn.silu(_heads(z, C_HEAD_DIM))
    return _dense(y.reshape(y.shape[0], value_width), w_out, w_out_hook)


def _swiglu(h, w13, w13_hook, w2, w2_hook):
    gate, up = jnp.split(_dense(h, w13, w13_hook), 2, axis=-1)
    return _dense(jax.nn.silu(gate) * up, w2, w2_hook)


def _forward(x, ctx, mods, big, hooks, small):
    n_ctx = ctx.shape[0]
    d = x.shape[-1]
    stream = jnp.concatenate([ctx, x], axis=0)
    is_ctx = (jnp.arange(stream.shape[0]) < n_ctx)[:, None]
    lb_all = jnp.cumsum(jax.nn.softmax(small["ev_lb"], axis=1), axis=1)
    for layer in range(2):
        m = [jnp.where(is_ctx, mods[layer, 1, s * d:(s + 1) * d][None, :], mods[layer, 0, s * d:(s + 1) * d][None, :])
             for s in range(6)]
        nw = small["norm_w"][layer]
        h = _rmsnorm(stream, nw[0]) * (1.0 + m[1]) + m[0]
        if layer == 0:
            y = _even_mixer(h, n_ctx, lb_all[:, layer], big["ev_w_in"], hooks["ev_w_in"], small["ev_a_norm"][0],
                            small["ev_pool_w"][0], small["ev_pool_scale"][0], big["ev_w_out"], hooks["ev_w_out"])
        else:
            y = _odd_mixer(h, n_ctx, big["od_w_main"], hooks["od_w_main"], big["od_w_gate"], hooks["od_w_gate"],
                           small["od_conv"][0], small["od_A_log"][0], small["od_dt_bias"][0], small["od_norm"][0],
                           big["od_w_out"], hooks["od_w_out"])
        stream = stream + m[2] * _rmsnorm(y, nw[1])
        h = _rmsnorm(stream, nw[2]) * (1.0 + m[4]) + m[3]
        f = _swiglu(h, big["ffn_w13_%d" % layer], hooks["ffn_w13_%d" % layer],
                    big["ffn_w2_%d" % layer], hooks["ffn_w2_%d" % layer])
        stream = stream + m[5] * _rmsnorm(f, nw[3])
    return stream[n_ctx:]


def _pack(arrays):
    flat = jnp.concatenate([a.reshape(-1) for a in arrays])
    pad = (-flat.shape[0]) % (SUBLANE * LANE)
    return jnp.pad(flat, (0, pad)).reshape(-1, LANE)


def _unpack(packed, shapes, lead=()):
    flat = packed.reshape(lead + (-1,))
    out, at = [], 0
    for shape in shapes:
        size = int(np.prod(shape))
        out.append(flat[..., at:at + size].reshape(lead + tuple(shape)))
        at += size
    return out


def _from_chips(gathered, axis):
    return jnp.concatenate([gathered[2 * j] for j in range(N_CHIP)], axis=axis)


def _col_natural(blocks):
    nblk, k, n = blocks.shape
    return blocks.transpose(1, 0, 2).reshape(k, nblk * n)


def _col_blocked(nat):
    k, n4 = nat.shape
    return nat.reshape(k, N_CHIP, n4 // N_CHIP).transpose(1, 0, 2)


def kernel(x, c, ctx, c_ctx, w_ada, b_ada, norm_w, ev_w_in, ev_lb, ev_a_norm, ev_pool_w, ev_pool_scale, ev_w_out, od_w_in, od_conv, od_A_log, od_dt_bias, od_norm, od_w_out, ffn_w13, ffn_w2, loss_target, m_c_ctx, m_w_ada, m_b_ada, m_norm_w, m_ev_w_in, m_ev_lb, m_ev_a_norm, m_ev_pool_w, m_ev_pool_scale, m_ev_w_out, m_od_w_in, m_od_conv, m_od_A_log, m_od_dt_bias, m_od_norm, m_od_w_out, m_ffn_w13, m_ffn_w2, v_c_ctx, v_w_ada, v_b_ada, v_norm_w, v_ev_w_in, v_ev_lb, v_ev_a_norm, v_ev_pool_w, v_ev_pool_scale, v_ev_w_out, v_od_w_in, v_od_conv, v_od_A_log, v_od_dt_bias, v_od_norm, v_od_w_out, v_ffn_w13, v_ffn_w2):
    ix, iy, ic = _place()
    me = 4 * ix + 2 * iy + ic
    chip = 2 * ix + iy
    d = x.shape[-1]
    layers = w_ada.shape[0]
    n_ada = w_ada.shape[-1]

    pre_parts = [c[0], norm_w, ev_lb, ev_pool_w, od_conv]
    pre = _all_gather8(_pack(pre_parts), "gather_small_inputs")
    c_all, norm_w_g, ev_lb_g, pool_w_g, od_conv_g = _unpack(pre, [p.shape for p in pre_parts], lead=(N_DEV,))
    small = {
        "norm_w": _from_chips(norm_w_g, 2),
        "ev_lb": _from_chips(ev_lb_g, 2),
        "ev_a_norm": ev_a_norm,
        "ev_pool_w": _from_chips(pool_w_g, 2),
        "ev_pool_scale": ev_pool_scale,
        "od_conv": _from_chips(od_conv_g, 2),
        "od_A_log": od_A_log,
        "od_dt_bias": od_dt_bias,
        "od_norm": od_norm,
    }

    silu_cc, silu_cc_vjp = jax.vjp(jax.nn.silu, c_ctx)
    a16 = jnp.concatenate([jax.nn.silu(c_all), silu_cc[None], jnp.zeros((2 * SUBLANE - N_DEV - 1, d), F32)], axis=0)
    mods_local = _ada_fwd(a16, w_ada)
    mods_g = _all_gather8(mods_local.reshape(-1, n_ada), "gather_modulation")
    mods_g = mods_g.reshape(N_DEV, layers, 2 * SUBLANE, n_ada)
    mods_full = _from_chips(mods_g, 2) + b_ada[:, None, :]
    mods = jnp.stack([lax.dynamic_index_in_dim(mods_full, me, axis=1, keepdims=False), mods_full[:, N_DEV]], axis=1)

    ffn13_rows = ffn_w13.shape[1]
    ffn2_rows = ffn_w2.shape[1]
    shards = [ev_w_in[0], ev_w_out[0], od_w_in[0], od_w_out[0],
              ffn_w13.reshape(-1, ffn_w13.shape[-1]), ffn_w2.reshape(-1, ffn_w2.shape[-1])]
    g_ev_in, g_ev_out, g_od_in, g_od_out, g_w13, g_w2 = _gather_weights([_cast_into_block(s, chip) for s in shards])
    od_nat = _col_natural(g_od_in)
    n_gate = 4 * (od_w_out.shape[1] * N_CHIP // C_HEAD_DIM)
    n_main = od_nat.shape[1] - n_gate
    big = {
        "ev_w_in": _col_natural(g_ev_in),
        "ev_w_out": g_ev_out.reshape(-1, g_ev_out.shape[-1]),
        "od_w_main": od_nat[:, :n_main],
        "od_w_gate": od_nat[:, n_main:],
        "od_w_out": g_od_out.reshape(-1, g_od_out.shape[-1]),
    }
    for layer in range(layers):
        big["ffn_w13_%d" % layer] = _col_natural(g_w13[:, layer * ffn13_rows:(layer + 1) * ffn13_rows])
        big["ffn_w2_%d" % layer] = g_w2[:, layer * ffn2_rows:(layer + 1) * ffn2_rows].reshape(-1, g_w2.shape[-1])
    hooks = {name: jnp.zeros(w.shape, F32) for name, w in big.items()}

    def local_forward(x_, mods_, hooks_, small_):
        return _forward(x_, ctx[0], mods_, big, hooks_, small_)

    y, pullback = jax.vjp(local_forward, x[0], mods, hooks, small)
    dy, sq = _loss_head(y, loss_target[0])
    loss = lax.psum(jnp.sum(sq) * (0.5 / d), ("x", "y", "c"))
    grad_x, d_mods, d_big, d_small = pullback(dy)

    d_od_in = jnp.concatenate([d_big["od_w_main"], d_big["od_w_gate"]], axis=1)
    blocked = [
        _col_blocked(d_big["ev_w_in"]),
        d_big["ev_w_out"].reshape(N_CHIP, -1, d_big["ev_w_out"].shape[-1]),
        _col_blocked(d_od_in),
        d_big["od_w_out"].reshape(N_CHIP, -1, d_big["od_w_out"].shape[-1]),
        jnp.concatenate([_col_blocked(d_big["ffn_w13_%d" % layer]) for layer in range(layers)], axis=1),
        jnp.concatenate([d_big["ffn_w2_%d" % layer].reshape(N_CHIP, -1, d_big["ffn_w2_%d" % layer].shape[-1])
                         for layer in range(layers)], axis=1),
    ]
    r_ev_in, r_ev_out, r_od_in, r_od_out, r_w13, r_w2 = _reduce_grads(blocked, ix, iy, ic)

    small_names = ["norm_w", "ev_lb", "ev_a_norm", "ev_pool_w", "ev_pool_scale", "od_conv", "od_A_log",
                   "od_dt_bias", "od_norm"]
    post_parts = [d_mods[:, 0], d_mods[:, 1]] + [d_small[n] for n in small_names]
    post = _all_gather8(_pack(post_parts), "gather_small_grads")
    post_sum = _sum_leading(post, "sum_small_grads")
    dm_l_all = _unpack(post, [post_parts[0].shape], lead=(N_DEV,))[0]
    summed = _unpack(post_sum, [p.shape for p in post_parts])
    dm_l_sum, dm_c_sum = summed[0], summed[1]
    g_small = dict(zip(small_names, summed[2:]))
    grad_b_ada = dm_l_sum + dm_c_sum

    def my_cols(full, width):
        return lax.dynamic_slice_in_dim(full, chip * width, width, axis=full.ndim - 1)

    dm_rows = jnp.concatenate([
        my_cols(dm_l_all, n_ada).transpose(1, 0, 2),
        my_cols(dm_c_sum, n_ada)[:, None, :],
        jnp.zeros((layers, 2 * SUBLANE - N_DEV - 1, n_ada), F32),
    ], axis=1)
    grad_w_ada, ga = _ada_bwd(a16, dm_rows, w_ada)
    ga_g = _all_gather8(jnp.pad(ga[N_DEV][None], ((0, SUBLANE - 1), (0, 0))), "gather_c_ctx_grad")
    g_silu_cc = _sum_leading(jnp.stack([ga_g[2 * j] for j in range(N_CHIP)]), "sum_c_ctx_grad")[0]
    grad_c_ctx = silu_cc_vjp(g_silu_cc)[0]

    grads = {
        "c_ctx": grad_c_ctx,
        "w_ada": grad_w_ada,
        "b_ada": grad_b_ada,
        "norm_w": my_cols(g_small["norm_w"], norm_w.shape[-1]),
        "ev_w_in": r_ev_in[None],
        "ev_lb": my_cols(g_small["ev_lb"], ev_lb.shape[-1]),
        "ev_a_norm": g_small["ev_a_norm"],
        "ev_pool_w": lax.dynamic_slice_in_dim(g_small["ev_pool_w"], chip * ev_pool_w.shape[2], ev_pool_w.shape[2], axis=2),
        "ev_pool_scale": g_small["ev_pool_scale"],
        "ev_w_out": r_ev_out[None],
        "od_w_in": r_od_in[None],
        "od_conv": my_cols(g_small["od_conv"], od_conv.shape[-1]),
        "od_A_log": g_small["od_A_log"],
        "od_dt_bias": g_small["od_dt_bias"],
        "od_norm": g_small["od_norm"],
        "od_w_out": r_od_out[None],
        "ffn_w13": r_w13.reshape(ffn_w13.shape),
        "ffn_w2": r_w2.reshape(ffn_w2.shape),
    }

    weights = dict(c_ctx=c_ctx, w_ada=w_ada, b_ada=b_ada, norm_w=norm_w, ev_w_in=ev_w_in, ev_lb=ev_lb,
                   ev_a_norm=ev_a_norm, ev_pool_w=ev_pool_w, ev_pool_scale=ev_pool_scale, ev_w_out=ev_w_out,
                   od_w_in=od_w_in, od_conv=od_conv, od_A_log=od_A_log, od_dt_bias=od_dt_bias, od_norm=od_norm,
                   od_w_out=od_w_out, ffn_w13=ffn_w13, ffn_w2=ffn_w2)
    first = dict(c_ctx=m_c_ctx, w_ada=m_w_ada, b_ada=m_b_ada, norm_w=m_norm_w, ev_w_in=m_ev_w_in, ev_lb=m_ev_lb,
                 ev_a_norm=m_ev_a_norm, ev_pool_w=m_ev_pool_w, ev_pool_scale=m_ev_pool_scale, ev_w_out=m_ev_w_out,
                 od_w_in=m_od_w_in, od_conv=m_od_conv, od_A_log=m_od_A_log, od_dt_bias=m_od_dt_bias,
                 od_norm=m_od_norm, od_w_out=m_od_w_out, ffn_w13=m_ffn_w13, ffn_w2=m_ffn_w2)
    second = dict(c_ctx=v_c_ctx, w_ada=v_w_ada, b_ada=v_b_ada, norm_w=v_norm_w, ev_w_in=v_ev_w_in, ev_lb=v_ev_lb,
                  ev_a_norm=v_ev_a_norm, ev_pool_w=v_ev_pool_w, ev_pool_scale=v_ev_pool_scale, ev_w_out=v_ev_w_out,
                  od_w_in=v_od_w_in, od_conv=v_od_conv, od_A_log=v_od_A_log, od_dt_bias=v_od_dt_bias,
                  od_norm=v_od_norm, od_w_out=v_od_w_out, ffn_w13=v_ffn_w13, ffn_w2=v_ffn_w2)
    names = list(weights)
    large = ["w_ada", "ev_w_in", "ev_w_out", "od_w_in", "od_w_out", "ffn_w13", "ffn_w2"]
    little = [n for n in names if n not in large]
    delta, new_m, new_v = {}, {}, {}
    for n in large:
        as2d = lambda t: t.reshape(-1, t.shape[-1])
        out = _adamw(as2d(weights[n]), as2d(grads[n]), as2d(first[n]), as2d(second[n]), "adamw_" + n)
        delta[n], new_m[n], new_v[n] = (t.reshape(weights[n].shape) for t in out)
    shapes = [weights[n].shape for n in little]
    out = _adamw(*(_pack([src[n] for n in little]) for src in (weights, grads, first, second)), "adamw_small")
    for res, packed in zip((delta, new_m, new_v), out):
        res.update(zip(little, _unpack(packed, shapes)))

    return (loss, grad_x[None], *[grads[n] for n in names], *[delta[n] for n in names],
            *[new_m[n] for n in names], *[new_v[n] for n in names])
```

```python
import functools

import numpy as np
import jax
import jax.numpy as jnp
from jax import lax
from jax.experimental import pallas as pl
from jax.experimental.pallas import tpu as pltpu

F32 = jnp.float32
BF16 = jnp.bfloat16
MESH = pl.DeviceIdType.MESH

EPS = 1e-6
GRID_W = 64
A_HEAD_DIM = 128
A_CHUNK = 32
POOL_WINDOWS = (2, 4, 8, 16)
C_HEAD_DIM = 128
C_CONV = 4
C_CHUNK = 64

ADAM_LR = 0.001
ADAM_B1 = 0.9
ADAM_B2 = 0.999
ADAM_EPS = 1e-08
ADAM_WD = 0.01
ADAM_STEP = 10

N_DEV = 8
N_CHIP = 4
LANE = 128
SUBLANE = 8
VMEM_LIMIT = 48 * 1024 * 1024

NN = (((1,), (0,)), ((), ()))
NT = (((1,), (1,)), ((), ()))
TN = (((0,), (0,)), ((), ()))


def _tile(n, cap, mult=LANE):
    best = 0
    for d in range(mult, min(n, cap) + 1, mult):
        if n % d == 0:
            best = d
    return best or n


def _div_le(n, cap):
    return max(d for d in range(1, cap + 1) if n % d == 0)


def _params(*sem):
    return pltpu.CompilerParams(dimension_semantics=sem or None, vmem_limit_bytes=VMEM_LIMIT)


def _dot(a, b, dims):
    return lax.dot_general(a, b, dims, preferred_element_type=F32)


def _matmul(a, b, mode, name):
    if mode == "nn":
        (m, k), n = a.shape, b.shape[1]
    elif mode == "nt":
        (m, k), n = a.shape, b.shape[0]
    else:
        (k, m), n = a.shape, b.shape[1]
    tm, tn, tk = _tile(m, 1024), _tile(n, 1024), _tile(k, 2048)
    nk = k // tk
    dims = {"nn": NN, "nt": NT, "tn": TN}[mode]
    if mode == "tn":
        a_spec = pl.BlockSpec((tk, tm), lambda i, j, kk: (kk, i))
    else:
        a_spec = pl.BlockSpec((tm, tk), lambda i, j, kk: (i, kk))
    if mode == "nt":
        b_spec = pl.BlockSpec((tn, tk), lambda i, j, kk: (j, kk))
    else:
        b_spec = pl.BlockSpec((tk, tn), lambda i, j, kk: (kk, j))

    def body(a_ref, b_ref, o_ref):
        part = _dot(a_ref[...], b_ref[...], dims)
        if nk == 1:
            o_ref[...] = part
        else:
            kk = pl.program_id(2)

            @pl.when(kk == 0)
            def _():
                o_ref[...] = part

            @pl.when(kk > 0)
            def _():
                o_ref[...] += part

    return pl.pallas_call(
        body,
        out_shape=jax.ShapeDtypeStruct((m, n), F32),
        grid=(m // tm, n // tn, nk),
        in_specs=[a_spec, b_spec],
        out_specs=pl.BlockSpec((tm, tn), lambda i, j, kk: (i, j)),
        compiler_params=_params("parallel", "parallel", "arbitrary"),
        name=name,
    )(a, b)


@jax.custom_vjp
def _dense(h, w, hook):
    return _matmul(h.astype(BF16), w, "nn", "dense_fwd")


def _dense_fwd(h, w, hook):
    hb = h.astype(BF16)
    return _matmul(hb, w, "nn", "dense_fwd"), (hb, w)


def _dense_bwd(res, dy):
    hb, w = res
    dyb = dy.astype(BF16)
    return _matmul(dyb, w, "nt", "dense_dx"), None, _matmul(hb, dyb, "tn", "dense_dw")


_dense.defvjp(_dense_fwd, _dense_bwd)


def _rec_plan(n, nc, anti, backward):
    nb = _div_le(int(np.gcd(n, nc)), 12)
    ng, ncb = n // nb, nc // nb
    if not anti:
        order = (lambda g: ng - 1 - g) if backward else (lambda g: g)
        return nb, ng, order, backward
    if backward:
        return nb, ng, (lambda g: jnp.where(g < ng - ncb, ncb + g, g - (ng - ncb))), False
    return nb, ng, (lambda g: jnp.where(g < ncb, ncb - 1 - g, ng - 1 - (g - ncb))), True


REC_HEADS = 4


def _rec_call(body, ins, outs, plan, state_shape, name):
    nb, ng, order, descending = plan
    hh = ins[0].shape[0]
    hb = _div_le(hh, REC_HEADS)

    def spec(shape):
        return pl.BlockSpec((hb, nb) + tuple(shape[2:]), lambda h, g: (h, order(g), 0, 0))

    return pl.pallas_call(
        functools.partial(body, hb, nb, descending),
        out_shape=outs,
        grid=(hh // hb, ng),
        in_specs=[spec(t.shape) for t in ins],
        out_specs=[spec(t.shape) for t in outs],
        scratch_shapes=[pltpu.VMEM((hb,) + tuple(state_shape), F32)],
        compiler_params=_params("parallel", "arbitrary"),
        name=name,
    )(*ins)


def _chunk_loop(nb, descending, step):
    def run(i, carry):
        step(nb - 1 - i if descending else i)
        return carry

    lax.fori_loop(0, nb, run, 0)


def _reset_at_start(state):
    @pl.when(pl.program_id(1) == 0)
    def _():
        state[...] = jnp.zeros_like(state)


def _bf(ref, h, c):
    return ref[h, c].astype(BF16)


def _within_chunk(scores, anti):
    row = lax.broadcasted_iota(jnp.int32, scores.shape, 0)
    col = lax.broadcasted_iota(jnp.int32, scores.shape, 1)
    return jnp.where(row <= col if anti else row >= col, scores, 0.0)


def _hgrn_rec_fwd(anti, nc, q_in, k_out, v, decay, qs, ks):
    hh, n, _, dk = q_in.shape
    dv = v.shape[-1]

    def body(hb, nb, descending, q_ref, k_ref, v_ref, d_ref, qs_ref, ks_ref, o_ref, s_ref, st):
        _reset_at_start(st)
        heads = range(hb)

        def step(c):
            s = [st[h] for h in heads]
            vb = [_bf(v_ref, h, c) for h in heads]
            scores = [_within_chunk(_dot(_bf(qs_ref, h, c), _bf(ks_ref, h, c), NT), anti) for h in heads]
            for h in heads:
                s_ref[h, c] = s[h]
                o_ref[h, c] = (_dot(_bf(q_ref, h, c), s[h].astype(BF16), NT)
                               + _dot(scores[h].astype(BF16), vb[h], NN))
            for h in heads:
                st[h] = s[h] * d_ref[h, c] + _dot(vb[h], _bf(k_ref, h, c), TN)

        _chunk_loop(nb, descending, step)

    ins = (q_in, k_out, v, decay, qs, ks)
    outs = (jax.ShapeDtypeStruct(v.shape, F32), jax.ShapeDtypeStruct((hh, n, dv, dk), F32))
    return _rec_call(body, ins, outs, _rec_plan(n, nc, anti, False), (dv, dk), "hgrn_rec_fwd")


def _hgrn_rec_bwd(anti, nc, q_in, k_out, v, decay, qs, ks, states, do):
    dk = q_in.shape[-1]
    n, dv = v.shape[1], v.shape[-1]

    def body(hb, nb, descending, q_ref, k_ref, v_ref, d_ref, qs_ref, ks_ref, s_ref, do_ref,
             dq_ref, dk_ref, dv_ref, dd_ref, dqs_ref, dks_ref, dst):
        _reset_at_start(dst)
        heads = range(hb)

        def step(c):
            ds = [dst[h] for h in heads]
            dsb = [x.astype(BF16) for x in ds]
            s = [s_ref[h, c] for h in heads]
            dob = [_bf(do_ref, h, c) for h in heads]
            vb = [_bf(v_ref, h, c) for h in heads]
            qsb = [_bf(qs_ref, h, c) for h in heads]
            ksb = [_bf(ks_ref, h, c) for h in heads]
            scores = [_within_chunk(_dot(qsb[h], ksb[h], NT), anti).astype(BF16) for h in heads]
            dscores = [_within_chunk(_dot(dob[h], vb[h], NT), anti).astype(BF16) for h in heads]
            for h in heads:
                dqs_ref[h, c] = _dot(dscores[h], ksb[h], NN)
                dks_ref[h, c] = _dot(dscores[h], qsb[h], TN)
                dq_ref[h, c] = _dot(dob[h], s[h].astype(BF16), NN)
                dk_ref[h, c] = _dot(vb[h], dsb[h], NN)
                dv_ref[h, c] = _dot(_bf(k_ref, h, c), dsb[h], NT) + _dot(scores[h], dob[h], TN)
                dd_ref[h, c] = jnp.sum(ds[h] * s[h], axis=0, keepdims=True)
            for h in heads:
                dst[h] = ds[h] * d_ref[h, c] + _dot(dob[h], _bf(q_ref, h, c), TN)

        _chunk_loop(nb, descending, step)

    ins = (q_in, k_out, v, decay, qs, ks, states, do)
    outs = tuple(jax.ShapeDtypeStruct(t.shape, F32) for t in (q_in, k_out, v, decay, qs, ks))
    return _rec_call(body, ins, outs, _rec_plan(n, nc, anti, True), (dv, dk), "hgrn_rec_bwd")


@functools.partial(jax.custom_vjp, nondiff_argnums=(0, 1))
def _hgrn_rec(anti, nc, q_in, k_out, v, decay, qs, ks):
    return _hgrn_rec_fwd(anti, nc, q_in, k_out, v, decay, qs, ks)[0]


def _hgrn_rec_vjp_fwd(anti, nc, q_in, k_out, v, decay, qs, ks):
    o, states = _hgrn_rec_fwd(anti, nc, q_in, k_out, v, decay, qs, ks)
    return o, (q_in, k_out, v, decay, qs, ks, states)


def _hgrn_rec_vjp_bwd(anti, nc, res, do):
    return _hgrn_rec_bwd(anti, nc, *res, do)


_hgrn_rec.defvjp(_hgrn_rec_vjp_fwd, _hgrn_rec_vjp_bwd)


def _gdn_rec_fwd(anti, nc, u, w, qk, qd, kd, gl):
    hh, n, _, dk = w.shape
    dv = u.shape[-1]

    def body(hb, nb, descending, u_ref, w_ref, qk_ref, qd_ref, kd_ref, gl_ref, o_ref, s_ref, vn_ref, st):
        _reset_at_start(st)
        heads = range(hb)

        def step(c):
            s = [st[h] for h in heads]
            sb = [x.astype(BF16) for x in s]
            vn = [u_ref[h, c] - _dot(_bf(w_ref, h, c), sb[h], NT) for h in heads]
            vnb = [x.astype(BF16) for x in vn]
            for h in heads:
                s_ref[h, c] = s[h]
                vn_ref[h, c] = vn[h]
                o_ref[h, c] = _dot(_bf(qd_ref, h, c), sb[h], NT) + _dot(_bf(qk_ref, h, c), vnb[h], NN)
            for h in heads:
                st[h] = s[h] * gl_ref[h, c] + _dot(vnb[h], _bf(kd_ref, h, c), TN)

        _chunk_loop(nb, descending, step)

    ins = (u, w, qk, qd, kd, gl)
    outs = (
        jax.ShapeDtypeStruct(u.shape, F32),
        jax.ShapeDtypeStruct((hh, n, dv, dk), F32),
        jax.ShapeDtypeStruct(u.shape, F32),
    )
    return _rec_call(body, ins, outs, _rec_plan(n, nc, anti, False), (dv, dk), "gdn_rec_fwd")


def _gdn_rec_bwd(anti, nc, w, qk, qd, kd, gl, states, vnew, do):
    n, dk = w.shape[1], w.shape[-1]
    dv = vnew.shape[-1]

    def body(hb, nb, descending, w_ref, qk_ref, qd_ref, kd_ref, gl_ref, s_ref, vn_ref, do_ref,
             du_ref, dw_ref, dqk_ref, dqd_ref, dkd_ref, dgl_ref, dst):
        _reset_at_start(dst)
        heads = range(hb)

        def step(c):
            ds = [dst[h] for h in heads]
            dsb = [x.astype(BF16) for x in ds]
            s = [s_ref[h, c] for h in heads]
            sb = [x.astype(BF16) for x in s]
            vnb = [_bf(vn_ref, h, c) for h in heads]
            dob = [_bf(do_ref, h, c) for h in heads]
            dvn = [_dot(_bf(qk_ref, h, c), dob[h], TN) + _dot(_bf(kd_ref, h, c), dsb[h], NT) for h in heads]
            dvnb = [x.astype(BF16) for x in dvn]
            for h in heads:
                du_ref[h, c] = dvn[h]
                dw_ref[h, c] = -_dot(dvnb[h], sb[h], NN)
                dqk_ref[h, c] = _dot(dob[h], vnb[h], NT)
                dqd_ref[h, c] = _dot(dob[h], sb[h], NN)
                dkd_ref[h, c] = _dot(vnb[h], dsb[h], NN)
                dgl_ref[h, c] = jnp.sum(ds[h] * s[h], axis=0, keepdims=True)
            for h in heads:
                dst[h] = (ds[h] * gl_ref[h, c] + _dot(dob[h], _bf(qd_ref, h, c), TN)
                          - _dot(dvnb[h], _bf(w_ref, h, c), TN))

        _chunk_loop(nb, descending, step)

    ins = (w, qk, qd, kd, gl, states, vnew, do)
    outs = tuple(jax.ShapeDtypeStruct(t.shape, F32) for t in (vnew, w, qk, qd, kd, gl))
    return _rec_call(body, ins, outs, _rec_plan(n, nc, anti, True), (dv, dk), "gdn_rec_bwd")


@functools.partial(jax.custom_vjp, nondiff_argnums=(0, 1))
def _gdn_rec(anti, nc, u, w, qk, qd, kd, gl):
    return _gdn_rec_fwd(anti, nc, u, w, qk, qd, kd, gl)[0]


def _gdn_rec_vjp_fwd(anti, nc, u, w, qk, qd, kd, gl):
    o, states, vnew = _gdn_rec_fwd(anti, nc, u, w, qk, qd, kd, gl)
    return o, (w, qk, qd, kd, gl, states, vnew)


def _gdn_rec_vjp_bwd(anti, nc, res, do):
    return _gdn_rec_bwd(anti, nc, *res, do)


_gdn_rec.defvjp(_gdn_rec_vjp_fwd, _gdn_rec_vjp_bwd)


def _split(a):
    hi = a.astype(BF16)
    return hi, (a - hi.astype(F32)).astype(BF16)


def _dot3(a, b, dims=NN):
    ah, al = _split(a)
    bh, bl = _split(b)
    return _dot(ah, bh, dims) + (_dot(ah, bl, dims) + _dot(al, bh, dims))


INV_BASE = 8


def _inv_unit_triangular(mats):
    size = mats[0].shape[0]
    row = lax.broadcasted_iota(jnp.int32, (size, size), 0)
    col = lax.broadcasted_iota(jnp.int32, (size, size), 1)
    eye = jnp.where(row == col, 1.0, 0.0).astype(F32)

    def same_block(width):
        shift = width.bit_length() - 1
        return jnp.right_shift(row, shift) == jnp.right_shift(col, shift)

    base = [jnp.where(same_block(INV_BASE), m, 0.0) for m in mats]
    sq = [_dot3(b, b) for b in base]
    inv = [_dot3(eye - b, eye + s) for b, s in zip(base, sq)]
    width = 4
    while width < INV_BASE:
        sq = [_dot3(s, s) for s in sq]
        inv = [_dot3(i, eye + s) for i, s in zip(inv, sq)]
        width *= 2
    while width < size:
        outer = same_block(2 * width) & jnp.logical_not(same_block(width))
        part = [_dot3(i, jnp.where(outer, m, 0.0)) for i, m in zip(inv, mats)]
        inv = [i - _dot3(p, i) for i, p in zip(inv, part)]
        width *= 2
    return inv


PREP_CHUNKS = 6


def _prep_call(body, ins, outs, name):
    nb = _div_le(ins[0].shape[1], PREP_CHUNKS)

    def spec(shape):
        return pl.BlockSpec((1, nb) + tuple(shape[2:]), lambda h, g: (h, g, 0, 0))

    return pl.pallas_call(
        functools.partial(body, nb),
        out_shape=outs,
        grid=(ins[0].shape[0], ins[0].shape[1] // nb),
        in_specs=[spec(t.shape) for t in ins],
        out_specs=[spec(t.shape) for t in outs],
        compiler_params=_params("parallel", "parallel"),
        name=name,
    )(*ins)


def _off_diagonal(x):
    row = lax.broadcasted_iota(jnp.int32, x.shape, 0)
    col = lax.broadcasted_iota(jnp.int32, x.shape, 1)
    return jnp.where(row == col, 0.0, x)


def _gdn_prep_fwd(kb, k, vb, kbg, q, decay):
    def body(nb, kb_ref, k_ref, vb_ref, kbg_ref, q_ref, dec_ref, u_ref, w_ref, qk_ref, t_ref):
        chunks = range(nb)
        kc = [k_ref[0, c].astype(BF16) for c in chunks]
        dec = [dec_ref[0, c] for c in chunks]
        a = [_off_diagonal(_dot(kb_ref[0, c].astype(BF16), kc[c], NT) * dec[c]) for c in chunks]
        t = _inv_unit_triangular(a)
        for c in chunks:
            tb = t[c].astype(BF16)
            t_ref[0, c] = t[c]
            u_ref[0, c] = _dot(tb, vb_ref[0, c].astype(BF16), NN)
            w_ref[0, c] = _dot(tb, kbg_ref[0, c].astype(BF16), NN)
            qk_ref[0, c] = _dot(q_ref[0, c].astype(BF16), kc[c], NT) * dec[c]

    ins = (kb, k, vb, kbg, q, decay)
    outs = tuple(jax.ShapeDtypeStruct(t.shape, F32) for t in (vb, kbg, decay, decay))
    return _prep_call(body, ins, outs, "gdn_prep_fwd")


def _gdn_prep_bwd(kb, k, vb, kbg, q, decay, t, du, dw, dqk):
    def body(nb, kb_ref, k_ref, vb_ref, kbg_ref, q_ref, dec_ref, t_ref, du_ref, dw_ref, dqk_ref,
             dkb_ref, dk_ref, dvb_ref, dkbg_ref, dq_ref, ddec_ref):
        chunks = range(nb)
        kbb = [_bf(kb_ref, 0, c) for c in chunks]
        kc = [_bf(k_ref, 0, c) for c in chunks]
        qc = [_bf(q_ref, 0, c) for c in chunks]
        tb = [_bf(t_ref, 0, c) for c in chunks]
        dub = [_bf(du_ref, 0, c) for c in chunks]
        dwb = [_bf(dw_ref, 0, c) for c in chunks]
        for c in chunks:
            dvb_ref[0, c] = _dot(tb[c], dub[c], TN)
            dkbg_ref[0, c] = _dot(tb[c], dwb[c], TN)
        dt = [_dot(dub[c], _bf(vb_ref, 0, c), NT) + _dot(dwb[c], _bf(kbg_ref, 0, c), NT) for c in chunks]
        left = [_dot(tb[c], dt[c].astype(BF16), TN).astype(BF16) for c in chunks]
        da = [-_off_diagonal(_dot(left[c], tb[c], NT)) for c in chunks]
        dqk = [dqk_ref[0, c] for c in chunks]
        for c in chunks:
            ddec_ref[0, c] = da[c] * _dot(kbb[c], kc[c], NT) + dqk[c] * _dot(qc[c], kc[c], NT)
        dkk = [(da[c] * dec_ref[0, c]).astype(BF16) for c in chunks]
        dqkd = [(dqk[c] * dec_ref[0, c]).astype(BF16) for c in chunks]
        for c in chunks:
            dkb_ref[0, c] = _dot(dkk[c], kc[c], NN)
            dk_ref[0, c] = _dot(dkk[c], kbb[c], TN) + _dot(dqkd[c], qc[c], TN)
            dq_ref[0, c] = _dot(dqkd[c], kc[c], NN)

    ins = (kb, k, vb, kbg, q, decay, t, du, dw, dqk)
    outs = tuple(jax.ShapeDtypeStruct(x.shape, F32) for x in (kb, k, vb, kbg, q, decay))
    return _prep_call(body, ins, outs, "gdn_prep_bwd")


@jax.custom_vjp
def _gdn_prep(kb, k, vb, kbg, q, decay):
    return _gdn_prep_fwd(kb, k, vb, kbg, q, decay)[:3]


def _gdn_prep_vjp_fwd(kb, k, vb, kbg, q, decay):
    u, w, qk, t = _gdn_prep_fwd(kb, k, vb, kbg, q, decay)
    return (u, w, qk), (kb, k, vb, kbg, q, decay, t)


def _gdn_prep_vjp_bwd(res, cot):
    return _gdn_prep_bwd(*res, *cot)


_gdn_prep.defvjp(_gdn_prep_vjp_fwd, _gdn_prep_vjp_bwd)


def _loss_head(y, target):
    rows, d = y.shape
    tr = _tile(rows, 256, SUBLANE)

    def body(y_ref, t_ref, dy_ref, sq_ref):
        diff = y_ref[...] - t_ref[...]
        dy_ref[...] = diff * (1.0 / d)
        part = jnp.sum((diff * diff).reshape(tr // SUBLANE, SUBLANE, d), axis=0)

        @pl.when(pl.program_id(0) == 0)
        def _():
            sq_ref[...] = part

        @pl.when(pl.program_id(0) > 0)
        def _():
            sq_ref[...] += part

    row = pl.BlockSpec((tr, d), lambda i: (i, 0))
    return pl.pallas_call(
        body,
        out_shape=(jax.ShapeDtypeStruct((rows, d), F32), jax.ShapeDtypeStruct((SUBLANE, d), F32)),
        grid=(rows // tr,),
        in_specs=[row, row],
        out_specs=[row, pl.BlockSpec((SUBLANE, d), lambda i: (0, 0))],
        compiler_params=_params("arbitrary"),
        name="loss_head",
    )(y, target)


def _add_half(g, r1, my_c):
    nblk, half, cols = r1.shape
    tr = _tile(half, 128, 2 * SUBLANE)
    per_half = half // tr

    def body(c_ref, g_ref, r_ref, o_ref):
        o_ref[...] = (g_ref[...] + r_ref[...]).astype(BF16)

    grid_spec = pltpu.PrefetchScalarGridSpec(
        num_scalar_prefetch=1,
        grid=(nblk, per_half),
        in_specs=[
            pl.BlockSpec((1, tr, cols), lambda b, i, c_ref: (b, c_ref[0] * per_half + i, 0)),
            pl.BlockSpec((1, tr, cols), lambda b, i, c_ref: (b, i, 0)),
        ],
        out_specs=pl.BlockSpec((1, tr, cols), lambda b, i, c_ref: (b, i, 0)),
    )
    return pl.pallas_call(
        body,
        out_shape=jax.ShapeDtypeStruct(r1.shape, BF16),
        grid_spec=grid_spec,
        compiler_params=_params("parallel", "parallel"),
        name="grad_add_half",
    )(my_c.reshape(1).astype(jnp.int32), g, r1)


def _sum_chips(own, slots, place):
    _, half, cols = own.shape
    tr = _tile(half, 128, 2 * SUBLANE)
    per_half = half // tr

    def body(p_ref, a_ref, b_ref, c_ref, d_ref, o_ref):
        f32 = lambda ref: ref[0].astype(F32)
        o_ref[...] = (f32(a_ref) + f32(b_ref)) + (f32(c_ref) + f32(d_ref))

    def block(k):
        return pl.BlockSpec((1, tr, cols), lambda i, p_ref: (p_ref[k], i, 0))

    grid_spec = pltpu.PrefetchScalarGridSpec(
        num_scalar_prefetch=1,
        grid=(per_half,),
        in_specs=[block(0), block(1), block(2), block(3)],
        out_specs=pl.BlockSpec((tr, cols), lambda i, p_ref: (p_ref[4] * per_half + i, 0)),
    )
    return pl.pallas_call(
        body,
        out_shape=jax.ShapeDtypeStruct((2 * half, cols), F32),
        grid_spec=grid_spec,
        compiler_params=_params("parallel"),
        name="grad_sum_chips",
    )(place, own, slots, slots, slots)


def _cast_into_block(w, chip):
    rows, cols = w.shape
    tr = _tile(rows, 128, 2 * SUBLANE)

    def body(j_ref, w_ref, o_ref):
        o_ref[0] = w_ref[...].astype(BF16)

    grid_spec = pltpu.PrefetchScalarGridSpec(
        num_scalar_prefetch=1,
        grid=(rows // tr,),
        in_specs=[pl.BlockSpec((tr, cols), lambda i, j_ref: (i, 0))],
        out_specs=pl.BlockSpec((1, tr, cols), lambda i, j_ref: (j_ref[0], i, 0)),
    )
    return pl.pallas_call(
        body,
        out_shape=jax.ShapeDtypeStruct((N_CHIP, rows, cols), BF16),
        grid_spec=grid_spec,
        compiler_params=_params("parallel"),
        name="cast_weight_shard",
    )(chip.reshape(1).astype(jnp.int32), w)


def _sum_leading(x, name):
    k, rows, cols = x.shape
    tr = _tile(rows, 128, SUBLANE)

    def body(x_ref, o_ref):
        parts = [x_ref[i] for i in range(k)]
        while len(parts) > 1:
            parts = [parts[i] + parts[i + 1] for i in range(0, len(parts), 2)]
        o_ref[...] = parts[0]

    return pl.pallas_call(
        body,
        out_shape=jax.ShapeDtypeStruct((rows, cols), F32),
        grid=(rows // tr,),
        in_specs=[pl.BlockSpec((k, tr, cols), lambda i: (0, i, 0))],
        out_specs=pl.BlockSpec((tr, cols), lambda i: (i, 0)),
        compiler_params=_params("parallel"),
        name=name,
    )(x)


def _adamw(w, g, m, v, name):
    rows, cols = w.shape
    tr = _tile(rows, 128, SUBLANE)
    m_scale = 1.0 / (1.0 - ADAM_B1 ** ADAM_STEP)
    v_scale = 1.0 / (1.0 - ADAM_B2 ** ADAM_STEP)

    def body(w_ref, g_ref, m_ref, v_ref, d_ref, nm_ref, nv_ref):
        gg = g_ref[...]
        nm = ADAM_B1 * m_ref[...] + (1.0 - ADAM_B1) * gg
        nv = ADAM_B2 * v_ref[...] + (1.0 - ADAM_B2) * (gg * gg)
        nm_ref[...] = nm
        nv_ref[...] = nv
        d_ref[...] = -ADAM_LR * ((nm * m_scale) / (jnp.sqrt(nv * v_scale) + ADAM_EPS) + ADAM_WD * w_ref[...])

    spec = pl.BlockSpec((tr, cols), lambda i: (i, 0))
    out = jax.ShapeDtypeStruct((rows, cols), F32)
    return pl.pallas_call(
        body,
        out_shape=(out, out, out),
        grid=(rows // tr,),
        in_specs=[spec] * 4,
        out_specs=[spec] * 3,
        compiler_params=_params("parallel"),
        name=name,
    )(w, g, m, v)


def _ada_fwd(a16, w):
    layers, d, n = w.shape
    tn = _tile(n, 512)

    def body(a_ref, w_ref, o_ref):
        o_ref[0] = _dot(a_ref[...].astype(BF16), w_ref[0].astype(BF16), NN)

    return pl.pallas_call(
        body,
        out_shape=jax.ShapeDtypeStruct((layers, a16.shape[0], n), F32),
        grid=(layers, n // tn),
        in_specs=[pl.BlockSpec(a16.shape, lambda l, j: (0, 0)), pl.BlockSpec((1, d, tn), lambda l, j: (l, 0, j))],
        out_specs=pl.BlockSpec((1, a16.shape[0], tn), lambda l, j: (l, 0, j)),
        compiler_params=_params("parallel", "parallel"),
        name="ada_fwd",
    )(a16, w)


def _ada_bwd(a16, dm, w):
    layers, d, n = w.shape
    tn = _tile(n, 512)
    rows = a16.shape[0]

    def body(a_ref, dm_ref, w_ref, gw_ref, ga_ref):
        dmb = dm_ref[0].astype(BF16)
        gw_ref[0] = _dot(a_ref[...].astype(BF16), dmb, TN)
        part = _dot(dmb, w_ref[0].astype(BF16), NT)
        first = (pl.program_id(0) == 0) & (pl.program_id(1) == 0)

        @pl.when(first)
        def _():
            ga_ref[...] = part

        @pl.when(jnp.logical_not(first))
        def _():
            ga_ref[...] += part

    return pl.pallas_call(
        body,
        out_shape=(jax.ShapeDtypeStruct(w.shape, F32), jax.ShapeDtypeStruct((rows, d), F32)),
        grid=(layers, n // tn),
        in_specs=[
            pl.BlockSpec((rows, d), lambda l, j: (0, 0)),
            pl.BlockSpec((1, rows, tn), lambda l, j: (l, 0, j)),
            pl.BlockSpec((1, d, tn), lambda l, j: (l, 0, j)),
        ],
        out_specs=[pl.BlockSpec((1, d, tn), lambda l, j: (l, 0, j)), pl.BlockSpec((rows, d), lambda l, j: (0, 0))],
        compiler_params=_params("arbitrary", "arbitrary"),
        name="ada_bwd",
    )(a16, dm, w)


def _place():
    ix, iy, ic = lax.axis_index("x"), lax.axis_index("y"), lax.axis_index("c")
    return ix, iy, ic


def _flip(coord, bit):
    return 1 - coord if bit else coord


def _remote(src, dst, send_sem, recv_sem, to):
    return pltpu.make_async_remote_copy(
        src_ref=src, dst_ref=dst, send_sem=send_sem, recv_sem=recv_sem, device_id=to, device_id_type=MESH)


def _all_gather8(x, name):
    rows, cols = x.shape

    def body(x_ref, o_ref, send_sems, recv_sems):
        ix, iy, ic = _place()
        me = 4 * ix + 2 * iy + ic
        o_ref[me] = x_ref[...]
        peers = []
        for p in range(1, N_DEV):
            to = (_flip(ix, p & 4), _flip(iy, p & 2), _flip(ic, p & 1))
            peers.append((to, 4 * to[0] + 2 * to[1] + to[2]))
        sends = [_remote(x_ref, o_ref.at[me], send_sems.at[p], recv_sems.at[p], to)
                 for p, (to, _) in enumerate(peers)]
        for cp in sends:
            cp.start()
        for p, (to, slot) in enumerate(peers):
            _remote(x_ref, o_ref.at[slot], send_sems.at[p], recv_sems.at[p], to).wait_recv()
        for cp in sends:
            cp.wait_send()

    return pl.pallas_call(
        body,
        out_shape=jax.ShapeDtypeStruct((N_DEV, rows, cols), F32),
        in_specs=[pl.BlockSpec(memory_space=pltpu.VMEM)],
        out_specs=pl.BlockSpec(memory_space=pltpu.VMEM),
        scratch_shapes=[pltpu.SemaphoreType.DMA((N_DEV - 1,)), pltpu.SemaphoreType.DMA((N_DEV - 1,))],
        compiler_params=pltpu.CompilerParams(vmem_limit_bytes=VMEM_LIMIT),
        name=name,
    )(x)


def _other_chips(ix, iy):
    chips = [(1 - ix, iy), (ix, 1 - iy), (1 - ix, 1 - iy)]
    return [(cx, cy, 2 * cx + cy) for cx, cy in chips]


def _hbm_call(body, ins, out_shapes, scratch, name, in_place=False):
    hbm = pl.BlockSpec(memory_space=pltpu.HBM)
    return pl.pallas_call(
        body,
        out_shape=out_shapes,
        in_specs=[hbm] * len(ins),
        out_specs=[hbm] * len(out_shapes),
        scratch_shapes=scratch,
        input_output_aliases={t: t for t in range(len(ins))} if in_place else {},
        compiler_params=pltpu.CompilerParams(vmem_limit_bytes=VMEM_LIMIT),
        name=name,
    )(*ins)


def _gather_weights(blocks):
    nt = len(blocks)

    def body(*refs):
        ins, outs = refs[:nt], refs[nt:2 * nt]
        ici_send, ici_recv, d2d_send, d2d_recv = refs[2 * nt:]
        ix, iy, ic = _place()
        j = 2 * ix + iy
        sibling = (ix, iy, 1 - ic)
        chips = _other_chips(ix, iy)
        sends, waits = [], []
        for t in range(nt):
            half = ins[t].shape[1] // 2
            mine = pl.ds(ic * half, half)
            for q, (cx, cy, _) in enumerate(chips):
                cp = _remote(ins[t].at[j, mine], outs[t].at[j, mine], ici_send.at[3 * t + q], ici_recv.at[3 * t + q],
                             (cx, cy, ic))
                cp.start()
                sends.append(cp)
        for q, (cx, cy, jq) in enumerate(chips):
            for t in range(nt):
                half = ins[t].shape[1] // 2
                mine = pl.ds(ic * half, half)
                theirs = pl.ds((1 - ic) * half, half)
                k = 3 * t + q
                _remote(ins[t].at[j, mine], outs[t].at[jq, mine], ici_send.at[k], ici_recv.at[k],
                        (cx, cy, ic)).wait_recv()
                cp = _remote(outs[t].at[jq, mine], outs[t].at[jq, mine], d2d_send.at[k], d2d_recv.at[k], sibling)
                cp.start()
                sends.append(cp)
                waits.append(_remote(outs[t].at[jq, theirs], outs[t].at[jq, theirs], d2d_send.at[k], d2d_recv.at[k],
                                     sibling))
        for cp in waits:
            cp.wait_recv()
        for cp in sends:
            cp.wait_send()

    dma = pltpu.SemaphoreType.DMA
    outs = tuple(jax.ShapeDtypeStruct(b.shape, b.dtype) for b in blocks)
    return _hbm_call(body, blocks, outs, [dma((3 * nt,))] * 4, "gather_weights", in_place=True)


def _exchange_halves(grads):
    nt = len(grads)

    def body(*refs):
        ins, outs = refs[:nt], refs[nt:2 * nt]
        send_sems, recv_sems = refs[2 * nt:]
        ix, iy, ic = _place()
        sibling = (ix, iy, 1 - ic)
        copies = []
        for t in range(nt):
            half = ins[t].shape[1] // 2
            theirs = pl.ds((1 - ic) * half, half)
            cp = _remote(ins[t].at[:, theirs, :], outs[t], send_sems.at[t], recv_sems.at[t], sibling)
            cp.start()
            copies.append(cp)
        for cp in copies:
            cp.wait_recv()
        for cp in copies:
            cp.wait_send()

    dma = pltpu.SemaphoreType.DMA
    outs = tuple(jax.ShapeDtypeStruct((g.shape[0], g.shape[1] // 2, g.shape[2]), F32) for g in grads)
    return _hbm_call(body, grads, outs, [dma((nt,)), dma((nt,))], "grad_exchange_halves")


def _exchange_blocks(parts):
    nt = len(parts)

    def body(*refs):
        ins, outs = refs[:nt], refs[nt:2 * nt]
        send_sems, recv_sems = refs[2 * nt:]
        ix, iy, ic = _place()
        j = 2 * ix + iy
        chips = _other_chips(ix, iy)
        sends, waits = [], []
        for t in range(nt):
            for q, (cx, cy, jq) in enumerate(chips):
                k = 3 * t + q
                cp = _remote(ins[t].at[jq], outs[t].at[j], send_sems.at[k], recv_sems.at[k], (cx, cy, ic))
                cp.start()
                sends.append(cp)
                waits.append(_remote(ins[t].at[jq], outs[t].at[jq], send_sems.at[k], recv_sems.at[k], (cx, cy, ic)))
        for cp in waits:
            cp.wait_recv()
        for cp in sends:
            cp.wait_send()

    dma = pltpu.SemaphoreType.DMA
    outs = tuple(jax.ShapeDtypeStruct(p.shape, p.dtype) for p in parts)
    return _hbm_call(body, parts, outs, [dma((3 * nt,)), dma((3 * nt,))], "grad_exchange_blocks")


def _share_halves(blocks):
    nt = len(blocks)

    def body(*refs):
        ins, outs = refs[:nt], refs[nt:2 * nt]
        send_sems, recv_sems = refs[2 * nt:]
        ix, iy, ic = _place()
        sibling = (ix, iy, 1 - ic)
        sends, waits = [], []
        for t in range(nt):
            half = ins[t].shape[0] // 2
            mine = pl.ds(ic * half, half)
            theirs = pl.ds((1 - ic) * half, half)
            cp = _remote(ins[t].at[mine], outs[t].at[mine], send_sems.at[t], recv_sems.at[t], sibling)
            cp.start()
            sends.append(cp)
            waits.append(_remote(ins[t].at[theirs], outs[t].at[theirs], send_sems.at[t], recv_sems.at[t], sibling))
        for cp in waits:
            cp.wait_recv()
        for cp in sends:
            cp.wait_send()

    dma = pltpu.SemaphoreType.DMA
    outs = tuple(jax.ShapeDtypeStruct(b.shape, F32) for b in blocks)
    return _hbm_call(body, blocks, outs, [dma((nt,)), dma((nt,))], "grad_share_halves", in_place=True)


def _reduce_grads(grads, ix, iy, ic):
    recv = _exchange_halves(grads)
    chip_sums = [_add_half(g, r1, ic) for g, r1 in zip(grads, recv)]
    slots = _exchange_blocks(chip_sums)
    place = jnp.stack([2 * ix + iy] + [jq for _, _, jq in _other_chips(ix, iy)] + [ic]).astype(jnp.int32)
    return _share_halves([_sum_chips(own, got, place) for own, got in zip(chip_sums, slots)])


def _rmsnorm(x, w):
    return x * lax.rsqrt(jnp.mean(x * x, axis=-1, keepdims=True) + EPS) * w


def _l2norm(x):
    return x * lax.rsqrt(jnp.sum(x * x, axis=-1, keepdims=True) + EPS)


def _heads(t, d):
    return t.reshape(t.shape[:-1] + (t.shape[-1] // d, d))


def _chunks(t, chunk):
    n = t.shape[0] // chunk
    return t.reshape(n, chunk, t.shape[1], t.shape[2]).transpose(2, 0, 1, 3)


def _unchunks(t):
    hh, n, chunk, d = t.shape
    return t.transpose(1, 2, 0, 3).reshape(n * chunk, hh, d)


def _triangle(size, anti):
    ones = jnp.ones((size, size), bool)
    return jnp.triu(ones) if anti else jnp.tril(ones)


def _hgrn2_scan(q, k, v, log_f, n_ctx, anti):
    q, k, v, log_f = (_chunks(t, A_CHUNK) for t in (q, k, v, log_f))
    b = lax.cumsum(log_f, axis=2, reverse=anti)
    end = 0 if anti else A_CHUNK - 1
    mid = A_CHUNK // 2 if anti else A_CHUNK // 2 - 1
    b_last = b[:, :, end:end + 1, :]
    b_mid = b[:, :, mid:mid + 1, :]
    return _unchunks(_hgrn_rec(anti, n_ctx // A_CHUNK, q * jnp.exp(b), k * jnp.exp(b_last - b), v, jnp.exp(b_last),
                               q * jnp.exp(b - b_mid), k * jnp.exp(b_mid - b)))


def _gdn_scan(q, k, v, g, beta, n_ctx, anti):
    q, k, v = (_chunks(t, C_CHUNK) for t in (q, k, v))
    g = _chunks(g[..., None], C_CHUNK)[..., 0]
    beta = _chunks(beta[..., None], C_CHUNK)[..., 0]
    gc = lax.cumsum(g, axis=2, reverse=anti)
    end = 0 if anti else C_CHUNK - 1
    causal = _triangle(C_CHUNK, anti)
    diff = gc[..., :, None] - gc[..., None, :]
    decay = jnp.where(causal, jnp.exp(jnp.where(causal, diff, 0.0)), 0.0)
    k_beta = k * beta[..., None]
    u, w, qk = _gdn_prep(k_beta, k, v * beta[..., None], k_beta * jnp.exp(gc)[..., None], q, decay)
    q_dec = q * jnp.exp(gc)[..., None]
    k_dec = k * jnp.exp(gc[..., end:end + 1] - gc)[..., None]
    g_last = jnp.exp(gc[..., end])
    gl = jnp.broadcast_to(g_last[..., None, None], g_last.shape + (1, q.shape[-1]))
    return _unchunks(_gdn_rec(anti, n_ctx // C_CHUNK, u, w, qk, q_dec, k_dec, gl))


def _bidirectional(scan_fn, fwd_args, bwd_args, n_ctx):
    return scan_fn(*fwd_args, n_ctx, False) + scan_fn(*bwd_args, n_ctx, True)


def _multiscale_pool(u, pool_w, pool_scale):
    rows, length, width = u.shape
    groups = len(POOL_WINDOWS)
    gdim = width // groups
    uf = u.reshape(rows, length, groups, gdim)
    cs = jnp.concatenate([jnp.zeros_like(uf[:, :1]), jnp.cumsum(uf, axis=1)], axis=1)
    pos = np.arange(length)
    mixed = []
    for gi, win in enumerate(POOL_WINDOWS):
        below, above = win // 2, win - win // 2
        lo = np.clip(pos - below, 0, length - 1)
        hi = np.clip(pos + win - 1 - below, 0, length - 1)
        cnt = jnp.asarray((hi - lo + 1).astype(np.float32))[None, :, None]
        csg = cs[:, :, gi, :]
        upper = jnp.concatenate([csg[:, above:]] + [csg[:, length:]] * (above - 1), axis=1)
        lower = jnp.concatenate([jnp.zeros_like(csg[:, :below]), csg[:, :length - below]], axis=1)
        mixed.append((upper - lower) / cnt - uf[:, :, gi, :])
    d = jnp.stack(mixed, axis=2)
    y = jnp.einsum("rlgc,gcd->rlgd", d, pool_w)
    return y.reshape(rows, length, width) * pool_scale


def _short_conv(u, w):
    length = u.shape[0]
    left = C_CONV // 2
    up = jnp.pad(u, ((left, C_CONV - 1 - left), (0, 0)))
    out = up[0:length] * w[0]
    for j in range(1, C_CONV):
        out = out + up[j:j + length] * w[j]
    return out


def _hgrn2_gates(pre_f, lb):
    log_f = jnp.log(lb + (1.0 - lb) * jax.nn.sigmoid(pre_f))
    k = (1.0 - lb) * jax.nn.sigmoid(-pre_f)
    return _heads(k, A_HEAD_DIM), _heads(log_f, A_HEAD_DIM)


def _even_mixer(h, n_ctx, lb, w_in, w_in_hook, a_norm, pool_w, pool_scale, w_out, w_out_hook):
    a_width = w_out.shape[0] // 2
    p = _dense(h, w_in, w_in_hook)
    q, f_f, f_b, i, g, u = jnp.split(p, [a_width * s for s in range(1, 6)], axis=-1)
    q = _heads(jax.nn.silu(q), A_HEAD_DIM)
    i = _heads(i, A_HEAD_DIM)
    k_f, logf_f = _hgrn2_gates(f_f, lb[0])
    k_b, logf_b = _hgrn2_gates(f_b, lb[1])
    o = _bidirectional(_hgrn2_scan, (q, k_f, i, logf_f), (q, k_b, i, logf_b), n_ctx)
    b_width = u.shape[-1]
    u_l = u[n_ctx:].reshape(-1, GRID_W, b_width)
    pooled = jnp.concatenate([
        _multiscale_pool(u[None, :n_ctx], pool_w, pool_scale)[0],
        _multiscale_pool(u_l, pool_w, pool_scale).reshape(-1, b_width),
    ], axis=0)
    a_out = _rmsnorm(o, a_norm) * jax.nn.silu(_heads(g, A_HEAD_DIM))
    a_out = a_out.reshape(a_out.shape[0], a_width)
    return _dense(jnp.concatenate([a_out, pooled], axis=-1), w_out, w_out_hook)


def _odd_mixer(h, n_ctx, w_main, w_main_hook, w_gate, w_gate_hook, conv_w, a_log, dt_bias, norm_w, w_out, w_out_hook):
    value_width = w_out.shape[0]
    key_width = value_width // 2
    rep = 2
    a_rate = jnp.exp(a_log)
    p = _dense(h, w_main, w_main_hook)
    gates = _dense(h, w_gate, w_gate_hook)
    qkv, z = p[:, :2 * key_width + value_width], p[:, 2 * key_width + value_width:]
    qkv = jnp.concatenate([_short_conv(qkv[:n_ctx], conv_w), _short_conv(qkv[n_ctx:], conv_w)], axis=0)
    qkv = jax.nn.silu(qkv)
    q, k, v = jnp.split(qkv, [key_width, 2 * key_width], axis=-1)
    q = jnp.repeat(_l2norm(_heads(q, C_HEAD_DIM)) * C_HEAD_DIM ** -0.5, rep, axis=1)
    k = jnp.repeat(_l2norm(_heads(k, C_HEAD_DIM)), rep, axis=1)
    v = _heads(v, C_HEAD_DIM)
    a_f, a_b, b_f, b_b = jnp.split(gates, 4, axis=-1)
    g_f = -a_rate[0] * jax.nn.softplus(a_f + dt_bias[0])
    g_b = -a_rate[1] * jax.nn.softplus(a_b + dt_bias[1])
    o = _bidirectional(_gdn_scan, (q, k, v, g_f, jax.nn.sigmoid(b_f)), (q, k, v, g_b, jax.nn.sigmoid(b_b)), n_ctx)
    y = _rmsnorm(o, norm_w) * jax.nn.silu(_heads(z, C_HEAD_DIM))
    return _dense(y.reshape(y.shape[0], value_width), w_out, w_out_hook)


def _swiglu(h, w13, w13_hook, w2, w2_hook):
    gate, up = jnp.split(_dense(h, w13, w13_hook), 2, axis=-1)
    return _dense(jax.nn.silu(gate) * up, w2, w2_hook)


def _forward(x, ctx, mods, big, hooks, small):
    n_ctx = ctx.shape[0]
    d = x.shape[-1]
    stream = jnp.concatenate([ctx, x], axis=0)
    is_ctx = (jnp.arange(stream.shape[0]) < n_ctx)[:, None]
    lb_all = jnp.cumsum(jax.nn.softmax(small["ev_lb"], axis=1), axis=1)
    for layer in range(2):
        m = [jnp.where(is_ctx, mods[layer, 1, s * d:(s + 1) * d][None, :], mods[layer, 0, s * d:(s + 1) * d][None, :])
             for s in range(6)]
        nw = small["norm_w"][layer]
        h = _rmsnorm(stream, nw[0]) * (1.0 + m[1]) + m[0]
        if layer == 0:
            y = _even_mixer(h, n_ctx, lb_all[:, layer], big["ev_w_in"], hooks["ev_w_in"], small["ev_a_norm"][0],
                            small["ev_pool_w"][0], small["ev_pool_scale"][0], big["ev_w_out"], hooks["ev_w_out"])
        else:
            y = _odd_mixer(h, n_ctx, big["od_w_main"], hooks["od_w_main"], big["od_w_gate"], hooks["od_w_gate"],
                           small["od_conv"][0], small["od_A_log"][0], small["od_dt_bias"][0], small["od_norm"][0],
                           big["od_w_out"], hooks["od_w_out"])
        stream = stream + m[2] * _rmsnorm(y, nw[1])
        h = _rmsnorm(stream, nw[2]) * (1.0 + m[4]) + m[3]
        f = _swiglu(h, big["ffn_w13_%d" % layer], hooks["ffn_w13_%d" % layer],
                    big["ffn_w2_%d" % layer], hooks["ffn_w2_%d" % layer])
        stream = stream + m[5] * _rmsnorm(f, nw[3])
    return stream[n_ctx:]


def _pack(arrays):
    flat = jnp.concatenate([a.reshape(-1) for a in arrays])
    pad = (-flat.shape[0]) % (SUBLANE * LANE)
    return jnp.pad(flat, (0, pad)).reshape(-1, LANE)


def _unpack(packed, shapes, lead=()):
    flat = packed.reshape(lead + (-1,))
    out, at = [], 0
    for shape in shapes:
        size = int(np.prod(shape))
        out.append(flat[..., at:at + size].reshape(lead + tuple(shape)))
        at += size
    return out


def _from_chips(gathered, axis):
    return jnp.concatenate([gathered[2 * j] for j in range(N_CHIP)], axis=axis)


def _col_natural(blocks):
    nblk, k, n = blocks.shape
    return blocks.transpose(1, 0, 2).reshape(k, nblk * n)


def _col_blocked(nat):
    k, n4 = nat.shape
    return nat.reshape(k, N_CHIP, n4 // N_CHIP).transpose(1, 0, 2)


def kernel(x, c, ctx, c_ctx, w_ada, b_ada, norm_w, ev_w_in, ev_lb, ev_a_norm, ev_pool_w, ev_pool_scale, ev_w_out, od_w_in, od_conv, od_A_log, od_dt_bias, od_norm, od_w_out, ffn_w13, ffn_w2, loss_target, m_c_ctx, m_w_ada, m_b_ada, m_norm_w, m_ev_w_in, m_ev_lb, m_ev_a_norm, m_ev_pool_w, m_ev_pool_scale, m_ev_w_out, m_od_w_in, m_od_conv, m_od_A_log, m_od_dt_bias, m_od_norm, m_od_w_out, m_ffn_w13, m_ffn_w2, v_c_ctx, v_w_ada, v_b_ada, v_norm_w, v_ev_w_in, v_ev_lb, v_ev_a_norm, v_ev_pool_w, v_ev_pool_scale, v_ev_w_out, v_od_w_in, v_od_conv, v_od_A_log, v_od_dt_bias, v_od_norm, v_od_w_out, v_ffn_w13, v_ffn_w2):
    ix, iy, ic = _place()
    me = 4 * ix + 2 * iy + ic
    chip = 2 * ix + iy
    d = x.shape[-1]
    layers = w_ada.shape[0]
    n_ada = w_ada.shape[-1]

    pre_parts = [c[0], norm_w, ev_lb, ev_pool_w, od_conv]
    pre = _all_gather8(_pack(pre_parts), "gather_small_inputs")
    c_all, norm_w_g, ev_lb_g, pool_w_g, od_conv_g = _unpack(pre, [p.shape for p in pre_parts], lead=(N_DEV,))
    small = {
        "norm_w": _from_chips(norm_w_g, 2),
        "ev_lb": _from_chips(ev_lb_g, 2),
        "ev_a_norm": ev_a_norm,
        "ev_pool_w": _from_chips(pool_w_g, 2),
        "ev_pool_scale": ev_pool_scale,
        "od_conv": _from_chips(od_conv_g, 2),
        "od_A_log": od_A_log,
        "od_dt_bias": od_dt_bias,
        "od_norm": od_norm,
    }

    silu_cc, silu_cc_vjp = jax.vjp(jax.nn.silu, c_ctx)
    a16 = jnp.concatenate([jax.nn.silu(c_all), silu_cc[None], jnp.zeros((2 * SUBLANE - N_DEV - 1, d), F32)], axis=0)
    mods_local = _ada_fwd(a16, w_ada)
    mods_g = _all_gather8(mods_local.reshape(-1, n_ada), "gather_modulation")
    mods_g = mods_g.reshape(N_DEV, layers, 2 * SUBLANE, n_ada)
    mods_full = _from_chips(mods_g, 2) + b_ada[:, None, :]
    mods = jnp.stack([lax.dynamic_index_in_dim(mods_full, me, axis=1, keepdims=False), mods_full[:, N_DEV]], axis=1)

    ffn13_rows = ffn_w13.shape[1]
    ffn2_rows = ffn_w2.shape[1]
    shards = [ev_w_in[0], ev_w_out[0], od_w_in[0], od_w_out[0],
              ffn_w13.reshape(-1, ffn_w13.shape[-1]), ffn_w2.reshape(-1, ffn_w2.shape[-1])]
    g_ev_in, g_ev_out, g_od_in, g_od_out, g_w13, g_w2 = _gather_weights([_cast_into_block(s, chip) for s in shards])
    od_nat = _col_natural(g_od_in)
    n_gate = 4 * (od_w_out.shape[1] * N_CHIP // C_HEAD_DIM)
    n_main = od_nat.shape[1] - n_gate
    big = {
        "ev_w_in": _col_natural(g_ev_in),
        "ev_w_out": g_ev_out.reshape(-1, g_ev_out.shape[-1]),
        "od_w_main": od_nat[:, :n_main],
        "od_w_gate": od_nat[:, n_main:],
        "od_w_out": g_od_out.reshape(-1, g_od_out.shape[-1]),
    }
    for layer in range(layers):
        big["ffn_w13_%d" % layer] = _col_natural(g_w13[:, layer * ffn13_rows:(layer + 1) * ffn13_rows])
        big["ffn_w2_%d" % layer] = g_w2[:, layer * ffn2_rows:(layer + 1) * ffn2_rows].reshape(-1, g_w2.shape[-1])
    hooks = {name: jnp.zeros(w.shape, F32) for name, w in big.items()}

    def local_forward(x_, mods_, hooks_, small_):
        return _forward(x_, ctx[0], mods_, big, hooks_, small_)

    y, pullback = jax.vjp(local_forward, x[0], mods, hooks, small)
    dy, sq = _loss_head(y, loss_target[0])
    loss = lax.psum(jnp.sum(sq) * (0.5 / d), ("x", "y", "c"))
    grad_x, d_mods, d_big, d_small = pullback(dy)

    d_od_in = jnp.concatenate([d_big["od_w_main"], d_big["od_w_gate"]], axis=1)
    blocked = [
        _col_blocked(d_big["ev_w_in"]),
        d_big["ev_w_out"].reshape(N_CHIP, -1, d_big["ev_w_out"].shape[-1]),
        _col_blocked(d_od_in),
        d_big["od_w_out"].reshape(N_CHIP, -1, d_big["od_w_out"].shape[-1]),
        jnp.concatenate([_col_blocked(d_big["ffn_w13_%d" % layer]) for layer in range(layers)], axis=1),
        jnp.concatenate([d_big["ffn_w2_%d" % layer].reshape(N_CHIP, -1, d_big["ffn_w2_%d" % layer].shape[-1])
                         for layer in range(layers)], axis=1),
    ]
    r_ev_in, r_ev_out, r_od_in, r_od_out, r_w13, r_w2 = _reduce_grads(blocked, ix, iy, ic)

    small_names = ["norm_w", "ev_lb", "ev_a_norm", "ev_pool_w", "ev_pool_scale", "od_conv", "od_A_log",
                   "od_dt_bias", "od_norm"]
    post_parts = [d_mods[:, 0], d_mods[:, 1]] + [d_small[n] for n in small_names]
    post = _all_gather8(_pack(post_parts), "gather_small_grads")
    post_sum = _sum_leading(post, "sum_small_grads")
    dm_l_all = _unpack(post, [post_parts[0].shape], lead=(N_DEV,))[0]
    summed = _unpack(post_sum, [p.shape for p in post_parts])
    dm_l_sum, dm_c_sum = summed[0], summed[1]
    g_small = dict(zip(small_names, summed[2:]))
    grad_b_ada = dm_l_sum + dm_c_sum

    def my_cols(full, width):
        return lax.dynamic_slice_in_dim(full, chip * width, width, axis=full.ndim - 1)

    dm_rows = jnp.concatenate([
        my_cols(dm_l_all, n_ada).transpose(1, 0, 2),
        my_cols(dm_c_sum, n_ada)[:, None, :],
        jnp.zeros((layers, 2 * SUBLANE - N_DEV - 1, n_ada), F32),
    ], axis=1)
    grad_w_ada, ga = _ada_bwd(a16, dm_rows, w_ada)
    ga_g = _all_gather8(jnp.pad(ga[N_DEV][None], ((0, SUBLANE - 1), (0, 0))), "gather_c_ctx_grad")
    g_silu_cc = _sum_leading(jnp.stack([ga_g[2 * j] for j in range(N_CHIP)]), "sum_c_ctx_grad")[0]
    grad_c_ctx = silu_cc_vjp(g_silu_cc)[0]

    grads = {
        "c_ctx": grad_c_ctx,
        "w_ada": grad_w_ada,
        "b_ada": grad_b_ada,
        "norm_w": my_cols(g_small["norm_w"], norm_w.shape[-1]),
        "ev_w_in": r_ev_in[None],
        "ev_lb": my_cols(g_small["ev_lb"], ev_lb.shape[-1]),
        "ev_a_norm": g_small["ev_a_norm"],
        "ev_pool_w": lax.dynamic_slice_in_dim(g_small["ev_pool_w"], chip * ev_pool_w.shape[2], ev_pool_w.shape[2], axis=2),
        "ev_pool_scale": g_small["ev_pool_scale"],
        "ev_w_out": r_ev_out[None],
        "od_w_in": r_od_in[None],
        "od_conv": my_cols(g_small["od_conv"], od_conv.shape[-1]),
        "od_A_log": g_small["od_A_log"],
        "od_dt_bias": g_small["od_dt_bias"],
        "od_norm": g_small["od_norm"],
        "od_w_out": r_od_out[None],
        "ffn_w13": r_w13.reshape(ffn_w13.shape),
        "ffn_w2": r_w2.reshape(ffn_w2.shape),
    }

    weights = dict(c_ctx=c_ctx, w_ada=w_ada, b_ada=b_ada, norm_w=norm_w, ev_w_in=ev_w_in, ev_lb=ev_lb,
                   ev_a_norm=ev_a_norm, ev_pool_w=ev_pool_w, ev_pool_scale=ev_pool_scale, ev_w_out=ev_w_out,
                   od_w_in=od_w_in, od_conv=od_conv, od_A_log=od_A_log, od_dt_bias=od_dt_bias, od_norm=od_norm,
                   od_w_out=od_w_out, ffn_w13=ffn_w13, ffn_w2=ffn_w2)
    first = dict(c_ctx=m_c_ctx, w_ada=m_w_ada, b_ada=m_b_ada, norm_w=m_norm_w, ev_w_in=m_ev_w_in, ev_lb=m_ev_lb,
                 ev_a_norm=m_ev_a_norm, ev_pool_w=m_ev_pool_w, ev_pool_scale=m_ev_pool_scale, ev_w_out=m_ev_w_out,
                 od_w_in=m_od_w_in, od_conv=m_od_conv, od_A_log=m_od_A_log, od_dt_bias=m_od_dt_bias,
                 od_norm=m_od_norm, od_w_out=m_od_w_out, ffn_w13=m_ffn_w13, ffn_w2=m_ffn_w2)
    second = dict(c_ctx=v_c_ctx, w_ada=v_w_ada, b_ada=v_b_ada, norm_w=v_norm_w, ev_w_in=v_ev_w_in, ev_lb=v_ev_lb,
                  ev_a_norm=v_ev_a_norm, ev_pool_w=v_ev_pool_w, ev_pool_scale=v_ev_pool_scale, ev_w_out=v_ev_w_out,
                  od_w_in=v_od_w_in, od_conv=v_od_conv, od_A_log=v_od_A_log, od_dt_bias=v_od_dt_bias,
                  od_norm=v_od_norm, od_w_out=v_od_w_out, ffn_w13=v_ffn_w13, ffn_w2=v_ffn_w2)
    names = list(weights)
    large = ["w_ada", "ev_w_in", "ev_w_out", "od_w_in", "od_w_out", "ffn_w13", "ffn_w2"]
    little = [n for n in names if n not in large]
    delta, new_m, new_v = {}, {}, {}
    for n in large:
        as2d = lambda t: t.reshape(-1, t.shape[-1])
        out = _adamw(as2d(weights[n]), as2d(grads[n]), as2d(first[n]), as2d(second[n]), "adamw_" + n)
        delta[n], new_m[n], new_v[n] = (t.reshape(weights[n].shape) for t in out)
    shapes = [weights[n].shape for n in little]
    out = _adamw(*(_pack([src[n] for n in little]) for src in (weights, grads, first, second)), "adamw_small")
    for res, packed in zip((delta, new_m, new_v), out):
        res.update(zip(little, _unpack(packed, shapes)))

    return (loss, grad_x[None], *[grads[n] for n in names], *[delta[n] for n in names],
            *[new_m[n] for n in names], *[new_v[n] for n in names])
```

```python
import functools

import numpy as np
import jax
import jax.numpy as jnp
from jax import lax
from jax.experimental import pallas as pl
from jax.experimental.pallas import tpu as pltpu

F32 = jnp.float32
BF16 = jnp.bfloat16
MESH = pl.DeviceIdType.MESH

EPS = 1e-6
GRID_W = 64
A_HEAD_DIM = 128
A_CHUNK = 32
POOL_WINDOWS = (2, 4, 8, 16)
C_HEAD_DIM = 128
C_CONV = 4
C_CHUNK = 64

ADAM_LR = 0.001
ADAM_B1 = 0.9
ADAM_B2 = 0.999
ADAM_EPS = 1e-08
ADAM_WD = 0.01
ADAM_STEP = 10

N_DEV = 8
N_CHIP = 4
LANE = 128
SUBLANE = 8
VMEM_LIMIT = 48 * 1024 * 1024

NN = (((1,), (0,)), ((), ()))
NT = (((1,), (1,)), ((), ()))
TN = (((0,), (0,)), ((), ()))


def _tile(n, cap, mult=LANE):
    best = 0
    for d in range(mult, min(n, cap) + 1, mult):
        if n % d == 0:
            best = d
    return best or n


def _div_le(n, cap):
    return max(d for d in range(1, cap + 1) if n % d == 0)


def _params(*sem):
    return pltpu.CompilerParams(dimension_semantics=sem or None, vmem_limit_bytes=VMEM_LIMIT)


def _dot(a, b, dims):
    return lax.dot_general(a, b, dims, preferred_element_type=F32)


def _matmul(a, b, mode, name):
    if mode == "nn":
        (m, k), n = a.shape, b.shape[1]
    elif mode == "nt":
        (m, k), n = a.shape, b.shape[0]
    else:
        (k, m), n = a.shape, b.shape[1]
    tm, tn, tk = _tile(m, 1024), _tile(n, 1024), _tile(k, 2048)
    nk = k // tk
    dims = {"nn": NN, "nt": NT, "tn": TN}[mode]
    if mode == "tn":
        a_spec = pl.BlockSpec((tk, tm), lambda i, j, kk: (kk, i))
    else:
        a_spec = pl.BlockSpec((tm, tk), lambda i, j, kk: (i, kk))
    if mode == "nt":
        b_spec = pl.BlockSpec((tn, tk), lambda i, j, kk: (j, kk))
    else:
        b_spec = pl.BlockSpec((tk, tn), lambda i, j, kk: (kk, j))

    def body(a_ref, b_ref, o_ref):
        part = _dot(a_ref[...], b_ref[...], dims)
        if nk == 1:
            o_ref[...] = part
        else:
            kk = pl.program_id(2)

            @pl.when(kk == 0)
            def _():
                o_ref[...] = part

            @pl.when(kk > 0)
            def _():
                o_ref[...] += part

    return pl.pallas_call(
        body,
        out_shape=jax.ShapeDtypeStruct((m, n), F32),
        grid=(m // tm, n // tn, nk),
        in_specs=[a_spec, b_spec],
        out_specs=pl.BlockSpec((tm, tn), lambda i, j, kk: (i, j)),
        compiler_params=_params("parallel", "parallel", "arbitrary"),
        name=name,
    )(a, b)


@jax.custom_vjp
def _dense(h, w, hook):
    return _matmul(h.astype(BF16), w, "nn", "dense_fwd")


def _dense_fwd(h, w, hook):
    hb = h.astype(BF16)
    return _matmul(hb, w, "nn", "dense_fwd"), (hb, w)


def _dense_bwd(res, dy):
    hb, w = res
    dyb = dy.astype(BF16)
    return _matmul(dyb, w, "nt", "dense_dx"), None, _matmul(hb, dyb, "tn", "dense_dw")


_dense.defvjp(_dense_fwd, _dense_bwd)


def _rec_plan(n, nc, anti, backward):
    nb = _div_le(int(np.gcd(n, nc)), 12)
    ng, ncb = n // nb, nc // nb
    if not anti:
        order = (lambda g: ng - 1 - g) if backward else (lambda g: g)
        return nb, ng, order, backward
    if backward:
        return nb, ng, (lambda g: jnp.where(g < ng - ncb, ncb + g, g - (ng - ncb))), False
    return nb, ng, (lambda g: jnp.where(g < ncb, ncb - 1 - g, ng - 1 - (g - ncb))), True


REC_HEADS = 4


def _rec_call(body, ins, outs, plan, state_shape, name):
    nb, ng, order, descending = plan
    hh = outs[0].shape[0]
    hb = _div_le(hh, REC_HEADS)

    def spec(shape):
        shared = hh // shape[0]
        return pl.BlockSpec((hb // shared, nb) + tuple(shape[2:]), lambda h, g: (h, order(g), 0, 0))

    return pl.pallas_call(
        functools.partial(body, hb, nb, descending),
        out_shape=outs,
        grid=(hh // hb, ng),
        in_specs=[spec(t.shape) for t in ins],
        out_specs=[spec(t.shape) for t in outs],
        scratch_shapes=[pltpu.VMEM((hb,) + tuple(state_shape), F32)],
        compiler_params=_params("parallel", "arbitrary"),
        name=name,
    )(*ins)


def _chunk_loop(nb, descending, step):
    def run(i, carry):
        step(nb - 1 - i if descending else i)
        return carry

    lax.fori_loop(0, nb, run, 0)


def _reset_at_start(state):
    @pl.when(pl.program_id(1) == 0)
    def _():
        state[...] = jnp.zeros_like(state)


def _bf(ref, h, c):
    return ref[h, c].astype(BF16)


def _within_chunk(scores, anti):
    row = lax.broadcasted_iota(jnp.int32, scores.shape, 0)
    col = lax.broadcasted_iota(jnp.int32, scores.shape, 1)
    return jnp.where(row <= col if anti else row >= col, scores, 0.0)


def _hgrn_rec_fwd(anti, nc, q_in, k_out, v, decay, qs, ks):
    hh, n, _, dk = q_in.shape
    dv = v.shape[-1]

    def body(hb, nb, descending, q_ref, k_ref, v_ref, d_ref, qs_ref, ks_ref, o_ref, s_ref, st):
        _reset_at_start(st)
        heads = range(hb)

        def step(c):
            s = [st[h] for h in heads]
            vb = [_bf(v_ref, h, c) for h in heads]
            scores = [_within_chunk(_dot(_bf(qs_ref, h, c), _bf(ks_ref, h, c), NT), anti) for h in heads]
            for h in heads:
                s_ref[h, c] = s[h]
                o_ref[h, c] = (_dot(_bf(q_ref, h, c), s[h].astype(BF16), NT)
                               + _dot(scores[h].astype(BF16), vb[h], NN))
            for h in heads:
                st[h] = s[h] * d_ref[h, c] + _dot(vb[h], _bf(k_ref, h, c), TN)

        _chunk_loop(nb, descending, step)

    ins = (q_in, k_out, v, decay, qs, ks)
    outs = (jax.ShapeDtypeStruct(v.shape, F32), jax.ShapeDtypeStruct((hh, n, dv, dk), F32))
    return _rec_call(body, ins, outs, _rec_plan(n, nc, anti, False), (dv, dk), "hgrn_rec_fwd")


def _hgrn_rec_bwd(anti, nc, q_in, k_out, v, decay, qs, ks, states, do):
    dk = q_in.shape[-1]
    n, dv = v.shape[1], v.shape[-1]

    def body(hb, nb, descending, q_ref, k_ref, v_ref, d_ref, qs_ref, ks_ref, s_ref, do_ref,
             dq_ref, dk_ref, dv_ref, dd_ref, dqs_ref, dks_ref, dst):
        _reset_at_start(dst)
        heads = range(hb)

        def step(c):
            ds = [dst[h] for h in heads]
            dsb = [x.astype(BF16) for x in ds]
            s = [s_ref[h, c] for h in heads]
            dob = [_bf(do_ref, h, c) for h in heads]
            vb = [_bf(v_ref, h, c) for h in heads]
            qsb = [_bf(qs_ref, h, c) for h in heads]
            ksb = [_bf(ks_ref, h, c) for h in heads]
            scores = [_within_chunk(_dot(qsb[h], ksb[h], NT), anti).astype(BF16) for h in heads]
            dscores = [_within_chunk(_dot(dob[h], vb[h], NT), anti).astype(BF16) for h in heads]
            for h in heads:
                dqs_ref[h, c] = _dot(dscores[h], ksb[h], NN)
                dks_ref[h, c] = _dot(dscores[h], qsb[h], TN)
                dq_ref[h, c] = _dot(dob[h], s[h].astype(BF16), NN)
                dk_ref[h, c] = _dot(vb[h], dsb[h], NN)
                dv_ref[h, c] = _dot(_bf(k_ref, h, c), dsb[h], NT) + _dot(scores[h], dob[h], TN)
                dd_ref[h, c] = jnp.sum(ds[h] * s[h], axis=0, keepdims=True)
            for h in heads:
                dst[h] = ds[h] * d_ref[h, c] + _dot(dob[h], _bf(q_ref, h, c), TN)

        _chunk_loop(nb, descending, step)

    ins = (q_in, k_out, v, decay, qs, ks, states, do)
    outs = tuple(jax.ShapeDtypeStruct(t.shape, F32) for t in (q_in, k_out, v, decay, qs, ks))
    return _rec_call(body, ins, outs, _rec_plan(n, nc, anti, True), (dv, dk), "hgrn_rec_bwd")


@functools.partial(jax.custom_vjp, nondiff_argnums=(0, 1))
def _hgrn_rec(anti, nc, q_in, k_out, v, decay, qs, ks):
    return _hgrn_rec_fwd(anti, nc, q_in, k_out, v, decay, qs, ks)[0]


def _hgrn_rec_vjp_fwd(anti, nc, q_in, k_out, v, decay, qs, ks):
    o, states = _hgrn_rec_fwd(anti, nc, q_in, k_out, v, decay, qs, ks)
    return o, (q_in, k_out, v, decay, qs, ks, states)


def _hgrn_rec_vjp_bwd(anti, nc, res, do):
    return _hgrn_rec_bwd(anti, nc, *res, do)


_hgrn_rec.defvjp(_hgrn_rec_vjp_fwd, _hgrn_rec_vjp_bwd)


def _gdn_rec_fwd(anti, nc, u, w, qk, q, k, scal, gl):
    hh, n, _, dk = w.shape
    dv = u.shape[-1]
    rep = hh // q.shape[0]

    def body(hb, nb, descending, u_ref, w_ref, qk_ref, q_ref, k_ref, sc_ref, gl_ref, o_ref, s_ref, vn_ref, st):
        _reset_at_start(st)
        heads = range(hb)

        def step(c):
            s = [st[h] for h in heads]
            sb = [x.astype(BF16) for x in s]
            sc = [_scalars(sc_ref, h, c, dk, (EG, EGR)) for h in heads]
            qd = [(q_ref[h // rep, c] * sc[h][0]).astype(BF16) for h in heads]
            kd = [(k_ref[h // rep, c] * sc[h][1]).astype(BF16) for h in heads]
            vn = [u_ref[h, c] - _dot(_bf(w_ref, h, c), sb[h], NT) for h in heads]
            vnb = [x.astype(BF16) for x in vn]
            for h in heads:
                s_ref[h, c] = s[h]
                vn_ref[h, c] = vn[h]
                o_ref[h, c] = _dot(qd[h], sb[h], NT) + _dot(_bf(qk_ref, h, c), vnb[h], NN)
            for h in heads:
                st[h] = s[h] * gl_ref[h, c] + _dot(vnb[h], kd[h], TN)

        _chunk_loop(nb, descending, step)

    ins = (u, w, qk, q, k, scal, gl)
    outs = (
        jax.ShapeDtypeStruct(u.shape, F32),
        jax.ShapeDtypeStruct((hh, n, dv, dk), F32),
        jax.ShapeDtypeStruct(u.shape, F32),
    )
    return _rec_call(body, ins, outs, _rec_plan(n, nc, anti, False), (dv, dk), "gdn_rec_fwd")


def _gdn_rec_bwd(anti, nc, w, qk, q, k, scal, gl, states, vnew, do):
    hh, n, _, dk = w.shape
    dv = vnew.shape[-1]
    rep = hh // q.shape[0]

    def body(hb, nb, descending, w_ref, qk_ref, q_ref, k_ref, sc_ref, gl_ref, s_ref, vn_ref, do_ref,
             du_ref, dw_ref, dqk_ref, dq_ref, dk_ref, dsc_ref, dgl_ref, dst):
        _reset_at_start(dst)
        heads = range(hb)

        def step(c):
            ds = [dst[h] for h in heads]
            dsb = [x.astype(BF16) for x in ds]
            s = [s_ref[h, c] for h in heads]
            sb = [x.astype(BF16) for x in s]
            sc = [_scalars(sc_ref, h, c, dk, (EG, EGR)) for h in heads]
            qf = [q_ref[h // rep, c] for h in heads]
            kf = [k_ref[h // rep, c] for h in heads]
            qd = [(qf[h] * sc[h][0]).astype(BF16) for h in heads]
            kd = [(kf[h] * sc[h][1]).astype(BF16) for h in heads]
            vnb = [_bf(vn_ref, h, c) for h in heads]
            dob = [_bf(do_ref, h, c) for h in heads]
            dvn = [_dot(_bf(qk_ref, h, c), dob[h], TN) + _dot(kd[h], dsb[h], NT) for h in heads]
            dvnb = [x.astype(BF16) for x in dvn]
            dqd = [_dot(dob[h], sb[h], NN) for h in heads]
            dkd = [_dot(vnb[h], dsb[h], NN) for h in heads]
            for h in heads:
                du_ref[h, c] = dvn[h]
                dw_ref[h, c] = -_dot(dvnb[h], sb[h], NN)
                dqk_ref[h, c] = _dot(dob[h], vnb[h], NT)
                dq_ref[h, c] = dqd[h] * sc[h][0]
                dk_ref[h, c] = dkd[h] * sc[h][1]
                dsc_ref[h, c] = _scalar_rows([None, _token_sums(dqd[h] * qf[h]), _token_sums(dkd[h] * kf[h])])
                dgl_ref[h, c] = jnp.sum(ds[h] * s[h], axis=0, keepdims=True)
            for h in heads:
                dst[h] = ds[h] * gl_ref[h, c] + _dot(dob[h], qd[h], TN) - _dot(dvnb[h], _bf(w_ref, h, c), TN)

        _chunk_loop(nb, descending, step)

    ins = (w, qk, q, k, scal, gl, states, vnew, do)
    outs = (jax.ShapeDtypeStruct(vnew.shape, F32), jax.ShapeDtypeStruct(w.shape, F32),
            jax.ShapeDtypeStruct(qk.shape, F32), jax.ShapeDtypeStruct((hh,) + q.shape[1:], F32),
            jax.ShapeDtypeStruct((hh,) + k.shape[1:], F32), jax.ShapeDtypeStruct(scal.shape, F32),
            jax.ShapeDtypeStruct(gl.shape, F32))
    return _rec_call(body, ins, outs, _rec_plan(n, nc, anti, True), (dv, dk), "gdn_rec_bwd")


@functools.partial(jax.custom_vjp, nondiff_argnums=(0, 1))
def _gdn_rec(anti, nc, u, w, qk, q, k, scal, gl):
    return _gdn_rec_fwd(anti, nc, u, w, qk, q, k, scal, gl)[0]


def _gdn_rec_vjp_fwd(anti, nc, u, w, qk, q, k, scal, gl):
    o, states, vnew = _gdn_rec_fwd(anti, nc, u, w, qk, q, k, scal, gl)
    return o, (w, qk, q, k, scal, gl, states, vnew)


def _gdn_rec_vjp_bwd(anti, nc, res, do):
    du, dw, dqk, dq, dk, dsc, dgl = _gdn_rec_bwd(anti, nc, *res, do)
    return du, dw, dqk, _add_shared_heads(dq, res[2].shape[0]), _add_shared_heads(dk, res[3].shape[0]), dsc, dgl


_gdn_rec.defvjp(_gdn_rec_vjp_fwd, _gdn_rec_vjp_bwd)


def _split(a):
    hi = a.astype(BF16)
    return hi, (a - hi.astype(F32)).astype(BF16)


def _dot3(a, b, dims=NN):
    ah, al = _split(a)
    bh, bl = _split(b)
    return _dot(ah, bh, dims) + (_dot(ah, bl, dims) + _dot(al, bh, dims))


INV_BASE = 8


def _inv_unit_triangular(mats):
    size = mats[0].shape[0]
    row = lax.broadcasted_iota(jnp.int32, (size, size), 0)
    col = lax.broadcasted_iota(jnp.int32, (size, size), 1)
    eye = jnp.where(row == col, 1.0, 0.0).astype(F32)

    def same_block(width):
        shift = width.bit_length() - 1
        return jnp.right_shift(row, shift) == jnp.right_shift(col, shift)

    base = [jnp.where(same_block(INV_BASE), m, 0.0) for m in mats]
    sq = [_dot3(b, b) for b in base]
    inv = [_dot3(eye - b, eye + s) for b, s in zip(base, sq)]
    width = 4
    while width < INV_BASE:
        sq = [_dot3(s, s) for s in sq]
        inv = [_dot3(i, eye + s) for i, s in zip(inv, sq)]
        width *= 2
    while width < size:
        outer = same_block(2 * width) & jnp.logical_not(same_block(width))
        part = [_dot3(i, jnp.where(outer, m, 0.0)) for i, m in zip(inv, mats)]
        inv = [i - _dot3(p, i) for i, p in zip(inv, part)]
        width *= 2
    return inv


PREP_CHUNKS = 6


def _prep_call(body, ins, outs, name):
    heads = outs[0].shape[0]
    nb = _div_le(ins[0].shape[1], PREP_CHUNKS)

    def spec(shape):
        rep = heads // shape[0]
        return pl.BlockSpec((1, nb) + tuple(shape[2:]), lambda h, g: (h // rep, g, 0, 0))

    return pl.pallas_call(
        functools.partial(body, nb),
        out_shape=outs,
        grid=(heads, ins[0].shape[1] // nb),
        in_specs=[spec(t.shape) for t in ins],
        out_specs=[spec(t.shape) for t in outs],
        compiler_params=_params("parallel", "parallel"),
        name=name,
    )(*ins)


def _lane_broadcast(row, width):
    size = row.shape[1]
    r = lax.broadcasted_iota(jnp.int32, (size, size), 0)
    c = lax.broadcasted_iota(jnp.int32, (size, size), 1)
    hi, lo = _split(jnp.where(r == c, jnp.broadcast_to(row, (size, size)), 0.0))
    ones = jnp.ones((size, width), BF16)
    return _dot(hi, ones, NN) + _dot(lo, ones, NN)


def _token_sums(p):
    hi, lo = _split(p)
    ones = jnp.ones((SUBLANE, p.shape[1]), BF16)
    return _dot(ones, hi, NT) + _dot(ones, lo, NT)


def _scalar_rows(rows):
    shape = next(r.shape for r in rows if r is not None)
    index = lax.broadcasted_iota(jnp.int32, shape, 0)
    out = jnp.zeros(shape, F32)
    for j, r in enumerate(rows):
        if r is not None:
            out = jnp.where(index == j, r, out)
    return out


BETA, EG, EGR = 0, 1, 2


def _scalars(sc_ref, h, c, width, which):
    rows = sc_ref[h, c]
    return [_lane_broadcast(rows[j:j + 1], width) for j in which]


def _off_diagonal(x):
    row = lax.broadcasted_iota(jnp.int32, x.shape, 0)
    col = lax.broadcasted_iota(jnp.int32, x.shape, 1)
    return jnp.where(row == col, 0.0, x)


def _gdn_prep_fwd(q, k, v, decay, scal):
    dk, dv = k.shape[-1], v.shape[-1]

    def body(nb, q_ref, k_ref, v_ref, dec_ref, sc_ref, u_ref, w_ref, qk_ref, t_ref):
        chunks = range(nb)
        sc = [_scalars(sc_ref, 0, c, dk, (BETA, EG)) for c in chunks]
        kf = [k_ref[0, c] for c in chunks]
        kc = [x.astype(BF16) for x in kf]
        kb = [kf[c] * sc[c][0] for c in chunks]
        dec = [dec_ref[0, c] for c in chunks]
        a = [_off_diagonal(_dot(kb[c].astype(BF16), kc[c], NT) * dec[c]) for c in chunks]
        t = _inv_unit_triangular(a)
        for c in chunks:
            tb = t[c].astype(BF16)
            t_ref[0, c] = t[c]
            u_ref[0, c] = _dot(tb, (v_ref[0, c] * sc[c][0][:, :dv]).astype(BF16), NN)
            w_ref[0, c] = _dot(tb, (kb[c] * sc[c][1]).astype(BF16), NN)
            qk_ref[0, c] = _dot(_bf(q_ref, 0, c), kc[c], NT) * dec[c]

    ins = (q, k, v, decay, scal)
    outs = (jax.ShapeDtypeStruct(v.shape, F32), jax.ShapeDtypeStruct(v.shape[:3] + (dk,), F32),
            jax.ShapeDtypeStruct(decay.shape, F32), jax.ShapeDtypeStruct(decay.shape, F32))
    return _prep_call(body, ins, outs, "gdn_prep_fwd")


def _gdn_prep_bwd(q, k, v, decay, scal, t, du, dw, dqk):
    dk = k.shape[-1]

    def body(nb, q_ref, k_ref, v_ref, dec_ref, sc_ref, t_ref, du_ref, dw_ref, dqk_ref,
             dq_ref, dk_ref, dv_ref, ddec_ref, dsc_ref):
        chunks = range(nb)
        sc = [_scalars(sc_ref, 0, c, dk, (BETA, EG)) for c in chunks]
        beta = [s[0] for s in sc]
        eg = [s[1] for s in sc]
        kf = [k_ref[0, c] for c in chunks]
        vf = [v_ref[0, c] for c in chunks]
        kc = [x.astype(BF16) for x in kf]
        qc = [_bf(q_ref, 0, c) for c in chunks]
        kb = [kf[c] * beta[c] for c in chunks]
        kbb = [x.astype(BF16) for x in kb]
        tb = [_bf(t_ref, 0, c) for c in chunks]
        dub = [_bf(du_ref, 0, c) for c in chunks]
        dwb = [_bf(dw_ref, 0, c) for c in chunks]
        dvb = [_dot(tb[c], dub[c], TN) for c in chunks]
        dkbg = [_dot(tb[c], dwb[c], TN) for c in chunks]
        dt = [_dot(dub[c], (vf[c] * beta[c]).astype(BF16), NT) + _dot(dwb[c], (kb[c] * eg[c]).astype(BF16), NT)
              for c in chunks]
        left = [_dot(tb[c], dt[c].astype(BF16), TN).astype(BF16) for c in chunks]
        da = [-_off_diagonal(_dot(left[c], tb[c], NT)) for c in chunks]
        dqk = [dqk_ref[0, c] for c in chunks]
        for c in chunks:
            ddec_ref[0, c] = da[c] * _dot(kbb[c], kc[c], NT) + dqk[c] * _dot(qc[c], kc[c], NT)
        dkk = [(da[c] * dec_ref[0, c]).astype(BF16) for c in chunks]
        dqkd = [(dqk[c] * dec_ref[0, c]).astype(BF16) for c in chunks]
        dkb = [_dot(dkk[c], kc[c], NN) + dkbg[c] * eg[c] for c in chunks]
        for c in chunks:
            dq_ref[0, c] = _dot(dqkd[c], kc[c], NN)
            dk_ref[0, c] = _dot(dkk[c], kbb[c], TN) + _dot(dqkd[c], qc[c], TN) + dkb[c] * beta[c]
            dv_ref[0, c] = dvb[c] * beta[c]
            dsc_ref[0, c] = _scalar_rows([_token_sums(dkb[c] * kf[c] + dvb[c] * vf[c]),
                                          _token_sums(dkbg[c] * kb[c])])

    ins = (q, k, v, decay, scal, t, du, dw, dqk)
    heads = v.shape[0]
    outs = (jax.ShapeDtypeStruct((heads,) + q.shape[1:], F32), jax.ShapeDtypeStruct((heads,) + k.shape[1:], F32),
            jax.ShapeDtypeStruct(v.shape, F32), jax.ShapeDtypeStruct(decay.shape, F32),
            jax.ShapeDtypeStruct(scal.shape, F32))
    return _prep_call(body, ins, outs, "gdn_prep_bwd")


def _add_shared_heads(d, key_heads):
    return d.reshape((key_heads, d.shape[0] // key_heads) + d.shape[1:]).sum(axis=1)


@jax.custom_vjp
def _gdn_prep(q, k, v, decay, scal):
    return _gdn_prep_fwd(q, k, v, decay, scal)[:3]


def _gdn_prep_vjp_fwd(q, k, v, decay, scal):
    u, w, qk, t = _gdn_prep_fwd(q, k, v, decay, scal)
    return (u, w, qk), (q, k, v, decay, scal, t)


def _gdn_prep_vjp_bwd(res, cot):
    dq, dk, dv, ddec, dsc = _gdn_prep_bwd(*res, *cot)
    return _add_shared_heads(dq, res[0].shape[0]), _add_shared_heads(dk, res[1].shape[0]), dv, ddec, dsc


_gdn_prep.defvjp(_gdn_prep_vjp_fwd, _gdn_prep_vjp_bwd)


def _loss_head(y, target):
    rows, d = y.shape
    tr = _tile(rows, 256, SUBLANE)

    def body(y_ref, t_ref, dy_ref, sq_ref):
        diff = y_ref[...] - t_ref[...]
        dy_ref[...] = diff * (1.0 / d)
        part = jnp.sum((diff * diff).reshape(tr // SUBLANE, SUBLANE, d), axis=0)

        @pl.when(pl.program_id(0) == 0)
        def _():
            sq_ref[...] = part

        @pl.when(pl.program_id(0) > 0)
        def _():
            sq_ref[...] += part

    row = pl.BlockSpec((tr, d), lambda i: (i, 0))
    return pl.pallas_call(
        body,
        out_shape=(jax.ShapeDtypeStruct((rows, d), F32), jax.ShapeDtypeStruct((SUBLANE, d), F32)),
        grid=(rows // tr,),
        in_specs=[row, row],
        out_specs=[row, pl.BlockSpec((SUBLANE, d), lambda i: (0, 0))],
        compiler_params=_params("arbitrary"),
        name="loss_head",
    )(y, target)


def _add_half(g, r1, my_c):
    nblk, half, cols = r1.shape
    tr = _tile(half, 128, 2 * SUBLANE)
    per_half = half // tr

    def body(c_ref, g_ref, r_ref, o_ref):
        o_ref[...] = (g_ref[...] + r_ref[...]).astype(BF16)

    grid_spec = pltpu.PrefetchScalarGridSpec(
        num_scalar_prefetch=1,
        grid=(nblk, per_half),
        in_specs=[
            pl.BlockSpec((1, tr, cols), lambda b, i, c_ref: (b, c_ref[0] * per_half + i, 0)),
            pl.BlockSpec((1, tr, cols), lambda b, i, c_ref: (b, i, 0)),
        ],
        out_specs=pl.BlockSpec((1, tr, cols), lambda b, i, c_ref: (b, i, 0)),
    )
    return pl.pallas_call(
        body,
        out_shape=jax.ShapeDtypeStruct(r1.shape, BF16),
        grid_spec=grid_spec,
        compiler_params=_params("parallel", "parallel"),
        name="grad_add_half",
    )(my_c.reshape(1).astype(jnp.int32), g, r1)


def _sum_chips(own, slots, place):
    _, half, cols = own.shape
    tr = _tile(half, 128, 2 * SUBLANE)
    per_half = half // tr

    def body(p_ref, a_ref, b_ref, c_ref, d_ref, o_ref):
        f32 = lambda ref: ref[0].astype(F32)
        o_ref[...] = (f32(a_ref) + f32(b_ref)) + (f32(c_ref) + f32(d_ref))

    def block(k):
        return pl.BlockSpec((1, tr, cols), lambda i, p_ref: (p_ref[k], i, 0))

    grid_spec = pltpu.PrefetchScalarGridSpec(
        num_scalar_prefetch=1,
        grid=(per_half,),
        in_specs=[block(0), block(1), block(2), block(3)],
        out_specs=pl.BlockSpec((tr, cols), lambda i, p_ref: (p_ref[4] * per_half + i, 0)),
    )
    return pl.pallas_call(
        body,
        out_shape=jax.ShapeDtypeStruct((2 * half, cols), F32),
        grid_spec=grid_spec,
        compiler_params=_params("parallel"),
        name="grad_sum_chips",
    )(place, own, slots, slots, slots)


def _cast_into_block(w, chip):
    rows, cols = w.shape
    tr = _tile(rows, 128, 2 * SUBLANE)

    def body(j_ref, w_ref, o_ref):
        o_ref[0] = w_ref[...].astype(BF16)

    grid_spec = pltpu.PrefetchScalarGridSpec(
        num_scalar_prefetch=1,
        grid=(rows // tr,),
        in_specs=[pl.BlockSpec((tr, cols), lambda i, j_ref: (i, 0))],
        out_specs=pl.BlockSpec((1, tr, cols), lambda i, j_ref: (j_ref[0], i, 0)),
    )
    return pl.pallas_call(
        body,
        out_shape=jax.ShapeDtypeStruct((N_CHIP, rows, cols), BF16),
        grid_spec=grid_spec,
        compiler_params=_params("parallel"),
        name="cast_weight_shard",
    )(chip.reshape(1).astype(jnp.int32), w)


def _sum_leading(x, name):
    k, rows, cols = x.shape
    tr = _tile(rows, 128, SUBLANE)

    def body(x_ref, o_ref):
        parts = [x_ref[i] for i in range(k)]
        while len(parts) > 1:
            parts = [parts[i] + parts[i + 1] for i in range(0, len(parts), 2)]
        o_ref[...] = parts[0]

    return pl.pallas_call(
        body,
        out_shape=jax.ShapeDtypeStruct((rows, cols), F32),
        grid=(rows // tr,),
        in_specs=[pl.BlockSpec((k, tr, cols), lambda i: (0, i, 0))],
        out_specs=pl.BlockSpec((tr, cols), lambda i: (i, 0)),
        compiler_params=_params("parallel"),
        name=name,
    )(x)


def _adamw(w, g, m, v, name):
    rows, cols = w.shape
    tr = _tile(rows, 128, SUBLANE)
    m_scale = 1.0 / (1.0 - ADAM_B1 ** ADAM_STEP)
    v_scale = 1.0 / (1.0 - ADAM_B2 ** ADAM_STEP)

    def body(w_ref, g_ref, m_ref, v_ref, d_ref, nm_ref, nv_ref):
        gg = g_ref[...]
        nm = ADAM_B1 * m_ref[...] + (1.0 - ADAM_B1) * gg
        nv = ADAM_B2 * v_ref[...] + (1.0 - ADAM_B2) * (gg * gg)
        nm_ref[...] = nm
        nv_ref[...] = nv
        d_ref[...] = -ADAM_LR * ((nm * m_scale) / (jnp.sqrt(nv * v_scale) + ADAM_EPS) + ADAM_WD * w_ref[...])

    spec = pl.BlockSpec((tr, cols), lambda i: (i, 0))
    out = jax.ShapeDtypeStruct((rows, cols), F32)
    return pl.pallas_call(
        body,
        out_shape=(out, out, out),
        grid=(rows // tr,),
        in_specs=[spec] * 4,
        out_specs=[spec] * 3,
        compiler_params=_params("parallel"),
        name=name,
    )(w, g, m, v)


def _ada_fwd(a16, w):
    layers, d, n = w.shape
    tn = _tile(n, 512)

    def body(a_ref, w_ref, o_ref):
        o_ref[0] = _dot(a_ref[...].astype(BF16), w_ref[0].astype(BF16), NN)

    return pl.pallas_call(
        body,
        out_shape=jax.ShapeDtypeStruct((layers, a16.shape[0], n), F32),
        grid=(layers, n // tn),
        in_specs=[pl.BlockSpec(a16.shape, lambda l, j: (0, 0)), pl.BlockSpec((1, d, tn), lambda l, j: (l, 0, j))],
        out_specs=pl.BlockSpec((1, a16.shape[0], tn), lambda l, j: (l, 0, j)),
        compiler_params=_params("parallel", "parallel"),
        name="ada_fwd",
    )(a16, w)


def _ada_bwd(a16, dm, w):
    layers, d, n = w.shape
    tn = _tile(n, 512)
    rows = a16.shape[0]

    def body(a_ref, dm_ref, w_ref, gw_ref, ga_ref):
        dmb = dm_ref[0].astype(BF16)
        gw_ref[0] = _dot(a_ref[...].astype(BF16), dmb, TN)
        part = _dot(dmb, w_ref[0].astype(BF16), NT)
        first = (pl.program_id(0) == 0) & (pl.program_id(1) == 0)

        @pl.when(first)
        def _():
            ga_ref[...] = part

        @pl.when(jnp.logical_not(first))
        def _():
            ga_ref[...] += part

    return pl.pallas_call(
        body,
        out_shape=(jax.ShapeDtypeStruct(w.shape, F32), jax.ShapeDtypeStruct((rows, d), F32)),
        grid=(layers, n // tn),
        in_specs=[
            pl.BlockSpec((rows, d), lambda l, j: (0, 0)),
            pl.BlockSpec((1, rows, tn), lambda l, j: (l, 0, j)),
            pl.BlockSpec((1, d, tn), lambda l, j: (l, 0, j)),
        ],
        out_specs=[pl.BlockSpec((1, d, tn), lambda l, j: (l, 0, j)), pl.BlockSpec((rows, d), lambda l, j: (0, 0))],
        compiler_params=_params("arbitrary", "arbitrary"),
        name="ada_bwd",
    )(a16, dm, w)


def _place():
    ix, iy, ic = lax.axis_index("x"), lax.axis_index("y"), lax.axis_index("c")
    return ix, iy, ic


def _flip(coord, bit):
    return 1 - coord if bit else coord


def _remote(src, dst, send_sem, recv_sem, to):
    return pltpu.make_async_remote_copy(
        src_ref=src, dst_ref=dst, send_sem=send_sem, recv_sem=recv_sem, device_id=to, device_id_type=MESH)


def _all_gather8(x, name):
    rows, cols = x.shape

    def body(x_ref, o_ref, send_sems, recv_sems):
        ix, iy, ic = _place()
        me = 4 * ix + 2 * iy + ic
        o_ref[me] = x_ref[...]
        peers = []
        for p in range(1, N_DEV):
            to = (_flip(ix, p & 4), _flip(iy, p & 2), _flip(ic, p & 1))
            peers.append((to, 4 * to[0] + 2 * to[1] + to[2]))
        sends = [_remote(x_ref, o_ref.at[me], send_sems.at[p], recv_sems.at[p], to)
                 for p, (to, _) in enumerate(peers)]
        for cp in sends:
            cp.start()
        for p, (to, slot) in enumerate(peers):
            _remote(x_ref, o_ref.at[slot], send_sems.at[p], recv_sems.at[p], to).wait_recv()
        for cp in sends:
            cp.wait_send()

    return pl.pallas_call(
        body,
        out_shape=jax.ShapeDtypeStruct((N_DEV, rows, cols), F32),
        in_specs=[pl.BlockSpec(memory_space=pltpu.VMEM)],
        out_specs=pl.BlockSpec(memory_space=pltpu.VMEM),
        scratch_shapes=[pltpu.SemaphoreType.DMA((N_DEV - 1,)), pltpu.SemaphoreType.DMA((N_DEV - 1,))],
        compiler_params=pltpu.CompilerParams(vmem_limit_bytes=VMEM_LIMIT),
        name=name,
    )(x)


def _other_chips(ix, iy):
    chips = [(1 - ix, iy), (ix, 1 - iy), (1 - ix, 1 - iy)]
    return [(cx, cy, 2 * cx + cy) for cx, cy in chips]


def _hbm_call(body, ins, out_shapes, scratch, name, in_place=False):
    hbm = pl.BlockSpec(memory_space=pltpu.HBM)
    return pl.pallas_call(
        body,
        out_shape=out_shapes,
        in_specs=[hbm] * len(ins),
        out_specs=[hbm] * len(out_shapes),
        scratch_shapes=scratch,
        input_output_aliases={t: t for t in range(len(ins))} if in_place else {},
        compiler_params=pltpu.CompilerParams(vmem_limit_bytes=VMEM_LIMIT),
        name=name,
    )(*ins)


def _gather_weights(blocks):
    nt = len(blocks)

    def body(*refs):
        ins, outs = refs[:nt], refs[nt:2 * nt]
        ici_send, ici_recv, d2d_send, d2d_recv = refs[2 * nt:]
        ix, iy, ic = _place()
        j = 2 * ix + iy
        sibling = (ix, iy, 1 - ic)
        chips = _other_chips(ix, iy)
        sends, waits = [], []
        for t in range(nt):
            half = ins[t].shape[1] // 2
            mine = pl.ds(ic * half, half)
            for q, (cx, cy, _) in enumerate(chips):
                cp = _remote(ins[t].at[j, mine], outs[t].at[j, mine], ici_send.at[3 * t + q], ici_recv.at[3 * t + q],
                             (cx, cy, ic))
                cp.start()
                sends.append(cp)
        for q, (cx, cy, jq) in enumerate(chips):
            for t in range(nt):
                half = ins[t].shape[1] // 2
                mine = pl.ds(ic * half, half)
                theirs = pl.ds((1 - ic) * half, half)
                k = 3 * t + q
                _remote(ins[t].at[j, mine], outs[t].at[jq, mine], ici_send.at[k], ici_recv.at[k],
                        (cx, cy, ic)).wait_recv()
                cp = _remote(outs[t].at[jq, mine], outs[t].at[jq, mine], d2d_send.at[k], d2d_recv.at[k], sibling)
                cp.start()
                sends.append(cp)
                waits.append(_remote(outs[t].at[jq, theirs], outs[t].at[jq, theirs], d2d_send.at[k], d2d_recv.at[k],
                                     sibling))
        for cp in waits:
            cp.wait_recv()
        for cp in sends:
            cp.wait_send()

    dma = pltpu.SemaphoreType.DMA
    outs = tuple(jax.ShapeDtypeStruct(b.shape, b.dtype) for b in blocks)
    return _hbm_call(body, blocks, outs, [dma((3 * nt,))] * 4, "gather_weights", in_place=True)


def _exchange_halves(grads):
    nt = len(grads)

    def body(*refs):
        ins, outs = refs[:nt], refs[nt:2 * nt]
        send_sems, recv_sems = refs[2 * nt:]
        ix, iy, ic = _place()
        sibling = (ix, iy, 1 - ic)
        copies = []
        for t in range(nt):
            half = ins[t].shape[1] // 2
            theirs = pl.ds((1 - ic) * half, half)
            cp = _remote(ins[t].at[:, theirs, :], outs[t], send_sems.at[t], recv_sems.at[t], sibling)
            cp.start()
            copies.append(cp)
        for cp in copies:
            cp.wait_recv()
        for cp in copies:
            cp.wait_send()

    dma = pltpu.SemaphoreType.DMA
    outs = tuple(jax.ShapeDtypeStruct((g.shape[0], g.shape[1] // 2, g.shape[2]), F32) for g in grads)
    return _hbm_call(body, grads, outs, [dma((nt,)), dma((nt,))], "grad_exchange_halves")


def _exchange_blocks(parts):
    nt = len(parts)

    def body(*refs):
        ins, outs = refs[:nt], refs[nt:2 * nt]
        send_sems, recv_sems = refs[2 * nt:]
        ix, iy, ic = _place()
        j = 2 * ix + iy
        chips = _other_chips(ix, iy)
        sends, waits = [], []
        for t in range(nt):
            for q, (cx, cy, jq) in enumerate(chips):
                k = 3 * t + q
                cp = _remote(ins[t].at[jq], outs[t].at[j], send_sems.at[k], recv_sems.at[k], (cx, cy, ic))
                cp.start()
                sends.append(cp)
                waits.append(_remote(ins[t].at[jq], outs[t].at[jq], send_sems.at[k], recv_sems.at[k], (cx, cy, ic)))
        for cp in waits:
            cp.wait_recv()
        for cp in sends:
            cp.wait_send()

    dma = pltpu.SemaphoreType.DMA
    outs = tuple(jax.ShapeDtypeStruct(p.shape, p.dtype) for p in parts)
    return _hbm_call(body, parts, outs, [dma((3 * nt,)), dma((3 * nt,))], "grad_exchange_blocks")


def _share_halves(blocks):
    nt = len(blocks)

    def body(*refs):
        ins, outs = refs[:nt], refs[nt:2 * nt]
        send_sems, recv_sems = refs[2 * nt:]
        ix, iy, ic = _place()
        sibling = (ix, iy, 1 - ic)
        sends, waits = [], []
        for t in range(nt):
            half = ins[t].shape[0] // 2
            mine = pl.ds(ic * half, half)
            theirs = pl.ds((1 - ic) * half, half)
            cp = _remote(ins[t].at[mine], outs[t].at[mine], send_sems.at[t], recv_sems.at[t], sibling)
            cp.start()
            sends.append(cp)
            waits.append(_remote(ins[t].at[theirs], outs[t].at[theirs], send_sems.at[t], recv_sems.at[t], sibling))
        for cp in waits:
            cp.wait_recv()
        for cp in sends:
            cp.wait_send()

    dma = pltpu.SemaphoreType.DMA
    outs = tuple(jax.ShapeDtypeStruct(b.shape, F32) for b in blocks)
    return _hbm_call(body, blocks, outs, [dma((nt,)), dma((nt,))], "grad_share_halves", in_place=True)


def _reduce_grads(grads, ix, iy, ic):
    recv = _exchange_halves(grads)
    chip_sums = [_add_half(g, r1, ic) for g, r1 in zip(grads, recv)]
    slots = _exchange_blocks(chip_sums)
    place = jnp.stack([2 * ix + iy] + [jq for _, _, jq in _other_chips(ix, iy)] + [ic]).astype(jnp.int32)
    return _share_halves([_sum_chips(own, got, place) for own, got in zip(chip_sums, slots)])


def _rmsnorm(x, w):
    return x * lax.rsqrt(jnp.mean(x * x, axis=-1, keepdims=True) + EPS) * w


def _l2norm(x):
    return x * lax.rsqrt(jnp.sum(x * x, axis=-1, keepdims=True) + EPS)


def _heads(t, d):
    return t.reshape(t.shape[:-1] + (t.shape[-1] // d, d))


def _chunks(t, chunk):
    n = t.shape[0] // chunk
    return t.reshape(n, chunk, t.shape[1], t.shape[2]).transpose(2, 0, 1, 3)


def _unchunks(t):
    hh, n, chunk, d = t.shape
    return t.transpose(1, 2, 0, 3).reshape(n * chunk, hh, d)


def _triangle(size, anti):
    ones = jnp.ones((size, size), bool)
    return jnp.triu(ones) if anti else jnp.tril(ones)


def _hgrn2_scan(q, k, v, log_f, n_ctx, anti):
    q, k, v, log_f = (_chunks(t, A_CHUNK) for t in (q, k, v, log_f))
    b = lax.cumsum(log_f, axis=2, reverse=anti)
    end = 0 if anti else A_CHUNK - 1
    mid = A_CHUNK // 2 if anti else A_CHUNK // 2 - 1
    b_last = b[:, :, end:end + 1, :]
    b_mid = b[:, :, mid:mid + 1, :]
    return _unchunks(_hgrn_rec(anti, n_ctx // A_CHUNK, q * jnp.exp(b), k * jnp.exp(b_last - b), v, jnp.exp(b_last),
                               q * jnp.exp(b - b_mid), k * jnp.exp(b_mid - b)))


def _gdn_scan(q, k, v, g, beta, n_ctx, anti):
    g = _chunks(g[..., None], C_CHUNK)[..., 0]
    beta = _chunks(beta[..., None], C_CHUNK)[..., 0]
    gc = lax.cumsum(g, axis=2, reverse=anti)
    end = 0 if anti else C_CHUNK - 1
    causal = _triangle(C_CHUNK, anti)
    diff = gc[..., :, None] - gc[..., None, :]
    decay = jnp.where(causal, jnp.exp(jnp.where(causal, diff, 0.0)), 0.0)
    rows = [beta, jnp.exp(gc), jnp.exp(gc[..., end:end + 1] - gc)]
    scal = jnp.stack(rows + [jnp.zeros_like(beta)] * (SUBLANE - len(rows)), axis=2)
    u, w, qk = _gdn_prep(q, k, v, decay, scal)
    g_last = jnp.exp(gc[..., end])
    gl = jnp.broadcast_to(g_last[..., None, None], g_last.shape + (1, q.shape[-1]))
    return _unchunks(_gdn_rec(anti, n_ctx // C_CHUNK, u, w, qk, q, k, scal, gl))


def _bidirectional(scan_fn, fwd_args, bwd_args, n_ctx):
    return scan_fn(*fwd_args, n_ctx, False) + scan_fn(*bwd_args, n_ctx, True)


def _multiscale_pool(u, pool_w, pool_scale):
    rows, length, width = u.shape
    groups = len(POOL_WINDOWS)
    gdim = width // groups
    uf = u.reshape(rows, length, groups, gdim)
    cs = jnp.concatenate([jnp.zeros_like(uf[:, :1]), jnp.cumsum(uf, axis=1)], axis=1)
    pos = np.arange(length)
    mixed = []
    for gi, win in enumerate(POOL_WINDOWS):
        below, above = win // 2, win - win // 2
        lo = np.clip(pos - below, 0, length - 1)
        hi = np.clip(pos + win - 1 - below, 0, length - 1)
        cnt = jnp.asarray((hi - lo + 1).astype(np.float32))[None, :, None]
        csg = cs[:, :, gi, :]
        upper = jnp.concatenate([csg[:, above:]] + [csg[:, length:]] * (above - 1), axis=1)
        lower = jnp.concatenate([jnp.zeros_like(csg[:, :below]), csg[:, :length - below]], axis=1)
        mixed.append((upper - lower) / cnt - uf[:, :, gi, :])
    d = jnp.stack(mixed, axis=2)
    y = jnp.einsum("rlgc,gcd->rlgd", d, pool_w)
    return y.reshape(rows, length, width) * pool_scale


def _short_conv(u, w):
    length = u.shape[0]
    left = C_CONV // 2
    up = jnp.pad(u, ((left, C_CONV - 1 - left), (0, 0)))
    out = up[0:length] * w[0]
    for j in range(1, C_CONV):
        out = out + up[j:j + length] * w[j]
    return out


def _hgrn2_gates(pre_f, lb):
    log_f = jnp.log(lb + (1.0 - lb) * jax.nn.sigmoid(pre_f))
    k = (1.0 - lb) * jax.nn.sigmoid(-pre_f)
    return _heads(k, A_HEAD_DIM), _heads(log_f, A_HEAD_DIM)


def _even_mixer(h, n_ctx, lb, w_in, w_in_hook, a_norm, pool_w, pool_scale, w_out, w_out_hook):
    a_width = w_out.shape[0] // 2
    p = _dense(h, w_in, w_in_hook)
    q, f_f, f_b, i, g, u = jnp.split(p, [a_width * s for s in range(1, 6)], axis=-1)
    q = _heads(jax.nn.silu(q), A_HEAD_DIM)
    i = _heads(i, A_HEAD_DIM)
    k_f, logf_f = _hgrn2_gates(f_f, lb[0])
    k_b, logf_b = _hgrn2_gates(f_b, lb[1])
    o = _bidirectional(_hgrn2_scan, (q, k_f, i, logf_f), (q, k_b, i, logf_b), n_ctx)
    b_width = u.shape[-1]
    u_l = u[n_ctx:].reshape(-1, GRID_W, b_width)
    pooled = jnp.concatenate([
        _multiscale_pool(u[None, :n_ctx], pool_w, pool_scale)[0],
        _multiscale_pool(u_l, pool_w, pool_scale).reshape(-1, b_width),
    ], axis=0)
    a_out = _rmsnorm(o, a_norm) * jax.nn.silu(_heads(g, A_HEAD_DIM))
    a_out = a_out.reshape(a_out.shape[0], a_width)
    return _dense(jnp.concatenate([a_out, pooled], axis=-1), w_out, w_out_hook)


def _odd_mixer(h, n_ctx, w_main, w_main_hook, w_gate, w_gate_hook, conv_w, a_log, dt_bias, norm_w, w_out, w_out_hook):
    value_width = w_out.shape[0]
    key_width = value_width // 2
    a_rate = jnp.exp(a_log)
    p = _dense(h, w_main, w_main_hook)
    gates = _dense(h, w_gate, w_gate_hook)
    qkv, z = p[:, :2 * key_width + value_width], p[:, 2 * key_width + value_width:]
    qkv = jnp.concatenate([_short_conv(qkv[:n_ctx], conv_w), _short_conv(qkv[n_ctx:], conv_w)], axis=0)
    qkv = jax.nn.silu(qkv)
    q, k, v = jnp.split(qkv, [key_width, 2 * key_width], axis=-1)
    q = _chunks(_l2norm(_heads(q, C_HEAD_DIM)) * C_HEAD_DIM ** -0.5, C_CHUNK)
    k = _chunks(_l2norm(_heads(k, C_HEAD_DIM)), C_CHUNK)
    v = _chunks(_heads(v, C_HEAD_DIM), C_CHUNK)
    a_f, a_b, b_f, b_b = jnp.split(gates, 4, axis=-1)
    g_f = -a_rate[0] * jax.nn.softplus(a_f + dt_bias[0])
    g_b = -a_rate[1] * jax.nn.softplus(a_b + dt_bias[1])
    o = _bidirectional(_gdn_scan, (q, k, v, g_f, jax.nn.sigmoid(b_f)), (q, k, v, g_b, jax.nn.sigmoid(b_b)), n_ctx)
    y = _rmsnorm(o, norm_w) * jax.nn.silu(_heads(z, C_HEAD_DIM))
    return _dense(y.reshape(y.shape[0], value_width), w_out, w_out_hook)


def _swiglu(h, w13, w13_hook, w2, w2_hook):
    gate, up = jnp.split(_dense(h, w13, w13_hook), 2, axis=-1)
    return _dense(jax.nn.silu(gate) * up, w2, w2_hook)


def _forward(x, ctx, mods, big, hooks, small):
    n_ctx = ctx.shape[0]
    d = x.shape[-1]
    stream = jnp.concatenate([ctx, x], axis=0)
    is_ctx = (jnp.arange(stream.shape[0]) < n_ctx)[:, None]
    lb_all = jnp.cumsum(jax.nn.softmax(small["ev_lb"], axis=1), axis=1)
    for layer in range(2):
        m = [jnp.where(is_ctx, mods[layer, 1, s * d:(s + 1) * d][None, :], mods[layer, 0, s * d:(s + 1) * d][None, :])
             for s in range(6)]
        nw = small["norm_w"][layer]
        h = _rmsnorm(stream, nw[0]) * (1.0 + m[1]) + m[0]
        if layer == 0:
            y = _even_mixer(h, n_ctx, lb_all[:, layer], big["ev_w_in"], hooks["ev_w_in"], small["ev_a_norm"][0],
                            small["ev_pool_w"][0], small["ev_pool_scale"][0], big["ev_w_out"], hooks["ev_w_out"])
        else:
            y = _odd_mixer(h, n_ctx, big["od_w_main"], hooks["od_w_main"], big["od_w_gate"], hooks["od_w_gate"],
                           small["od_conv"][0], small["od_A_log"][0], small["od_dt_bias"][0], small["od_norm"][0],
                           big["od_w_out"], hooks["od_w_out"])
        stream = stream + m[2] * _rmsnorm(y, nw[1])
        h = _rmsnorm(stream, nw[2]) * (1.0 + m[4]) + m[3]
        f = _swiglu(h, big["ffn_w13_%d" % layer], hooks["ffn_w13_%d" % layer],
                    big["ffn_w2_%d" % layer], hooks["ffn_w2_%d" % layer])
        stream = stream + m[5] * _rmsnorm(f, nw[3])
    return stream[n_ctx:]


def _pack(arrays):
    flat = jnp.concatenate([a.reshape(-1) for a in arrays])
    pad = (-flat.shape[0]) % (SUBLANE * LANE)
    return jnp.pad(flat, (0, pad)).reshape(-1, LANE)


def _unpack(packed, shapes, lead=()):
    flat = packed.reshape(lead + (-1,))
    out, at = [], 0
    for shape in shapes:
        size = int(np.prod(shape))
        out.append(flat[..., at:at + size].reshape(lead + tuple(shape)))
        at += size
    return out


def _from_chips(gathered, axis):
    return jnp.concatenate([gathered[2 * j] for j in range(N_CHIP)], axis=axis)


def _col_natural(blocks):
    nblk, k, n = blocks.shape
    return blocks.transpose(1, 0, 2).reshape(k, nblk * n)


def _col_blocked(nat):
    k, n4 = nat.shape
    return nat.reshape(k, N_CHIP, n4 // N_CHIP).transpose(1, 0, 2)


def kernel(x, c, ctx, c_ctx, w_ada, b_ada, norm_w, ev_w_in, ev_lb, ev_a_norm, ev_pool_w, ev_pool_scale, ev_w_out, od_w_in, od_conv, od_A_log, od_dt_bias, od_norm, od_w_out, ffn_w13, ffn_w2, loss_target, m_c_ctx, m_w_ada, m_b_ada, m_norm_w, m_ev_w_in, m_ev_lb, m_ev_a_norm, m_ev_pool_w, m_ev_pool_scale, m_ev_w_out, m_od_w_in, m_od_conv, m_od_A_log, m_od_dt_bias, m_od_norm, m_od_w_out, m_ffn_w13, m_ffn_w2, v_c_ctx, v_w_ada, v_b_ada, v_norm_w, v_ev_w_in, v_ev_lb, v_ev_a_norm, v_ev_pool_w, v_ev_pool_scale, v_ev_w_out, v_od_w_in, v_od_conv, v_od_A_log, v_od_dt_bias, v_od_norm, v_od_w_out, v_ffn_w13, v_ffn_w2):
    ix, iy, ic = _place()
    me = 4 * ix + 2 * iy + ic
    chip = 2 * ix + iy
    d = x.shape[-1]
    layers = w_ada.shape[0]
    n_ada = w_ada.shape[-1]

    pre_parts = [c[0], norm_w, ev_lb, ev_pool_w, od_conv]
    pre = _all_gather8(_pack(pre_parts), "gather_small_inputs")
    c_all, norm_w_g, ev_lb_g, pool_w_g, od_conv_g = _unpack(pre, [p.shape for p in pre_parts], lead=(N_DEV,))
    small = {
        "norm_w": _from_chips(norm_w_g, 2),
        "ev_lb": _from_chips(ev_lb_g, 2),
        "ev_a_norm": ev_a_norm,
        "ev_pool_w": _from_chips(pool_w_g, 2),
        "ev_pool_scale": ev_pool_scale,
        "od_conv": _from_chips(od_conv_g, 2),
        "od_A_log": od_A_log,
        "od_dt_bias": od_dt_bias,
        "od_norm": od_norm,
    }

    silu_cc, silu_cc_vjp = jax.vjp(jax.nn.silu, c_ctx)
    a16 = jnp.concatenate([jax.nn.silu(c_all), silu_cc[None], jnp.zeros((2 * SUBLANE - N_DEV - 1, d), F32)], axis=0)
    mods_local = _ada_fwd(a16, w_ada)
    mods_g = _all_gather8(mods_local.reshape(-1, n_ada), "gather_modulation")
    mods_g = mods_g.reshape(N_DEV, layers, 2 * SUBLANE, n_ada)
    mods_full = _from_chips(mods_g, 2) + b_ada[:, None, :]
    mods = jnp.stack([lax.dynamic_index_in_dim(mods_full, me, axis=1, keepdims=False), mods_full[:, N_DEV]], axis=1)

    ffn13_rows = ffn_w13.shape[1]
    ffn2_rows = ffn_w2.shape[1]
    shards = [ev_w_in[0], ev_w_out[0], od_w_in[0], od_w_out[0],
              ffn_w13.reshape(-1, ffn_w13.shape[-1]), ffn_w2.reshape(-1, ffn_w2.shape[-1])]
    g_ev_in, g_ev_out, g_od_in, g_od_out, g_w13, g_w2 = _gather_weights([_cast_into_block(s, chip) for s in shards])
    od_nat = _col_natural(g_od_in)
    n_gate = 4 * (od_w_out.shape[1] * N_CHIP // C_HEAD_DIM)
    n_main = od_nat.shape[1] - n_gate
    big = {
        "ev_w_in": _col_natural(g_ev_in),
        "ev_w_out": g_ev_out.reshape(-1, g_ev_out.shape[-1]),
        "od_w_main": od_nat[:, :n_main],
        "od_w_gate": od_nat[:, n_main:],
        "od_w_out": g_od_out.reshape(-1, g_od_out.shape[-1]),
    }
    for layer in range(layers):
        big["ffn_w13_%d" % layer] = _col_natural(g_w13[:, layer * ffn13_rows:(layer + 1) * ffn13_rows])
        big["ffn_w2_%d" % layer] = g_w2[:, layer * ffn2_rows:(layer + 1) * ffn2_rows].reshape(-1, g_w2.shape[-1])
    hooks = {name: jnp.zeros(w.shape, F32) for name, w in big.items()}

    def local_forward(x_, mods_, hooks_, small_):
        return _forward(x_, ctx[0], mods_, big, hooks_, small_)

    y, pullback = jax.vjp(local_forward, x[0], mods, hooks, small)
    dy, sq = _loss_head(y, loss_target[0])
    loss = lax.psum(jnp.sum(sq) * (0.5 / d), ("x", "y", "c"))
    grad_x, d_mods, d_big, d_small = pullback(dy)

    d_od_in = jnp.concatenate([d_big["od_w_main"], d_big["od_w_gate"]], axis=1)
    blocked = [
        _col_blocked(d_big["ev_w_in"]),
        d_big["ev_w_out"].reshape(N_CHIP, -1, d_big["ev_w_out"].shape[-1]),
        _col_blocked(d_od_in),
        d_big["od_w_out"].reshape(N_CHIP, -1, d_big["od_w_out"].shape[-1]),
        jnp.concatenate([_col_blocked(d_big["ffn_w13_%d" % layer]) for layer in range(layers)], axis=1),
        jnp.concatenate([d_big["ffn_w2_%d" % layer].reshape(N_CHIP, -1, d_big["ffn_w2_%d" % layer].shape[-1])
                         for layer in range(layers)], axis=1),
    ]
    r_ev_in, r_ev_out, r_od_in, r_od_out, r_w13, r_w2 = _reduce_grads(blocked, ix, iy, ic)

    small_names = ["norm_w", "ev_lb", "ev_a_norm", "ev_pool_w", "ev_pool_scale", "od_conv", "od_A_log",
                   "od_dt_bias", "od_norm"]
    post_parts = [d_mods[:, 0], d_mods[:, 1]] + [d_small[n] for n in small_names]
    post = _all_gather8(_pack(post_parts), "gather_small_grads")
    post_sum = _sum_leading(post, "sum_small_grads")
    dm_l_all = _unpack(post, [post_parts[0].shape], lead=(N_DEV,))[0]
    summed = _unpack(post_sum, [p.shape for p in post_parts])
    dm_l_sum, dm_c_sum = summed[0], summed[1]
    g_small = dict(zip(small_names, summed[2:]))
    grad_b_ada = dm_l_sum + dm_c_sum

    def my_cols(full, width):
        return lax.dynamic_slice_in_dim(full, chip * width, width, axis=full.ndim - 1)

    dm_rows = jnp.concatenate([
        my_cols(dm_l_all, n_ada).transpose(1, 0, 2),
        my_cols(dm_c_sum, n_ada)[:, None, :],
        jnp.zeros((layers, 2 * SUBLANE - N_DEV - 1, n_ada), F32),
    ], axis=1)
    grad_w_ada, ga = _ada_bwd(a16, dm_rows, w_ada)
    ga_g = _all_gather8(jnp.pad(ga[N_DEV][None], ((0, SUBLANE - 1), (0, 0))), "gather_c_ctx_grad")
    g_silu_cc = _sum_leading(jnp.stack([ga_g[2 * j] for j in range(N_CHIP)]), "sum_c_ctx_grad")[0]
    grad_c_ctx = silu_cc_vjp(g_silu_cc)[0]

    grads = {
        "c_ctx": grad_c_ctx,
        "w_ada": grad_w_ada,
        "b_ada": grad_b_ada,
        "norm_w": my_cols(g_small["norm_w"], norm_w.shape[-1]),
        "ev_w_in": r_ev_in[None],
        "ev_lb": my_cols(g_small["ev_lb"], ev_lb.shape[-1]),
        "ev_a_norm": g_small["ev_a_norm"],
        "ev_pool_w": lax.dynamic_slice_in_dim(g_small["ev_pool_w"], chip * ev_pool_w.shape[2], ev_pool_w.shape[2], axis=2),
        "ev_pool_scale": g_small["ev_pool_scale"],
        "ev_w_out": r_ev_out[None],
        "od_w_in": r_od_in[None],
        "od_conv": my_cols(g_small["od_conv"], od_conv.shape[-1]),
        "od_A_log": g_small["od_A_log"],
        "od_dt_bias": g_small["od_dt_bias"],
        "od_norm": g_small["od_norm"],
        "od_w_out": r_od_out[None],
        "ffn_w13": r_w13.reshape(ffn_w13.shape),
        "ffn_w2": r_w2.reshape(ffn_w2.shape),
    }

    weights = dict(c_ctx=c_ctx, w_ada=w_ada, b_ada=b_ada, norm_w=norm_w, ev_w_in=ev_w_in, ev_lb=ev_lb,
                   ev_a_norm=ev_a_norm, ev_pool_w=ev_pool_w, ev_pool_scale=ev_pool_scale, ev_w_out=ev_w_out,
                   od_w_in=od_w_in, od_conv=od_conv, od_A_log=od_A_log, od_dt_bias=od_dt_bias, od_norm=od_norm,
                   od_w_out=od_w_out, ffn_w13=ffn_w13, ffn_w2=ffn_w2)
    first = dict(c_ctx=m_c_ctx, w_ada=m_w_ada, b_ada=m_b_ada, norm_w=m_norm_w, ev_w_in=m_ev_w_in, ev_lb=m_ev_lb,
                 ev_a_norm=m_ev_a_norm, ev_pool_w=m_ev_pool_w, ev_pool_scale=m_ev_pool_scale, ev_w_out=m_ev_w_out,
                 od_w_in=m_od_w_in, od_conv=m_od_conv, od_A_log=m_od_A_log, od_dt_bias=m_od_dt_bias,
                 od_norm=m_od_norm, od_w_out=m_od_w_out, ffn_w13=m_ffn_w13, ffn_w2=m_ffn_w2)
    second = dict(c_ctx=v_c_ctx, w_ada=v_w_ada, b_ada=v_b_ada, norm_w=v_norm_w, ev_w_in=v_ev_w_in, ev_lb=v_ev_lb,
                  ev_a_norm=v_ev_a_norm, ev_pool_w=v_ev_pool_w, ev_pool_scale=v_ev_pool_scale, ev_w_out=v_ev_w_out,
                  od_w_in=v_od_w_in, od_conv=v_od_conv, od_A_log=v_od_A_log, od_dt_bias=v_od_dt_bias,
                  od_norm=v_od_norm, od_w_out=v_od_w_out, ffn_w13=v_ffn_w13, ffn_w2=v_ffn_w2)
    names = list(weights)
    large = ["w_ada", "ev_w_in", "ev_w_out", "od_w_in", "od_w_out", "ffn_w13", "ffn_w2"]
    little = [n for n in names if n not in large]
    delta, new_m, new_v = {}, {}, {}
    for n in large:
        as2d = lambda t: t.reshape(-1, t.shape[-1])
        out = _adamw(as2d(weights[n]), as2d(grads[n]), as2d(first[n]), as2d(second[n]), "adamw_" + n)
        delta[n], new_m[n], new_v[n] = (t.reshape(weights[n].shape) for t in out)
    shapes = [weights[n].shape for n in little]
    out = _adamw(*(_pack([src[n] for n in little]) for src in (weights, grads, first, second)), "adamw_small")
    for res, packed in zip((delta, new_m, new_v), out):
        res.update(zip(little, _unpack(packed, shapes)))

    return (loss, grad_x[None], *[grads[n] for n in names], *[delta[n] for n in names],
            *[new_m[n] for n in names], *[new_v[n] for n in names])
```

```python
import functools

import numpy as np
import jax
import jax.numpy as jnp
from jax import lax
from jax.experimental import pallas as pl
from jax.experimental.pallas import tpu as pltpu

F32 = jnp.float32
BF16 = jnp.bfloat16
MESH = pl.DeviceIdType.MESH

EPS = 1e-6
GRID_W = 64
A_HEAD_DIM = 128
A_CHUNK = 32
POOL_WINDOWS = (2, 4, 8, 16)
C_HEAD_DIM = 128
C_CONV = 4
C_CHUNK = 64

ADAM_LR = 0.001
ADAM_B1 = 0.9
ADAM_B2 = 0.999
ADAM_EPS = 1e-08
ADAM_WD = 0.01
ADAM_STEP = 10

N_DEV = 8
N_CHIP = 4
LANE = 128
SUBLANE = 8
VMEM_LIMIT = 48 * 1024 * 1024

NN = (((1,), (0,)), ((), ()))
NT = (((1,), (1,)), ((), ()))
TN = (((0,), (0,)), ((), ()))


def _tile(n, cap, mult=LANE):
    best = 0
    for d in range(mult, min(n, cap) + 1, mult):
        if n % d == 0:
            best = d
    return best or n


def _div_le(n, cap):
    return max(d for d in range(1, cap + 1) if n % d == 0)


def _params(*sem):
    return pltpu.CompilerParams(dimension_semantics=sem or None, vmem_limit_bytes=VMEM_LIMIT)


def _dot(a, b, dims):
    return lax.dot_general(a, b, dims, preferred_element_type=F32)


def _matmul(a, b, mode, name):
    if mode == "nn":
        (m, k), n = a.shape, b.shape[1]
    elif mode == "nt":
        (m, k), n = a.shape, b.shape[0]
    else:
        (k, m), n = a.shape, b.shape[1]
    tm, tn, tk = _tile(m, 1024), _tile(n, 1024), _tile(k, 2048)
    nk = k // tk
    dims = {"nn": NN, "nt": NT, "tn": TN}[mode]
    if mode == "tn":
        a_spec = pl.BlockSpec((tk, tm), lambda i, j, kk: (kk, i))
    else:
        a_spec = pl.BlockSpec((tm, tk), lambda i, j, kk: (i, kk))
    if mode == "nt":
        b_spec = pl.BlockSpec((tn, tk), lambda i, j, kk: (j, kk))
    else:
        b_spec = pl.BlockSpec((tk, tn), lambda i, j, kk: (kk, j))

    def body(a_ref, b_ref, o_ref):
        part = _dot(a_ref[...], b_ref[...], dims)
        if nk == 1:
            o_ref[...] = part
        else:
            kk = pl.program_id(2)

            @pl.when(kk == 0)
            def _():
                o_ref[...] = part

            @pl.when(kk > 0)
            def _():
                o_ref[...] += part

    return pl.pallas_call(
        body,
        out_shape=jax.ShapeDtypeStruct((m, n), F32),
        grid=(m // tm, n // tn, nk),
        in_specs=[a_spec, b_spec],
        out_specs=pl.BlockSpec((tm, tn), lambda i, j, kk: (i, j)),
        compiler_params=_params("parallel", "parallel", "arbitrary"),
        name=name,
    )(a, b)


@jax.custom_vjp
def _dense(h, w, hook):
    return _matmul(h.astype(BF16), w, "nn", "dense_fwd")


def _dense_fwd(h, w, hook):
    hb = h.astype(BF16)
    return _matmul(hb, w, "nn", "dense_fwd"), (hb, w)


def _dense_bwd(res, dy):
    hb, w = res
    dyb = dy.astype(BF16)
    return _matmul(dyb, w, "nt", "dense_dx"), None, _matmul(hb, dyb, "tn", "dense_dw")


_dense.defvjp(_dense_fwd, _dense_bwd)


def _rec_plan(n, nc, anti, backward):
    nb = _div_le(int(np.gcd(n, nc)), 12)
    ng, ncb = n // nb, nc // nb
    if not anti:
        order = (lambda g: ng - 1 - g) if backward else (lambda g: g)
        return nb, ng, order, backward
    if backward:
        return nb, ng, (lambda g: jnp.where(g < ng - ncb, ncb + g, g - (ng - ncb))), False
    return nb, ng, (lambda g: jnp.where(g < ncb, ncb - 1 - g, ng - 1 - (g - ncb))), True


REC_HEADS = 4


def _rec_call(body, ins, outs, plan, state_shape, name):
    nb, ng, order, descending = plan
    hh = outs[0].shape[0]
    hb = _div_le(hh, REC_HEADS)

    def spec(shape):
        shared = hh // shape[0]
        return pl.BlockSpec((hb // shared, nb) + tuple(shape[2:]), lambda h, g: (h, order(g), 0, 0))

    return pl.pallas_call(
        functools.partial(body, hb, nb, descending),
        out_shape=outs,
        grid=(hh // hb, ng),
        in_specs=[spec(t.shape) for t in ins],
        out_specs=[spec(t.shape) for t in outs],
        scratch_shapes=[pltpu.VMEM((hb,) + tuple(state_shape), F32)],
        compiler_params=_params("parallel", "arbitrary"),
        name=name,
    )(*ins)


def _chunk_loop(nb, descending, step):
    def run(i, carry):
        step(nb - 1 - i if descending else i)
        return carry

    lax.fori_loop(0, nb, run, 0)


def _reset_at_start(state):
    @pl.when(pl.program_id(1) == 0)
    def _():
        state[...] = jnp.zeros_like(state)


def _bf(ref, h, c):
    return ref[h, c].astype(BF16)


def _within_chunk(scores, anti):
    row = lax.broadcasted_iota(jnp.int32, scores.shape, 0)
    col = lax.broadcasted_iota(jnp.int32, scores.shape, 1)
    return jnp.where(row <= col if anti else row >= col, scores, 0.0)


def _hgrn_rec_fwd(anti, nc, q_in, k_out, v, decay, qs, ks):
    hh, n, _, dk = q_in.shape
    dv = v.shape[-1]

    def body(hb, nb, descending, q_ref, k_ref, v_ref, d_ref, qs_ref, ks_ref, o_ref, s_ref, st):
        _reset_at_start(st)
        heads = range(hb)

        def step(c):
            s = [st[h] for h in heads]
            vb = [_bf(v_ref, h, c) for h in heads]
            scores = [_within_chunk(_dot(_bf(qs_ref, h, c), _bf(ks_ref, h, c), NT), anti) for h in heads]
            for h in heads:
                s_ref[h, c] = s[h]
                o_ref[h, c] = (_dot(_bf(q_ref, h, c), s[h].astype(BF16), NT)
                               + _dot(scores[h].astype(BF16), vb[h], NN))
            for h in heads:
                st[h] = s[h] * d_ref[h, c] + _dot(vb[h], _bf(k_ref, h, c), TN)

        _chunk_loop(nb, descending, step)

    ins = (q_in, k_out, v, decay, qs, ks)
    outs = (jax.ShapeDtypeStruct(v.shape, F32), jax.ShapeDtypeStruct((hh, n, dv, dk), F32))
    return _rec_call(body, ins, outs, _rec_plan(n, nc, anti, False), (dv, dk), "hgrn_rec_fwd")


def _hgrn_rec_bwd(anti, nc, q_in, k_out, v, decay, qs, ks, states, do):
    dk = q_in.shape[-1]
    n, dv = v.shape[1], v.shape[-1]

    def body(hb, nb, descending, q_ref, k_ref, v_ref, d_ref, qs_ref, ks_ref, s_ref, do_ref,
             dq_ref, dk_ref, dv_ref, dd_ref, dqs_ref, dks_ref, dst):
        _reset_at_start(dst)
        heads = range(hb)

        def step(c):
            ds = [dst[h] for h in heads]
            dsb = [x.astype(BF16) for x in ds]
            s = [s_ref[h, c] for h in heads]
            dob = [_bf(do_ref, h, c) for h in heads]
            vb = [_bf(v_ref, h, c) for h in heads]
            qsb = [_bf(qs_ref, h, c) for h in heads]
            ksb = [_bf(ks_ref, h, c) for h in heads]
            scores = [_within_chunk(_dot(qsb[h], ksb[h], NT), anti).astype(BF16) for h in heads]
            dscores = [_within_chunk(_dot(dob[h], vb[h], NT), anti).astype(BF16) for h in heads]
            for h in heads:
                dqs_ref[h, c] = _dot(dscores[h], ksb[h], NN)
                dks_ref[h, c] = _dot(dscores[h], qsb[h], TN)
                dq_ref[h, c] = _dot(dob[h], s[h].astype(BF16), NN)
                dk_ref[h, c] = _dot(vb[h], dsb[h], NN)
                dv_ref[h, c] = _dot(_bf(k_ref, h, c), dsb[h], NT) + _dot(scores[h], dob[h], TN)
                dd_ref[h, c] = jnp.sum(ds[h] * s[h], axis=0, keepdims=True)
            for h in heads:
                dst[h] = ds[h] * d_ref[h, c] + _dot(dob[h], _bf(q_ref, h, c), TN)

        _chunk_loop(nb, descending, step)

    ins = (q_in, k_out, v, decay, qs, ks, states, do)
    outs = tuple(jax.ShapeDtypeStruct(t.shape, F32) for t in (q_in, k_out, v, decay, qs, ks))
    return _rec_call(body, ins, outs, _rec_plan(n, nc, anti, True), (dv, dk), "hgrn_rec_bwd")


@functools.partial(jax.custom_vjp, nondiff_argnums=(0, 1))
def _hgrn_rec(anti, nc, q_in, k_out, v, decay, qs, ks):
    return _hgrn_rec_fwd(anti, nc, q_in, k_out, v, decay, qs, ks)[0]


def _hgrn_rec_vjp_fwd(anti, nc, q_in, k_out, v, decay, qs, ks):
    o, states = _hgrn_rec_fwd(anti, nc, q_in, k_out, v, decay, qs, ks)
    return o, (q_in, k_out, v, decay, qs, ks, states)


def _hgrn_rec_vjp_bwd(anti, nc, res, do):
    return _hgrn_rec_bwd(anti, nc, *res, do)


_hgrn_rec.defvjp(_hgrn_rec_vjp_fwd, _hgrn_rec_vjp_bwd)


def _gdn_rec_fwd(anti, nc, u, w, qk, q, k, scal, gl):
    hh, n, _, dk = w.shape
    dv = u.shape[-1]
    rep = hh // q.shape[0]

    def body(hb, nb, descending, u_ref, w_ref, qk_ref, q_ref, k_ref, sc_ref, gl_ref, o_ref, s_ref, vn_ref, st):
        _reset_at_start(st)
        heads = range(hb)

        def step(c):
            s = [st[h] for h in heads]
            sb = [x.astype(BF16) for x in s]
            sc = [_scalars(sc_ref, h, c, dk, (EG, EGR)) for h in heads]
            qd = [(q_ref[h // rep, c] * sc[h][0]).astype(BF16) for h in heads]
            kd = [(k_ref[h // rep, c] * sc[h][1]).astype(BF16) for h in heads]
            vn = [u_ref[h, c] - _dot(_bf(w_ref, h, c), sb[h], NT) for h in heads]
            vnb = [x.astype(BF16) for x in vn]
            for h in heads:
                s_ref[h, c] = s[h]
                vn_ref[h, c] = vn[h]
                o_ref[h, c] = _dot(qd[h], sb[h], NT) + _dot(_bf(qk_ref, h, c), vnb[h], NN)
            for h in heads:
                st[h] = s[h] * gl_ref[h, c] + _dot(vnb[h], kd[h], TN)

        _chunk_loop(nb, descending, step)

    ins = (u, w, qk, q, k, scal, gl)
    outs = (
        jax.ShapeDtypeStruct(u.shape, F32),
        jax.ShapeDtypeStruct((hh, n, dv, dk), F32),
        jax.ShapeDtypeStruct(u.shape, F32),
    )
    return _rec_call(body, ins, outs, _rec_plan(n, nc, anti, False), (dv, dk), "gdn_rec_fwd")


def _gdn_rec_bwd(anti, nc, w, qk, q, k, scal, gl, states, vnew, do):
    hh, n, _, dk = w.shape
    dv = vnew.shape[-1]
    rep = hh // q.shape[0]

    def body(hb, nb, descending, w_ref, qk_ref, q_ref, k_ref, sc_ref, gl_ref, s_ref, vn_ref, do_ref,
             du_ref, dw_ref, dqk_ref, dq_ref, dk_ref, dsc_ref, dgl_ref, dst):
        _reset_at_start(dst)
        heads = range(hb)

        def step(c):
            ds = [dst[h] for h in heads]
            dsb = [x.astype(BF16) for x in ds]
            s = [s_ref[h, c] for h in heads]
            sb = [x.astype(BF16) for x in s]
            sc = [_scalars(sc_ref, h, c, dk, (EG, EGR)) for h in heads]
            qf = [q_ref[h // rep, c] for h in heads]
            kf = [k_ref[h // rep, c] for h in heads]
            qd = [(qf[h] * sc[h][0]).astype(BF16) for h in heads]
            kd = [(kf[h] * sc[h][1]).astype(BF16) for h in heads]
            vnb = [_bf(vn_ref, h, c) for h in heads]
            dob = [_bf(do_ref, h, c) for h in heads]
            dvn = [_dot(_bf(qk_ref, h, c), dob[h], TN) + _dot(kd[h], dsb[h], NT) for h in heads]
            dvnb = [x.astype(BF16) for x in dvn]
            dqd = [_dot(dob[h], sb[h], NN) for h in heads]
            dkd = [_dot(vnb[h], dsb[h], NN) for h in heads]
            for h in heads:
                du_ref[h, c] = dvn[h]
                dw_ref[h, c] = -_dot(dvnb[h], sb[h], NN)
                dqk_ref[h, c] = _dot(dob[h], vnb[h], NT)
                dq_ref[h, c] = dqd[h] * sc[h][0]
                dk_ref[h, c] = dkd[h] * sc[h][1]
                dsc_ref[h, c] = _scalar_rows([None, _token_sums(dqd[h] * qf[h]), _token_sums(dkd[h] * kf[h])])
                dgl_ref[h, c] = jnp.sum(ds[h] * s[h], axis=0, keepdims=True)
            for h in heads:
                dst[h] = ds[h] * gl_ref[h, c] + _dot(dob[h], qd[h], TN) - _dot(dvnb[h], _bf(w_ref, h, c), TN)

        _chunk_loop(nb, descending, step)

    ins = (w, qk, q, k, scal, gl, states, vnew, do)
    outs = (jax.ShapeDtypeStruct(vnew.shape, F32), jax.ShapeDtypeStruct(w.shape, F32),
            jax.ShapeDtypeStruct(qk.shape, F32), jax.ShapeDtypeStruct((hh,) + q.shape[1:], F32),
            jax.ShapeDtypeStruct((hh,) + k.shape[1:], F32), jax.ShapeDtypeStruct(scal.shape, F32),
            jax.ShapeDtypeStruct(gl.shape, F32))
    return _rec_call(body, ins, outs, _rec_plan(n, nc, anti, True), (dv, dk), "gdn_rec_bwd")


@functools.partial(jax.custom_vjp, nondiff_argnums=(0, 1))
def _gdn_rec(anti, nc, u, w, qk, q, k, scal, gl):
    return _gdn_rec_fwd(anti, nc, u, w, qk, q, k, scal, gl)[0]


def _gdn_rec_vjp_fwd(anti, nc, u, w, qk, q, k, scal, gl):
    o, states, vnew = _gdn_rec_fwd(anti, nc, u, w, qk, q, k, scal, gl)
    return o, (w, qk, q, k, scal, gl, states, vnew)


def _gdn_rec_vjp_bwd(anti, nc, res, do):
    du, dw, dqk, dq, dk, dsc, dgl = _gdn_rec_bwd(anti, nc, *res, do)
    return du, dw, dqk, _add_shared_heads(dq, res[2].shape[0]), _add_shared_heads(dk, res[3].shape[0]), dsc, dgl


_gdn_rec.defvjp(_gdn_rec_vjp_fwd, _gdn_rec_vjp_bwd)


def _split(a):
    hi = a.astype(BF16)
    return hi, (a - hi.astype(F32)).astype(BF16)


def _dot3(a, b, dims=NN):
    ah, al = _split(a)
    bh, bl = _split(b)
    return _dot(ah, bh, dims) + (_dot(ah, bl, dims) + _dot(al, bh, dims))


INV_BASE = 8


def _inv_unit_triangular(mats):
    size = mats[0].shape[0]
    row = lax.broadcasted_iota(jnp.int32, (size, size), 0)
    col = lax.broadcasted_iota(jnp.int32, (size, size), 1)
    eye = jnp.where(row == col, 1.0, 0.0).astype(F32)

    def same_block(width):
        shift = width.bit_length() - 1
        return jnp.right_shift(row, shift) == jnp.right_shift(col, shift)

    base = [jnp.where(same_block(INV_BASE), m, 0.0) for m in mats]
    sq = [_dot3(b, b) for b in base]
    inv = [_dot3(eye - b, eye + s) for b, s in zip(base, sq)]
    width = 4
    while width < INV_BASE:
        sq = [_dot3(s, s) for s in sq]
        inv = [_dot3(i, eye + s) for i, s in zip(inv, sq)]
        width *= 2
    while width < size:
        outer = same_block(2 * width) & jnp.logical_not(same_block(width))
        part = [_dot3(i, jnp.where(outer, m, 0.0)) for i, m in zip(inv, mats)]
        inv = [i - _dot3(p, i) for i, p in zip(inv, part)]
        width *= 2
    return inv


PREP_CHUNKS = 6


def _prep_call(body, ins, outs, name):
    heads = outs[0].shape[0]
    nb = _div_le(ins[0].shape[1], PREP_CHUNKS)

    def spec(shape):
        rep = heads // shape[0]
        return pl.BlockSpec((1, nb) + tuple(shape[2:]), lambda h, g: (h // rep, g, 0, 0))

    return pl.pallas_call(
        functools.partial(body, nb),
        out_shape=outs,
        grid=(heads, ins[0].shape[1] // nb),
        in_specs=[spec(t.shape) for t in ins],
        out_specs=[spec(t.shape) for t in outs],
        compiler_params=_params("parallel", "parallel"),
        name=name,
    )(*ins)


def _lane_broadcast(row, width):
    size = row.shape[1]
    r = lax.broadcasted_iota(jnp.int32, (size, size), 0)
    c = lax.broadcasted_iota(jnp.int32, (size, size), 1)
    hi, lo = _split(jnp.where(r == c, jnp.broadcast_to(row, (size, size)), 0.0))
    ones = jnp.ones((size, width), BF16)
    return _dot(hi, ones, NN) + _dot(lo, ones, NN)


def _token_sums(p):
    hi, lo = _split(p)
    ones = jnp.ones((SUBLANE, p.shape[1]), BF16)
    return _dot(ones, hi, NT) + _dot(ones, lo, NT)


def _scalar_rows(rows):
    shape = next(r.shape for r in rows if r is not None)
    index = lax.broadcasted_iota(jnp.int32, shape, 0)
    out = jnp.zeros(shape, F32)
    for j, r in enumerate(rows):
        if r is not None:
            out = jnp.where(index == j, r, out)
    return out


BETA, EG, EGR = 0, 1, 2


def _scalars(sc_ref, h, c, width, which):
    rows = sc_ref[h, c]
    return [_lane_broadcast(rows[j:j + 1], width) for j in which]


def _off_diagonal(x):
    row = lax.broadcasted_iota(jnp.int32, x.shape, 0)
    col = lax.broadcasted_iota(jnp.int32, x.shape, 1)
    return jnp.where(row == col, 0.0, x)


def _gdn_prep_fwd(q, k, v, decay, scal):
    dk, dv = k.shape[-1], v.shape[-1]

    def body(nb, q_ref, k_ref, v_ref, dec_ref, sc_ref, u_ref, w_ref, qk_ref, t_ref):
        chunks = range(nb)
        sc = [_scalars(sc_ref, 0, c, dk, (BETA, EG)) for c in chunks]
        kf = [k_ref[0, c] for c in chunks]
        kc = [x.astype(BF16) for x in kf]
        kb = [kf[c] * sc[c][0] for c in chunks]
        dec = [dec_ref[0, c] for c in chunks]
        a = [_off_diagonal(_dot(kb[c].astype(BF16), kc[c], NT) * dec[c]) for c in chunks]
        t = _inv_unit_triangular(a)
        for c in chunks:
            tb = t[c].astype(BF16)
            t_ref[0, c] = t[c]
            u_ref[0, c] = _dot(tb, (v_ref[0, c] * sc[c][0][:, :dv]).astype(BF16), NN)
            w_ref[0, c] = _dot(tb, (kb[c] * sc[c][1]).astype(BF16), NN)
            qk_ref[0, c] = _dot(_bf(q_ref, 0, c), kc[c], NT) * dec[c]

    ins = (q, k, v, decay, scal)
    outs = (jax.ShapeDtypeStruct(v.shape, F32), jax.ShapeDtypeStruct(v.shape[:3] + (dk,), F32),
            jax.ShapeDtypeStruct(decay.shape, F32), jax.ShapeDtypeStruct(decay.shape, F32))
    return _prep_call(body, ins, outs, "gdn_prep_fwd")


def _gdn_prep_bwd(q, k, v, decay, scal, t, du, dw, dqk):
    dk = k.shape[-1]

    def body(nb, q_ref, k_ref, v_ref, dec_ref, sc_ref, t_ref, du_ref, dw_ref, dqk_ref,
             dq_ref, dk_ref, dv_ref, ddec_ref, dsc_ref):
        chunks = range(nb)
        sc = [_scalars(sc_ref, 0, c, dk, (BETA, EG)) for c in chunks]
        beta = [s[0] for s in sc]
        eg = [s[1] for s in sc]
        kf = [k_ref[0, c] for c in chunks]
        vf = [v_ref[0, c] for c in chunks]
        kc = [x.astype(BF16) for x in kf]
        qc = [_bf(q_ref, 0, c) for c in chunks]
        kb = [kf[c] * beta[c] for c in chunks]
        kbb = [x.astype(BF16) for x in kb]
        tb = [_bf(t_ref, 0, c) for c in chunks]
        dub = [_bf(du_ref, 0, c) for c in chunks]
        dwb = [_bf(dw_ref, 0, c) for c in chunks]
        dvb = [_dot(tb[c], dub[c], TN) for c in chunks]
        dkbg = [_dot(tb[c], dwb[c], TN) for c in chunks]
        dt = [_dot(dub[c], (vf[c] * beta[c]).astype(BF16), NT) + _dot(dwb[c], (kb[c] * eg[c]).astype(BF16), NT)
              for c in chunks]
        left = [_dot(tb[c], dt[c].astype(BF16), TN).astype(BF16) for c in chunks]
        da = [-_off_diagonal(_dot(left[c], tb[c], NT)) for c in chunks]
        dqk = [dqk_ref[0, c] for c in chunks]
        for c in chunks:
            ddec_ref[0, c] = da[c] * _dot(kbb[c], kc[c], NT) + dqk[c] * _dot(qc[c], kc[c], NT)
        dkk = [(da[c] * dec_ref[0, c]).astype(BF16) for c in chunks]
        dqkd = [(dqk[c] * dec_ref[0, c]).astype(BF16) for c in chunks]
        dkb = [_dot(dkk[c], kc[c], NN) + dkbg[c] * eg[c] for c in chunks]
        for c in chunks:
            dq_ref[0, c] = _dot(dqkd[c], kc[c], NN)
            dk_ref[0, c] = _dot(dkk[c], kbb[c], TN) + _dot(dqkd[c], qc[c], TN) + dkb[c] * beta[c]
            dv_ref[0, c] = dvb[c] * beta[c]
            dsc_ref[0, c] = _scalar_rows([_token_sums(dkb[c] * kf[c] + dvb[c] * vf[c]),
                                          _token_sums(dkbg[c] * kb[c])])

    ins = (q, k, v, decay, scal, t, du, dw, dqk)
    heads = v.shape[0]
    outs = (jax.ShapeDtypeStruct((heads,) + q.shape[1:], F32), jax.ShapeDtypeStruct((heads,) + k.shape[1:], F32),
            jax.ShapeDtypeStruct(v.shape, F32), jax.ShapeDtypeStruct(decay.shape, F32),
            jax.ShapeDtypeStruct(scal.shape, F32))
    return _prep_call(body, ins, outs, "gdn_prep_bwd")


def _add_shared_heads(d, key_heads):
    return d.reshape((key_heads, d.shape[0] // key_heads) + d.shape[1:]).sum(axis=1)


@jax.custom_vjp
def _gdn_prep(q, k, v, decay, scal):
    return _gdn_prep_fwd(q, k, v, decay, scal)[:3]


def _gdn_prep_vjp_fwd(q, k, v, decay, scal):
    u, w, qk, t = _gdn_prep_fwd(q, k, v, decay, scal)
    return (u, w, qk), (q, k, v, decay, scal, t)


def _gdn_prep_vjp_bwd(res, cot):
    dq, dk, dv, ddec, dsc = _gdn_prep_bwd(*res, *cot)
    return _add_shared_heads(dq, res[0].shape[0]), _add_shared_heads(dk, res[1].shape[0]), dv, ddec, dsc


_gdn_prep.defvjp(_gdn_prep_vjp_fwd, _gdn_prep_vjp_bwd)


def _loss_head(y, target):
    rows, d = y.shape
    tr = _tile(rows, 256, SUBLANE)

    def body(y_ref, t_ref, dy_ref, sq_ref):
        diff = y_ref[...] - t_ref[...]
        dy_ref[...] = diff * (1.0 / d)
        part = jnp.sum((diff * diff).reshape(tr // SUBLANE, SUBLANE, d), axis=0)

        @pl.when(pl.program_id(0) == 0)
        def _():
            sq_ref[...] = part

        @pl.when(pl.program_id(0) > 0)
        def _():
            sq_ref[...] += part

    row = pl.BlockSpec((tr, d), lambda i: (i, 0))
    return pl.pallas_call(
        body,
        out_shape=(jax.ShapeDtypeStruct((rows, d), F32), jax.ShapeDtypeStruct((SUBLANE, d), F32)),
        grid=(rows // tr,),
        in_specs=[row, row],
        out_specs=[row, pl.BlockSpec((SUBLANE, d), lambda i: (0, 0))],
        compiler_params=_params("arbitrary"),
        name="loss_head",
    )(y, target)


def _add_half(g, r1, my_c):
    nblk, half, cols = r1.shape
    tr = _tile(half, 128, 2 * SUBLANE)
    per_half = half // tr

    def body(c_ref, g_ref, r_ref, o_ref):
        o_ref[...] = (g_ref[...] + r_ref[...]).astype(BF16)

    grid_spec = pltpu.PrefetchScalarGridSpec(
        num_scalar_prefetch=1,
        grid=(nblk, per_half),
        in_specs=[
            pl.BlockSpec((1, tr, cols), lambda b, i, c_ref: (b, c_ref[0] * per_half + i, 0)),
            pl.BlockSpec((1, tr, cols), lambda b, i, c_ref: (b, i, 0)),
        ],
        out_specs=pl.BlockSpec((1, tr, cols), lambda b, i, c_ref: (b, i, 0)),
    )
    return pl.pallas_call(
        body,
        out_shape=jax.ShapeDtypeStruct(r1.shape, BF16),
        grid_spec=grid_spec,
        compiler_params=_params("parallel", "parallel"),
        name="grad_add_half",
    )(my_c.reshape(1).astype(jnp.int32), g, r1)


def _sum_chips(own, slots, place):
    _, half, cols = own.shape
    tr = _tile(half, 128, 2 * SUBLANE)
    per_half = half // tr

    def body(p_ref, a_ref, b_ref, c_ref, d_ref, o_ref):
        f32 = lambda ref: ref[0].astype(F32)
        o_ref[...] = (f32(a_ref) + f32(b_ref)) + (f32(c_ref) + f32(d_ref))

    def block(k):
        return pl.BlockSpec((1, tr, cols), lambda i, p_ref: (p_ref[k], i, 0))

    grid_spec = pltpu.PrefetchScalarGridSpec(
        num_scalar_prefetch=1,
        grid=(per_half,),
        in_specs=[block(0), block(1), block(2), block(3)],
        out_specs=pl.BlockSpec((tr, cols), lambda i, p_ref: (p_ref[4] * per_half + i, 0)),
    )
    return pl.pallas_call(
        body,
        out_shape=jax.ShapeDtypeStruct((2 * half, cols), F32),
        grid_spec=grid_spec,
        compiler_params=_params("parallel"),
        name="grad_sum_chips",
    )(place, own, slots, slots, slots)


def _cast_into_block(w, chip):
    rows, cols = w.shape
    tr = _tile(rows, 128, 2 * SUBLANE)

    def body(j_ref, w_ref, o_ref):
        o_ref[0] = w_ref[...].astype(BF16)

    grid_spec = pltpu.PrefetchScalarGridSpec(
        num_scalar_prefetch=1,
        grid=(rows // tr,),
        in_specs=[pl.BlockSpec((tr, cols), lambda i, j_ref: (i, 0))],
        out_specs=pl.BlockSpec((1, tr, cols), lambda i, j_ref: (j_ref[0], i, 0)),
    )
    return pl.pallas_call(
        body,
        out_shape=jax.ShapeDtypeStruct((N_CHIP, rows, cols), BF16),
        grid_spec=grid_spec,
        compiler_params=_params("parallel"),
        name="cast_weight_shard",
    )(chip.reshape(1).astype(jnp.int32), w)


def _sum_leading(x, name):
    k, rows, cols = x.shape
    tr = _tile(rows, 128, SUBLANE)

    def body(x_ref, o_ref):
        parts = [x_ref[i] for i in range(k)]
        while len(parts) > 1:
            parts = [parts[i] + parts[i + 1] for i in range(0, len(parts), 2)]
        o_ref[...] = parts[0]

    return pl.pallas_call(
        body,
        out_shape=jax.ShapeDtypeStruct((rows, cols), F32),
        grid=(rows // tr,),
        in_specs=[pl.BlockSpec((k, tr, cols), lambda i: (0, i, 0))],
        out_specs=pl.BlockSpec((tr, cols), lambda i: (i, 0)),
        compiler_params=_params("parallel"),
        name=name,
    )(x)


def _adamw(w, g, m, v, name):
    rows, cols = w.shape
    tr = _tile(rows, 128, SUBLANE)
    m_scale = 1.0 / (1.0 - ADAM_B1 ** ADAM_STEP)
    v_scale = 1.0 / (1.0 - ADAM_B2 ** ADAM_STEP)

    def body(w_ref, g_ref, m_ref, v_ref, d_ref, nm_ref, nv_ref):
        gg = g_ref[...]
        nm = ADAM_B1 * m_ref[...] + (1.0 - ADAM_B1) * gg
        nv = ADAM_B2 * v_ref[...] + (1.0 - ADAM_B2) * (gg * gg)
        nm_ref[...] = nm
        nv_ref[...] = nv
        d_ref[...] = -ADAM_LR * ((nm * m_scale) / (jnp.sqrt(nv * v_scale) + ADAM_EPS) + ADAM_WD * w_ref[...])

    spec = pl.BlockSpec((tr, cols), lambda i: (i, 0))
    out = jax.ShapeDtypeStruct((rows, cols), F32)
    return pl.pallas_call(
        body,
        out_shape=(out, out, out),
        grid=(rows // tr,),
        in_specs=[spec] * 4,
        out_specs=[spec] * 3,
        compiler_params=_params("parallel"),
        name=name,
    )(w, g, m, v)


def _ada_fwd(a16, w):
    layers, d, n = w.shape
    tn = _tile(n, 512)

    def body(a_ref, w_ref, o_ref):
        o_ref[0] = _dot(a_ref[...].astype(BF16), w_ref[0].astype(BF16), NN)

    return pl.pallas_call(
        body,
        out_shape=jax.ShapeDtypeStruct((layers, a16.shape[0], n), F32),
        grid=(layers, n // tn),
        in_specs=[pl.BlockSpec(a16.shape, lambda l, j: (0, 0)), pl.BlockSpec((1, d, tn), lambda l, j: (l, 0, j))],
        out_specs=pl.BlockSpec((1, a16.shape[0], tn), lambda l, j: (l, 0, j)),
        compiler_params=_params("parallel", "parallel"),
        name="ada_fwd",
    )(a16, w)


def _ada_bwd(a16, dm, w):
    layers, d, n = w.shape
    tn = _tile(n, 512)
    rows = a16.shape[0]

    def body(a_ref, dm_ref, w_ref, gw_ref, ga_ref):
        dmb = dm_ref[0].astype(BF16)
        gw_ref[0] = _dot(a_ref[...].astype(BF16), dmb, TN)
        part = _dot(dmb, w_ref[0].astype(BF16), NT)
        first = (pl.program_id(0) == 0) & (pl.program_id(1) == 0)

        @pl.when(first)
        def _():
            ga_ref[...] = part

        @pl.when(jnp.logical_not(first))
        def _():
            ga_ref[...] += part

    return pl.pallas_call(
        body,
        out_shape=(jax.ShapeDtypeStruct(w.shape, F32), jax.ShapeDtypeStruct((rows, d), F32)),
        grid=(layers, n // tn),
        in_specs=[
            pl.BlockSpec((rows, d), lambda l, j: (0, 0)),
            pl.BlockSpec((1, rows, tn), lambda l, j: (l, 0, j)),
            pl.BlockSpec((1, d, tn), lambda l, j: (l, 0, j)),
        ],
        out_specs=[pl.BlockSpec((1, d, tn), lambda l, j: (l, 0, j)), pl.BlockSpec((rows, d), lambda l, j: (0, 0))],
        compiler_params=_params("arbitrary", "arbitrary"),
        name="ada_bwd",
    )(a16, dm, w)


def _place():
    ix, iy, ic = lax.axis_index("x"), lax.axis_index("y"), lax.axis_index("c")
    return ix, iy, ic


def _flip(coord, bit):
    return 1 - coord if bit else coord


def _remote(src, dst, send_sem, recv_sem, to):
    return pltpu.make_async_remote_copy(
        src_ref=src, dst_ref=dst, send_sem=send_sem, recv_sem=recv_sem, device_id=to, device_id_type=MESH)


def _all_gather8(x, name):
    rows, cols = x.shape

    def body(x_ref, o_ref, send_sems, recv_sems):
        ix, iy, ic = _place()
        me = 4 * ix + 2 * iy + ic
        o_ref[me] = x_ref[...]
        peers = []
        for p in range(1, N_DEV):
            to = (_flip(ix, p & 4), _flip(iy, p & 2), _flip(ic, p & 1))
            peers.append((to, 4 * to[0] + 2 * to[1] + to[2]))
        sends = [_remote(x_ref, o_ref.at[me], send_sems.at[p], recv_sems.at[p], to)
                 for p, (to, _) in enumerate(peers)]
        for cp in sends:
            cp.start()
        for p, (to, slot) in enumerate(peers):
            _remote(x_ref, o_ref.at[slot], send_sems.at[p], recv_sems.at[p], to).wait_recv()
        for cp in sends:
            cp.wait_send()

    return pl.pallas_call(
        body,
        out_shape=jax.ShapeDtypeStruct((N_DEV, rows, cols), F32),
        in_specs=[pl.BlockSpec(memory_space=pltpu.VMEM)],
        out_specs=pl.BlockSpec(memory_space=pltpu.VMEM),
        scratch_shapes=[pltpu.SemaphoreType.DMA((N_DEV - 1,)), pltpu.SemaphoreType.DMA((N_DEV - 1,))],
        compiler_params=pltpu.CompilerParams(vmem_limit_bytes=VMEM_LIMIT),
        name=name,
    )(x)


def _other_chips(ix, iy):
    chips = [(1 - ix, iy), (ix, 1 - iy), (1 - ix, 1 - iy)]
    return [(cx, cy, 2 * cx + cy) for cx, cy in chips]


def _hbm_call(body, ins, out_shapes, scratch, name, in_place=False):
    hbm = pl.BlockSpec(memory_space=pltpu.HBM)
    return pl.pallas_call(
        body,
        out_shape=out_shapes,
        in_specs=[hbm] * len(ins),
        out_specs=[hbm] * len(out_shapes),
        scratch_shapes=scratch,
        input_output_aliases={t: t for t in range(len(ins))} if in_place else {},
        compiler_params=pltpu.CompilerParams(vmem_limit_bytes=VMEM_LIMIT),
        name=name,
    )(*ins)


def _gather_weights(blocks):
    nt = len(blocks)

    def body(*refs):
        ins, outs = refs[:nt], refs[nt:2 * nt]
        ici_send, ici_recv, d2d_send, d2d_recv = refs[2 * nt:]
        ix, iy, ic = _place()
        j = 2 * ix + iy
        sibling = (ix, iy, 1 - ic)
        chips = _other_chips(ix, iy)
        sends, waits = [], []
        for t in range(nt):
            half = ins[t].shape[1] // 2
            mine = pl.ds(ic * half, half)
            for q, (cx, cy, _) in enumerate(chips):
                cp = _remote(ins[t].at[j, mine], outs[t].at[j, mine], ici_send.at[3 * t + q], ici_recv.at[3 * t + q],
                             (cx, cy, ic))
                cp.start()
                sends.append(cp)
        for q, (cx, cy, jq) in enumerate(chips):
            for t in range(nt):
                half = ins[t].shape[1] // 2
                mine = pl.ds(ic * half, half)
                theirs = pl.ds((1 - ic) * half, half)
                k = 3 * t + q
                _remote(ins[t].at[j, mine], outs[t].at[jq, mine], ici_send.at[k], ici_recv.at[k],
                        (cx, cy, ic)).wait_recv()
                cp = _remote(outs[t].at[jq, mine], outs[t].at[jq, mine], d2d_send.at[k], d2d_recv.at[k], sibling)
                cp.start()
                sends.append(cp)
                waits.append(_remote(outs[t].at[jq, theirs], outs[t].at[jq, theirs], d2d_send.at[k], d2d_recv.at[k],
                                     sibling))
        for cp in waits:
            cp.wait_recv()
        for cp in sends:
            cp.wait_send()

    dma = pltpu.SemaphoreType.DMA
    outs = tuple(jax.ShapeDtypeStruct(b.shape, b.dtype) for b in blocks)
    return _hbm_call(body, blocks, outs, [dma((3 * nt,))] * 4, "gather_weights", in_place=True)


def _exchange_halves(grads):
    nt = len(grads)

    def body(*refs):
        ins, outs = refs[:nt], refs[nt:2 * nt]
        send_sems, recv_sems = refs[2 * nt:]
        ix, iy, ic = _place()
        sibling = (ix, iy, 1 - ic)
        copies = []
        for t in range(nt):
            half = ins[t].shape[1] // 2
            theirs = pl.ds((1 - ic) * half, half)
            cp = _remote(ins[t].at[:, theirs, :], outs[t], send_sems.at[t], recv_sems.at[t], sibling)
            cp.start()
            copies.append(cp)
        for cp in copies:
            cp.wait_recv()
        for cp in copies:
            cp.wait_send()

    dma = pltpu.SemaphoreType.DMA
    outs = tuple(jax.ShapeDtypeStruct((g.shape[0], g.shape[1] // 2, g.shape[2]), F32) for g in grads)
    return _hbm_call(body, grads, outs, [dma((nt,)), dma((nt,))], "grad_exchange_halves")


def _exchange_blocks(parts):
    nt = len(parts)

    def body(*refs):
        ins, outs = refs[:nt], refs[nt:2 * nt]
        send_sems, recv_sems = refs[2 * nt:]
        ix, iy, ic = _place()
        j = 2 * ix + iy
        chips = _other_chips(ix, iy)
        sends, waits = [], []
        for t in range(nt):
            for q, (cx, cy, jq) in enumerate(chips):
                k = 3 * t + q
                cp = _remote(ins[t].at[jq], outs[t].at[j], send_sems.at[k], recv_sems.at[k], (cx, cy, ic))
                cp.start()
                sends.append(cp)
                waits.append(_remote(ins[t].at[jq], outs[t].at[jq], send_sems.at[k], recv_sems.at[k], (cx, cy, ic)))
        for cp in waits:
            cp.wait_recv()
        for cp in sends:
            cp.wait_send()

    dma = pltpu.SemaphoreType.DMA
    outs = tuple(jax.ShapeDtypeStruct(p.shape, p.dtype) for p in parts)
    return _hbm_call(body, parts, outs, [dma((3 * nt,)), dma((3 * nt,))], "grad_exchange_blocks")


def _share_halves(blocks):
    nt = len(blocks)

    def body(*refs):
        ins, outs = refs[:nt], refs[nt:2 * nt]
        send_sems, recv_sems = refs[2 * nt:]
        ix, iy, ic = _place()
        sibling = (ix, iy, 1 - ic)
        sends, waits = [], []
        for t in range(nt):
            half = ins[t].shape[0] // 2
            mine = pl.ds(ic * half, half)
            theirs = pl.ds((1 - ic) * half, half)
            cp = _remote(ins[t].at[mine], outs[t].at[mine], send_sems.at[t], recv_sems.at[t], sibling)
            cp.start()
            sends.append(cp)
            waits.append(_remote(ins[t].at[theirs], outs[t].at[theirs], send_sems.at[t], recv_sems.at[t], sibling))
        for cp in waits:
            cp.wait_recv()
        for cp in sends:
            cp.wait_send()

    dma = pltpu.SemaphoreType.DMA
    outs = tuple(jax.ShapeDtypeStruct(b.shape, F32) for b in blocks)
    return _hbm_call(body, blocks, outs, [dma((nt,)), dma((nt,))], "grad_share_halves", in_place=True)


def _reduce_grads(grads, ix, iy, ic):
    recv = _exchange_halves(grads)
    chip_sums = [_add_half(g, r1, ic) for g, r1 in zip(grads, recv)]
    slots = _exchange_blocks(chip_sums)
    place = jnp.stack([2 * ix + iy] + [jq for _, _, jq in _other_chips(ix, iy)] + [ic]).astype(jnp.int32)
    return _share_halves([_sum_chips(own, got, place) for own, got in zip(chip_sums, slots)])


def _rmsnorm(x, w):
    return x * lax.rsqrt(jnp.mean(x * x, axis=-1, keepdims=True) + EPS) * w


def _l2norm(x):
    return x * lax.rsqrt(jnp.sum(x * x, axis=-1, keepdims=True) + EPS)


def _heads(t, d):
    return t.reshape(t.shape[:-1] + (t.shape[-1] // d, d))


def _chunks(t, chunk):
    n = t.shape[0] // chunk
    return t.reshape(n, chunk, t.shape[1], t.shape[2]).transpose(2, 0, 1, 3)


def _unchunks(t):
    hh, n, chunk, d = t.shape
    return t.transpose(1, 2, 0, 3).reshape(n * chunk, hh, d)


def _triangle(size, anti):
    ones = jnp.ones((size, size), bool)
    return jnp.triu(ones) if anti else jnp.tril(ones)


def _hgrn2_scan(q, k, v, log_f, n_ctx, anti):
    q, k, v, log_f = (_chunks(t, A_CHUNK) for t in (q, k, v, log_f))
    b = lax.cumsum(log_f, axis=2, reverse=anti)
    end = 0 if anti else A_CHUNK - 1
    mid = A_CHUNK // 2 if anti else A_CHUNK // 2 - 1
    b_last = b[:, :, end:end + 1, :]
    b_mid = b[:, :, mid:mid + 1, :]
    return _unchunks(_hgrn_rec(anti, n_ctx // A_CHUNK, q * jnp.exp(b), k * jnp.exp(b_last - b), v, jnp.exp(b_last),
                               q * jnp.exp(b - b_mid), k * jnp.exp(b_mid - b)))


def _gdn_scan(q, k, v, g, beta, n_ctx, anti):
    g = _chunks(g[..., None], C_CHUNK)[..., 0]
    beta = _chunks(beta[..., None], C_CHUNK)[..., 0]
    gc = lax.cumsum(g, axis=2, reverse=anti)
    end = 0 if anti else C_CHUNK - 1
    causal = _triangle(C_CHUNK, anti)
    diff = gc[..., :, None] - gc[..., None, :]
    decay = jnp.where(causal, jnp.exp(jnp.where(causal, diff, 0.0)), 0.0)
    rows = [beta, jnp.exp(gc), jnp.exp(gc[..., end:end + 1] - gc)]
    scal = jnp.stack(rows + [jnp.zeros_like(beta)] * (SUBLANE - len(rows)), axis=2)
    u, w, qk = _gdn_prep(q, k, v, decay, scal)
    g_last = jnp.exp(gc[..., end])
    gl = jnp.broadcast_to(g_last[..., None, None], g_last.shape + (1, q.shape[-1]))
    return _unchunks(_gdn_rec(anti, n_ctx // C_CHUNK, u, w, qk, q, k, scal, gl))


def _bidirectional(scan_fn, fwd_args, bwd_args, n_ctx):
    return scan_fn(*fwd_args, n_ctx, False) + scan_fn(*bwd_args, n_ctx, True)


def _multiscale_pool(u, pool_w, pool_scale):
    rows, length, width = u.shape
    groups = len(POOL_WINDOWS)
    gdim = width // groups
    uf = u.reshape(rows, length, groups, gdim)
    cs = jnp.concatenate([jnp.zeros_like(uf[:, :1]), jnp.cumsum(uf, axis=1)], axis=1)
    pos = np.arange(length)
    mixed = []
    for gi, win in enumerate(POOL_WINDOWS):
        below, above = win // 2, win - win // 2
        lo = np.clip(pos - below, 0, length - 1)
        hi = np.clip(pos + win - 1 - below, 0, length - 1)
        cnt = jnp.asarray((hi - lo + 1).astype(np.float32))[None, :, None]
        csg = cs[:, :, gi, :]
        upper = jnp.concatenate([csg[:, above:]] + [csg[:, length:]] * (above - 1), axis=1)
        lower = jnp.concatenate([jnp.zeros_like(csg[:, :below]), csg[:, :length - below]], axis=1)
        mixed.append((upper - lower) / cnt - uf[:, :, gi, :])
    d = jnp.stack(mixed, axis=2)
    y = jnp.einsum("rlgc,gcd->rlgd", d, pool_w)
    return y.reshape(rows, length, width) * pool_scale


CONV_LEFT = C_CONV // 2
CONV_PAD = SUBLANE


def _conv_window_taps(win, r0, rows, n_ctx, reverse):
    tg = r0 + lax.broadcasted_iota(jnp.int32, (rows, win.shape[1]), 0)
    taps = []
    for j in range(C_CONV):
        off = (CONV_LEFT - j) if reverse else (j - CONV_LEFT)
        if off == 0:
            taps.append(win[CONV_PAD:CONV_PAD + rows])
            continue
        moved = pltpu.roll(win, (-off) % win.shape[0], 0)[CONV_PAD:CONV_PAD + rows]
        lo, hi = (n_ctx, n_ctx - off) if off < 0 else (n_ctx - off, n_ctx)
        taps.append(jnp.where(tg >= lo, jnp.where(tg < hi, 0.0, moved), moved))
    return taps


def _fill_padded(pad, src, total):
    zeros = jnp.zeros((CONV_PAD, pad.shape[1]), F32)
    pad[0:CONV_PAD] = zeros
    pad[total + CONV_PAD:total + 2 * CONV_PAD] = zeros
    if src is not None:
        pad[CONV_PAD:total + CONV_PAD] = src[...]


def _conv_call(body, ins, outs, out_blocks, scratch, name):
    total, width = ins[0].shape
    tc = LANE
    col = pl.BlockSpec((total, tc), lambda j: (0, j))
    return pl.pallas_call(
        body,
        out_shape=outs,
        grid=(width // tc,),
        in_specs=[col if t.shape[0] == total else pl.BlockSpec((t.shape[0], tc), lambda j: (0, j)) for t in ins],
        out_specs=[pl.BlockSpec((rows, tc), lambda j: (0, j)) for rows in out_blocks],
        scratch_shapes=[pltpu.VMEM((total + 2 * CONV_PAD, tc), F32)] * scratch,
        compiler_params=_params("parallel"),
        name=name,
    )(*ins)


def _conv_silu_fwd(n_ctx, x, w):
    total = x.shape[0]
    rows = _tile(total, 128, SUBLANE)

    def body(x_ref, w_ref, o_ref, pad):
        _fill_padded(pad, x_ref, total)
        wt = w_ref[...]

        def chunk(i, carry):
            r0 = pl.multiple_of(i * rows, SUBLANE)
            taps = _conv_window_taps(pad[pl.ds(r0, rows + 2 * CONV_PAD), :], r0, rows, n_ctx, False)
            pre = sum(t * wt[j:j + 1] for j, t in enumerate(taps))
            o_ref[pl.ds(r0, rows), :] = pre / (1.0 + jnp.exp(-pre))
            return carry

        lax.fori_loop(0, total // rows, chunk, 0)

    return _conv_call(body, (x, w), (jax.ShapeDtypeStruct(x.shape, F32),), (total,), 1, "conv_silu_fwd")[0]


def _conv_silu_bwd(n_ctx, x, w, dact):
    total = x.shape[0]
    rows = _tile(total, 128, SUBLANE)

    def body(x_ref, w_ref, da_ref, dx_ref, dw_ref, xpad, gpad):
        _fill_padded(xpad, x_ref, total)
        _fill_padded(gpad, None, total)
        wt = w_ref[...]
        dw_ref[...] = jnp.zeros_like(dw_ref)

        def grad_pre(i, carry):
            r0 = pl.multiple_of(i * rows, SUBLANE)
            taps = _conv_window_taps(xpad[pl.ds(r0, rows + 2 * CONV_PAD), :], r0, rows, n_ctx, False)
            pre = sum(t * wt[j:j + 1] for j, t in enumerate(taps))
            sig = 1.0 / (1.0 + jnp.exp(-pre))
            dpre = da_ref[pl.ds(r0, rows), :] * (sig * (1.0 + pre * (1.0 - sig)))
            gpad[pl.ds(r0 + CONV_PAD, rows), :] = dpre
            for j, t in enumerate(taps):
                part = jnp.sum((dpre * t).reshape(rows // SUBLANE, SUBLANE, t.shape[1]), axis=0)
                dw_ref[SUBLANE * j:SUBLANE * (j + 1)] += part
            return carry

        lax.fori_loop(0, total // rows, grad_pre, 0)

        def grad_x(i, carry):
            r0 = pl.multiple_of(i * rows, SUBLANE)
            taps = _conv_window_taps(gpad[pl.ds(r0, rows + 2 * CONV_PAD), :], r0, rows, n_ctx, True)
            dx_ref[pl.ds(r0, rows), :] = sum(t * wt[j:j + 1] for j, t in enumerate(taps))
            return carry

        lax.fori_loop(0, total // rows, grad_x, 0)

    outs = (jax.ShapeDtypeStruct(x.shape, F32), jax.ShapeDtypeStruct((C_CONV * SUBLANE, x.shape[1]), F32))
    return _conv_call(body, (x, w, dact), outs, (total, C_CONV * SUBLANE), 2, "conv_silu_bwd")


@functools.partial(jax.custom_vjp, nondiff_argnums=(0,))
def _conv_silu(n_ctx, x, w):
    return _conv_silu_fwd(n_ctx, x, w)


def _conv_silu_vjp_fwd(n_ctx, x, w):
    return _conv_silu_fwd(n_ctx, x, w), (x, w)


def _conv_silu_vjp_bwd(n_ctx, res, dact):
    x, w = res
    dx, dw = _conv_silu_bwd(n_ctx, x, w, dact)
    return dx, dw.reshape(C_CONV, SUBLANE, -1).sum(axis=1)


_conv_silu.defvjp(_conv_silu_vjp_fwd, _conv_silu_vjp_bwd)


def _hgrn2_gates(pre_f, lb):
    log_f = jnp.log(lb + (1.0 - lb) * jax.nn.sigmoid(pre_f))
    k = (1.0 - lb) * jax.nn.sigmoid(-pre_f)
    return _heads(k, A_HEAD_DIM), _heads(log_f, A_HEAD_DIM)


def _even_mixer(h, n_ctx, lb, w_in, w_in_hook, a_norm, pool_w, pool_scale, w_out, w_out_hook):
    a_width = w_out.shape[0] // 2
    p = _dense(h, w_in, w_in_hook)
    q, f_f, f_b, i, g, u = jnp.split(p, [a_width * s for s in range(1, 6)], axis=-1)
    q = _heads(jax.nn.silu(q), A_HEAD_DIM)
    i = _heads(i, A_HEAD_DIM)
    k_f, logf_f = _hgrn2_gates(f_f, lb[0])
    k_b, logf_b = _hgrn2_gates(f_b, lb[1])
    o = _bidirectional(_hgrn2_scan, (q, k_f, i, logf_f), (q, k_b, i, logf_b), n_ctx)
    b_width = u.shape[-1]
    u_l = u[n_ctx:].reshape(-1, GRID_W, b_width)
    pooled = jnp.concatenate([
        _multiscale_pool(u[None, :n_ctx], pool_w, pool_scale)[0],
        _multiscale_pool(u_l, pool_w, pool_scale).reshape(-1, b_width),
    ], axis=0)
    a_out = _rmsnorm(o, a_norm) * jax.nn.silu(_heads(g, A_HEAD_DIM))
    a_out = a_out.reshape(a_out.shape[0], a_width)
    return _dense(jnp.concatenate([a_out, pooled], axis=-1), w_out, w_out_hook)


def _odd_mixer(h, n_ctx, big, hooks, conv_w, a_log, dt_bias, norm_w):
    w_out, w_out_hook = big["od_w_out"], hooks["od_w_out"]
    value_width = w_out.shape[0]
    key_width = value_width // 2
    a_rate = jnp.exp(a_log)
    z = _dense(h, big["od_w_z"], hooks["od_w_z"])
    gates = _dense(h, big["od_w_gate"], hooks["od_w_gate"])
    qkv = _conv_silu(n_ctx, _dense(h, big["od_w_qkv"], hooks["od_w_qkv"]), conv_w)
    q, k, v = jnp.split(qkv, [key_width, 2 * key_width], axis=-1)
    q = _chunks(_l2norm(_heads(q, C_HEAD_DIM)) * C_HEAD_DIM ** -0.5, C_CHUNK)
    k = _chunks(_l2norm(_heads(k, C_HEAD_DIM)), C_CHUNK)
    v = _chunks(_heads(v, C_HEAD_DIM), C_CHUNK)
    a_f, a_b, b_f, b_b = jnp.split(gates, 4, axis=-1)
    g_f = -a_rate[0] * jax.nn.softplus(a_f + dt_bias[0])
    g_b = -a_rate[1] * jax.nn.softplus(a_b + dt_bias[1])
    o = _bidirectional(_gdn_scan, (q, k, v, g_f, jax.nn.sigmoid(b_f)), (q, k, v, g_b, jax.nn.sigmoid(b_b)), n_ctx)
    y = _rmsnorm(o, norm_w) * jax.nn.silu(_heads(z, C_HEAD_DIM))
    return _dense(y.reshape(y.shape[0], value_width), w_out, w_out_hook)


def _swiglu(h, w13, w13_hook, w2, w2_hook):
    gate, up = jnp.split(_dense(h, w13, w13_hook), 2, axis=-1)
    return _dense(jax.nn.silu(gate) * up, w2, w2_hook)


def _forward(x, ctx, mods, big, hooks, small):
    n_ctx = ctx.shape[0]
    d = x.shape[-1]
    stream = jnp.concatenate([ctx, x], axis=0)
    is_ctx = (jnp.arange(stream.shape[0]) < n_ctx)[:, None]
    lb_all = jnp.cumsum(jax.nn.softmax(small["ev_lb"], axis=1), axis=1)
    for layer in range(2):
        m = [jnp.where(is_ctx, mods[layer, 1, s * d:(s + 1) * d][None, :], mods[layer, 0, s * d:(s + 1) * d][None, :])
             for s in range(6)]
        nw = small["norm_w"][layer]
        h = _rmsnorm(stream, nw[0]) * (1.0 + m[1]) + m[0]
        if layer == 0:
            y = _even_mixer(h, n_ctx, lb_all[:, layer], big["ev_w_in"], hooks["ev_w_in"], small["ev_a_norm"][0],
                            small["ev_pool_w"][0], small["ev_pool_scale"][0], big["ev_w_out"], hooks["ev_w_out"])
        else:
            y = _odd_mixer(h, n_ctx, big, hooks, small["od_conv"][0], small["od_A_log"][0], small["od_dt_bias"][0],
                           small["od_norm"][0])
        stream = stream + m[2] * _rmsnorm(y, nw[1])
        h = _rmsnorm(stream, nw[2]) * (1.0 + m[4]) + m[3]
        f = _swiglu(h, big["ffn_w13_%d" % layer], hooks["ffn_w13_%d" % layer],
                    big["ffn_w2_%d" % layer], hooks["ffn_w2_%d" % layer])
        stream = stream + m[5] * _rmsnorm(f, nw[3])
    return stream[n_ctx:]


def _pack(arrays):
    flat = jnp.concatenate([a.reshape(-1) for a in arrays])
    pad = (-flat.shape[0]) % (SUBLANE * LANE)
    return jnp.pad(flat, (0, pad)).reshape(-1, LANE)


def _unpack(packed, shapes, lead=()):
    flat = packed.reshape(lead + (-1,))
    out, at = [], 0
    for shape in shapes:
        size = int(np.prod(shape))
        out.append(flat[..., at:at + size].reshape(lead + tuple(shape)))
        at += size
    return out


def _from_chips(gathered, axis):
    return jnp.concatenate([gathered[2 * j] for j in range(N_CHIP)], axis=axis)


def _col_natural(blocks):
    nblk, k, n = blocks.shape
    return blocks.transpose(1, 0, 2).reshape(k, nblk * n)


def _col_blocked(nat):
    k, n4 = nat.shape
    return nat.reshape(k, N_CHIP, n4 // N_CHIP).transpose(1, 0, 2)


def kernel(x, c, ctx, c_ctx, w_ada, b_ada, norm_w, ev_w_in, ev_lb, ev_a_norm, ev_pool_w, ev_pool_scale, ev_w_out, od_w_in, od_conv, od_A_log, od_dt_bias, od_norm, od_w_out, ffn_w13, ffn_w2, loss_target, m_c_ctx, m_w_ada, m_b_ada, m_norm_w, m_ev_w_in, m_ev_lb, m_ev_a_norm, m_ev_pool_w, m_ev_pool_scale, m_ev_w_out, m_od_w_in, m_od_conv, m_od_A_log, m_od_dt_bias, m_od_norm, m_od_w_out, m_ffn_w13, m_ffn_w2, v_c_ctx, v_w_ada, v_b_ada, v_norm_w, v_ev_w_in, v_ev_lb, v_ev_a_norm, v_ev_pool_w, v_ev_pool_scale, v_ev_w_out, v_od_w_in, v_od_conv, v_od_A_log, v_od_dt_bias, v_od_norm, v_od_w_out, v_ffn_w13, v_ffn_w2):
    ix, iy, ic = _place()
    me = 4 * ix + 2 * iy + ic
    chip = 2 * ix + iy
    d = x.shape[-1]
    layers = w_ada.shape[0]
    n_ada = w_ada.shape[-1]

    pre_parts = [c[0], norm_w, ev_lb, ev_pool_w, od_conv]
    pre = _all_gather8(_pack(pre_parts), "gather_small_inputs")
    c_all, norm_w_g, ev_lb_g, pool_w_g, od_conv_g = _unpack(pre, [p.shape for p in pre_parts], lead=(N_DEV,))
    small = {
        "norm_w": _from_chips(norm_w_g, 2),
        "ev_lb": _from_chips(ev_lb_g, 2),
        "ev_a_norm": ev_a_norm,
        "ev_pool_w": _from_chips(pool_w_g, 2),
        "ev_pool_scale": ev_pool_scale,
        "od_conv": _from_chips(od_conv_g, 2),
        "od_A_log": od_A_log,
        "od_dt_bias": od_dt_bias,
        "od_norm": od_norm,
    }

    silu_cc, silu_cc_vjp = jax.vjp(jax.nn.silu, c_ctx)
    a16 = jnp.concatenate([jax.nn.silu(c_all), silu_cc[None], jnp.zeros((2 * SUBLANE - N_DEV - 1, d), F32)], axis=0)
    mods_local = _ada_fwd(a16, w_ada)
    mods_g = _all_gather8(mods_local.reshape(-1, n_ada), "gather_modulation")
    mods_g = mods_g.reshape(N_DEV, layers, 2 * SUBLANE, n_ada)
    mods_full = _from_chips(mods_g, 2) + b_ada[:, None, :]
    mods = jnp.stack([lax.dynamic_index_in_dim(mods_full, me, axis=1, keepdims=False), mods_full[:, N_DEV]], axis=1)

    ffn13_rows = ffn_w13.shape[1]
    ffn2_rows = ffn_w2.shape[1]
    shards = [ev_w_in[0], ev_w_out[0], od_w_in[0], od_w_out[0],
              ffn_w13.reshape(-1, ffn_w13.shape[-1]), ffn_w2.reshape(-1, ffn_w2.shape[-1])]
    g_ev_in, g_ev_out, g_od_in, g_od_out, g_w13, g_w2 = _gather_weights([_cast_into_block(s, chip) for s in shards])
    od_nat = _col_natural(g_od_in)
    n_qkv = 2 * od_w_out.shape[1] * N_CHIP
    n_main = n_qkv + n_qkv // 2
    big = {
        "ev_w_in": _col_natural(g_ev_in),
        "ev_w_out": g_ev_out.reshape(-1, g_ev_out.shape[-1]),
        "od_w_qkv": od_nat[:, :n_qkv],
        "od_w_z": od_nat[:, n_qkv:n_main],
        "od_w_gate": od_nat[:, n_main:],
        "od_w_out": g_od_out.reshape(-1, g_od_out.shape[-1]),
    }
    for layer in range(layers):
        big["ffn_w13_%d" % layer] = _col_natural(g_w13[:, layer * ffn13_rows:(layer + 1) * ffn13_rows])
        big["ffn_w2_%d" % layer] = g_w2[:, layer * ffn2_rows:(layer + 1) * ffn2_rows].reshape(-1, g_w2.shape[-1])
    hooks = {name: jnp.zeros(w.shape, F32) for name, w in big.items()}

    def local_forward(x_, mods_, hooks_, small_):
        return _forward(x_, ctx[0], mods_, big, hooks_, small_)

    y, pullback = jax.vjp(local_forward, x[0], mods, hooks, small)
    dy, sq = _loss_head(y, loss_target[0])
    loss = lax.psum(jnp.sum(sq) * (0.5 / d), ("x", "y", "c"))
    grad_x, d_mods, d_big, d_small = pullback(dy)

    d_od_in = jnp.concatenate([d_big["od_w_qkv"], d_big["od_w_z"], d_big["od_w_gate"]], axis=1)
    blocked = [
        _col_blocked(d_big["ev_w_in"]),
        d_big["ev_w_out"].reshape(N_CHIP, -1, d_big["ev_w_out"].shape[-1]),
        _col_blocked(d_od_in),
        d_big["od_w_out"].reshape(N_CHIP, -1, d_big["od_w_out"].shape[-1]),
        jnp.concatenate([_col_blocked(d_big["ffn_w13_%d" % layer]) for layer in range(layers)], axis=1),
        jnp.concatenate([d_big["ffn_w2_%d" % layer].reshape(N_CHIP, -1, d_big["ffn_w2_%d" % layer].shape[-1])
                         for layer in range(layers)], axis=1),
    ]
    r_ev_in, r_ev_out, r_od_in, r_od_out, r_w13, r_w2 = _reduce_grads(blocked, ix, iy, ic)

    small_names = ["norm_w", "ev_lb", "ev_a_norm", "ev_pool_w", "ev_pool_scale", "od_conv", "od_A_log",
                   "od_dt_bias", "od_norm"]
    post_parts = [d_mods[:, 0], d_mods[:, 1]] + [d_small[n] for n in small_names]
    post = _all_gather8(_pack(post_parts), "gather_small_grads")
    post_sum = _sum_leading(post, "sum_small_grads")
    dm_l_all = _unpack(post, [post_parts[0].shape], lead=(N_DEV,))[0]
    summed = _unpack(post_sum, [p.shape for p in post_parts])
    dm_l_sum, dm_c_sum = summed[0], summed[1]
    g_small = dict(zip(small_names, summed[2:]))
    grad_b_ada = dm_l_sum + dm_c_sum

    def my_cols(full, width):
        return lax.dynamic_slice_in_dim(full, chip * width, width, axis=full.ndim - 1)

    dm_rows = jnp.concatenate([
        my_cols(dm_l_all, n_ada).transpose(1, 0, 2),
        my_cols(dm_c_sum, n_ada)[:, None, :],
        jnp.zeros((layers, 2 * SUBLANE - N_DEV - 1, n_ada), F32),
    ], axis=1)
    grad_w_ada, ga = _ada_bwd(a16, dm_rows, w_ada)
    ga_g = _all_gather8(jnp.pad(ga[N_DEV][None], ((0, SUBLANE - 1), (0, 0))), "gather_c_ctx_grad")
    g_silu_cc = _sum_leading(jnp.stack([ga_g[2 * j] for j in range(N_CHIP)]), "sum_c_ctx_grad")[0]
    grad_c_ctx = silu_cc_vjp(g_silu_cc)[0]

    grads = {
        "c_ctx": grad_c_ctx,
        "w_ada": grad_w_ada,
        "b_ada": grad_b_ada,
        "norm_w": my_cols(g_small["norm_w"], norm_w.shape[-1]),
        "ev_w_in": r_ev_in[None],
        "ev_lb": my_cols(g_small["ev_lb"], ev_lb.shape[-1]),
        "ev_a_norm": g_small["ev_a_norm"],
        "ev_pool_w": lax.dynamic_slice_in_dim(g_small["ev_pool_w"], chip * ev_pool_w.shape[2], ev_pool_w.shape[2], axis=2),
        "ev_pool_scale": g_small["ev_pool_scale"],
        "ev_w_out": r_ev_out[None],
        "od_w_in": r_od_in[None],
        "od_conv": my_cols(g_small["od_conv"], od_conv.shape[-1]),
        "od_A_log": g_small["od_A_log"],
        "od_dt_bias": g_small["od_dt_bias"],
        "od_norm": g_small["od_norm"],
        "od_w_out": r_od_out[None],
        "ffn_w13": r_w13.reshape(ffn_w13.shape),
        "ffn_w2": r_w2.reshape(ffn_w2.shape),
    }

    weights = dict(c_ctx=c_ctx, w_ada=w_ada, b_ada=b_ada, norm_w=norm_w, ev_w_in=ev_w_in, ev_lb=ev_lb,
                   ev_a_norm=ev_a_norm, ev_pool_w=ev_pool_w, ev_pool_scale=ev_pool_scale, ev_w_out=ev_w_out,
                   od_w_in=od_w_in, od_conv=od_conv, od_A_log=od_A_log, od_dt_bias=od_dt_bias, od_norm=od_norm,
                   od_w_out=od_w_out, ffn_w13=ffn_w13, ffn_w2=ffn_w2)
    first = dict(c_ctx=m_c_ctx, w_ada=m_w_ada, b_ada=m_b_ada, norm_w=m_norm_w, ev_w_in=m_ev_w_in, ev_lb=m_ev_lb,
                 ev_a_norm=m_ev_a_norm, ev_pool_w=m_ev_pool_w, ev_pool_scale=m_ev_pool_scale, ev_w_out=m_ev_w_out,
                 od_w_in=m_od_w_in, od_conv=m_od_conv, od_A_log=m_od_A_log, od_dt_bias=m_od_dt_bias,
                 od_norm=m_od_norm, od_w_out=m_od_w_out, ffn_w13=m_ffn_w13, ffn_w2=m_ffn_w2)
    second = dict(c_ctx=v_c_ctx, w_ada=v_w_ada, b_ada=v_b_ada, norm_w=v_norm_w, ev_w_in=v_ev_w_in, ev_lb=v_ev_lb,
                  ev_a_norm=v_ev_a_norm, ev_pool_w=v_ev_pool_w, ev_pool_scale=v_ev_pool_scale, ev_w_out=v_ev_w_out,
                  od_w_in=v_od_w_in, od_conv=v_od_conv, od_A_log=v_od_A_log, od_dt_bias=v_od_dt_bias,
                  od_norm=v_od_norm, od_w_out=v_od_w_out, ffn_w13=v_ffn_w13, ffn_w2=v_ffn_w2)
    names = list(weights)
    large = ["w_ada", "ev_w_in", "ev_w_out", "od_w_in", "od_w_out", "ffn_w13", "ffn_w2"]
    little = [n for n in names if n not in large]
    delta, new_m, new_v = {}, {}, {}
    for n in large:
        as2d = lambda t: t.reshape(-1, t.shape[-1])
        out = _adamw(as2d(weights[n]), as2d(grads[n]), as2d(first[n]), as2d(second[n]), "adamw_" + n)
        delta[n], new_m[n], new_v[n] = (t.reshape(weights[n].shape) for t in out)
    shapes = [weights[n].shape for n in little]
    out = _adamw(*(_pack([src[n] for n in little]) for src in (weights, grads, first, second)), "adamw_small")
    for res, packed in zip((delta, new_m, new_v), out):
        res.update(zip(little, _unpack(packed, shapes)))

    return (loss, grad_x[None], *[grads[n] for n in names], *[delta[n] for n in names],
            *[new_m[n] for n in names], *[new_v[n] for n in names])
```

```python
import functools

import numpy as np
import jax
import jax.numpy as jnp
from jax import lax
from jax.experimental import pallas as pl
from jax.experimental.pallas import tpu as pltpu

F32 = jnp.float32
BF16 = jnp.bfloat16
MESH = pl.DeviceIdType.MESH

EPS = 1e-6
GRID_W = 64
A_HEAD_DIM = 128
A_CHUNK = 32
POOL_WINDOWS = (2, 4, 8, 16)
C_HEAD_DIM = 128
C_CONV = 4
C_CHUNK = 64

ADAM_LR = 0.001
ADAM_B1 = 0.9
ADAM_B2 = 0.999
ADAM_EPS = 1e-08
ADAM_WD = 0.01
ADAM_STEP = 10

N_DEV = 8
N_CHIP = 4
LANE = 128
SUBLANE = 8
VMEM_LIMIT = 48 * 1024 * 1024

NN = (((1,), (0,)), ((), ()))
NT = (((1,), (1,)), ((), ()))
TN = (((0,), (0,)), ((), ()))


def _tile(n, cap, mult=LANE):
    best = 0
    for d in range(mult, min(n, cap) + 1, mult):
        if n % d == 0:
            best = d
    return best or n


def _div_le(n, cap):
    return max(d for d in range(1, cap + 1) if n % d == 0)


def _params(*sem):
    return pltpu.CompilerParams(dimension_semantics=sem or None, vmem_limit_bytes=VMEM_LIMIT)


def _dot(a, b, dims):
    return lax.dot_general(a, b, dims, preferred_element_type=F32)


def _matmul(a, b, mode, name):
    if mode == "nn":
        (m, k), n = a.shape, b.shape[1]
    elif mode == "nt":
        (m, k), n = a.shape, b.shape[0]
    else:
        (k, m), n = a.shape, b.shape[1]
    tm, tn, tk = _tile(m, 1024), _tile(n, 1024), _tile(k, 2816)
    nk = k // tk
    dims = {"nn": NN, "nt": NT, "tn": TN}[mode]
    if mode == "tn":
        a_spec = pl.BlockSpec((tk, tm), lambda i, j, kk: (kk, i))
    else:
        a_spec = pl.BlockSpec((tm, tk), lambda i, j, kk: (i, kk))
    if mode == "nt":
        b_spec = pl.BlockSpec((tn, tk), lambda i, j, kk: (j, kk))
    else:
        b_spec = pl.BlockSpec((tk, tn), lambda i, j, kk: (kk, j))

    def body(a_ref, b_ref, o_ref):
        part = _dot(a_ref[...], b_ref[...], dims)
        if nk == 1:
            o_ref[...] = part
        else:
            kk = pl.program_id(2)

            @pl.when(kk == 0)
            def _():
                o_ref[...] = part

            @pl.when(kk > 0)
            def _():
                o_ref[...] += part

    return pl.pallas_call(
        body,
        out_shape=jax.ShapeDtypeStruct((m, n), F32),
        grid=(m // tm, n // tn, nk),
        in_specs=[a_spec, b_spec],
        out_specs=pl.BlockSpec((tm, tn), lambda i, j, kk: (i, j)),
        compiler_params=_params("parallel", "parallel", "arbitrary"),
        name=name,
    )(a, b)


@jax.custom_vjp
def _dense(h, w, hook):
    return _matmul(h.astype(BF16), w, "nn", "dense_fwd")


def _dense_fwd(h, w, hook):
    hb = h.astype(BF16)
    return _matmul(hb, w, "nn", "dense_fwd"), (hb, w)


def _dense_bwd(res, dy):
    hb, w = res
    dyb = dy.astype(BF16)
    return _matmul(dyb, w, "nt", "dense_dx"), None, _matmul(hb, dyb, "tn", "dense_dw")


_dense.defvjp(_dense_fwd, _dense_bwd)


def _rec_plan(n, nc, anti, backward):
    nb = _div_le(int(np.gcd(n, nc)), 12)
    ng, ncb = n // nb, nc // nb
    if not anti:
        order = (lambda g: ng - 1 - g) if backward else (lambda g: g)
        return nb, ng, order, backward
    if backward:
        return nb, ng, (lambda g: jnp.where(g < ng - ncb, ncb + g, g - (ng - ncb))), False
    return nb, ng, (lambda g: jnp.where(g < ncb, ncb - 1 - g, ng - 1 - (g - ncb))), True


REC_HEADS = 8


def _rec_call(body, ins, outs, plan, state_shape, name):
    nb, ng, order, descending = plan
    hh = outs[0].shape[0]
    hb = _div_le(hh, REC_HEADS)

    def spec(shape):
        shared = hh // shape[0]
        return pl.BlockSpec((hb // shared, nb) + tuple(shape[2:]), lambda h, g: (h, order(g), 0, 0))

    return pl.pallas_call(
        functools.partial(body, hb, nb, descending),
        out_shape=outs,
        grid=(hh // hb, ng),
        in_specs=[spec(t.shape) for t in ins],
        out_specs=[spec(t.shape) for t in outs],
        scratch_shapes=[pltpu.VMEM((hb,) + tuple(state_shape), F32)],
        compiler_params=_params("parallel", "arbitrary"),
        name=name,
    )(*ins)


def _chunk_loop(nb, descending, step):
    def run(i, carry):
        step(nb - 1 - i if descending else i)
        return carry

    lax.fori_loop(0, nb, run, 0)


def _reset_at_start(state):
    @pl.when(pl.program_id(1) == 0)
    def _():
        state[...] = jnp.zeros_like(state)


def _bf(ref, h, c):
    return ref[h, c].astype(BF16)


def _within_chunk(scores, anti):
    row = lax.broadcasted_iota(jnp.int32, scores.shape, 0)
    col = lax.broadcasted_iota(jnp.int32, scores.shape, 1)
    return jnp.where(row <= col if anti else row >= col, scores, 0.0)


def _hgrn_rec_fwd(anti, nc, q_in, k_out, v, decay, qs, ks):
    hh, n, _, dk = q_in.shape
    dv = v.shape[-1]

    def body(hb, nb, descending, q_ref, k_ref, v_ref, d_ref, qs_ref, ks_ref, o_ref, s_ref, st):
        _reset_at_start(st)
        heads = range(hb)

        def step(c):
            s = [st[h] for h in heads]
            vb = [_bf(v_ref, h, c) for h in heads]
            scores = [_within_chunk(_dot(_bf(qs_ref, h, c), _bf(ks_ref, h, c), NT), anti) for h in heads]
            for h in heads:
                s_ref[h, c] = s[h]
                o_ref[h, c] = (_dot(_bf(q_ref, h, c), s[h].astype(BF16), NT)
                               + _dot(scores[h].astype(BF16), vb[h], NN))
            for h in heads:
                st[h] = s[h] * d_ref[h, c] + _dot(vb[h], _bf(k_ref, h, c), TN)

        _chunk_loop(nb, descending, step)

    ins = (q_in, k_out, v, decay, qs, ks)
    outs = (jax.ShapeDtypeStruct(v.shape, F32), jax.ShapeDtypeStruct((hh, n, dv, dk), F32))
    return _rec_call(body, ins, outs, _rec_plan(n, nc, anti, False), (dv, dk), "hgrn_rec_fwd")


def _hgrn_rec_bwd(anti, nc, q_in, k_out, v, decay, qs, ks, states, do):
    dk = q_in.shape[-1]
    n, dv = v.shape[1], v.shape[-1]

    def body(hb, nb, descending, q_ref, k_ref, v_ref, d_ref, qs_ref, ks_ref, s_ref, do_ref,
             dq_ref, dk_ref, dv_ref, dd_ref, dqs_ref, dks_ref, dst):
        _reset_at_start(dst)
        heads = range(hb)

        def step(c):
            ds = [dst[h] for h in heads]
            dsb = [x.astype(BF16) for x in ds]
            s = [s_ref[h, c] for h in heads]
            dob = [_bf(do_ref, h, c) for h in heads]
            vb = [_bf(v_ref, h, c) for h in heads]
            qsb = [_bf(qs_ref, h, c) for h in heads]
            ksb = [_bf(ks_ref, h, c) for h in heads]
            scores = [_within_chunk(_dot(qsb[h], ksb[h], NT), anti).astype(BF16) for h in heads]
            dscores = [_within_chunk(_dot(dob[h], vb[h], NT), anti).astype(BF16) for h in heads]
            for h in heads:
                dqs_ref[h, c] = _dot(dscores[h], ksb[h], NN)
                dks_ref[h, c] = _dot(dscores[h], qsb[h], TN)
                dq_ref[h, c] = _dot(dob[h], s[h].astype(BF16), NN)
                dk_ref[h, c] = _dot(vb[h], dsb[h], NN)
                dv_ref[h, c] = _dot(_bf(k_ref, h, c), dsb[h], NT) + _dot(scores[h], dob[h], TN)
                dd_ref[h, c] = jnp.sum(ds[h] * s[h], axis=0, keepdims=True)
            for h in heads:
                dst[h] = ds[h] * d_ref[h, c] + _dot(dob[h], _bf(q_ref, h, c), TN)

        _chunk_loop(nb, descending, step)

    ins = (q_in, k_out, v, decay, qs, ks, states, do)
    outs = tuple(jax.ShapeDtypeStruct(t.shape, F32) for t in (q_in, k_out, v, decay, qs, ks))
    return _rec_call(body, ins, outs, _rec_plan(n, nc, anti, True), (dv, dk), "hgrn_rec_bwd")


@functools.partial(jax.custom_vjp, nondiff_argnums=(0, 1))
def _hgrn_rec(anti, nc, q_in, k_out, v, decay, qs, ks):
    return _hgrn_rec_fwd(anti, nc, q_in, k_out, v, decay, qs, ks)[0]


def _hgrn_rec_vjp_fwd(anti, nc, q_in, k_out, v, decay, qs, ks):
    o, states = _hgrn_rec_fwd(anti, nc, q_in, k_out, v, decay, qs, ks)
    return o, (q_in, k_out, v, decay, qs, ks, states)


def _hgrn_rec_vjp_bwd(anti, nc, res, do):
    return _hgrn_rec_bwd(anti, nc, *res, do)


_hgrn_rec.defvjp(_hgrn_rec_vjp_fwd, _hgrn_rec_vjp_bwd)


def _gdn_rec_fwd(anti, nc, u, w, qk, q, k, scal, gl):
    hh, n, _, dk = w.shape
    dv = u.shape[-1]
    rep = hh // q.shape[0]

    def body(hb, nb, descending, u_ref, w_ref, qk_ref, q_ref, k_ref, sc_ref, gl_ref, o_ref, s_ref, vn_ref, st):
        _reset_at_start(st)
        heads = range(hb)

        def step(c):
            s = [st[h] for h in heads]
            sb = [x.astype(BF16) for x in s]
            sc = [_scalars(sc_ref, h, c, dk, (EG, EGR)) for h in heads]
            qd = [(q_ref[h // rep, c] * sc[h][0]).astype(BF16) for h in heads]
            kd = [(k_ref[h // rep, c] * sc[h][1]).astype(BF16) for h in heads]
            vn = [u_ref[h, c] - _dot(_bf(w_ref, h, c), sb[h], NT) for h in heads]
            vnb = [x.astype(BF16) for x in vn]
            for h in heads:
                s_ref[h, c] = s[h]
                vn_ref[h, c] = vn[h]
                o_ref[h, c] = _dot(qd[h], sb[h], NT) + _dot(_bf(qk_ref, h, c), vnb[h], NN)
            for h in heads:
                st[h] = s[h] * gl_ref[h, c] + _dot(vnb[h], kd[h], TN)

        _chunk_loop(nb, descending, step)

    ins = (u, w, qk, q, k, scal, gl)
    outs = (
        jax.ShapeDtypeStruct(u.shape, F32),
        jax.ShapeDtypeStruct((hh, n, dv, dk), F32),
        jax.ShapeDtypeStruct(u.shape, F32),
    )
    return _rec_call(body, ins, outs, _rec_plan(n, nc, anti, False), (dv, dk), "gdn_rec_fwd")


def _gdn_rec_bwd(anti, nc, w, qk, q, k, scal, gl, states, vnew, do):
    hh, n, _, dk = w.shape
    dv = vnew.shape[-1]
    rep = hh // q.shape[0]

    def body(hb, nb, descending, w_ref, qk_ref, q_ref, k_ref, sc_ref, gl_ref, s_ref, vn_ref, do_ref,
             du_ref, dw_ref, dqk_ref, dq_ref, dk_ref, dsc_ref, dgl_ref, dst):
        _reset_at_start(dst)
        heads = range(hb)

        def step(c):
            ds = [dst[h] for h in heads]
            dsb = [x.astype(BF16) for x in ds]
            s = [s_ref[h, c] for h in heads]
            sb = [x.astype(BF16) for x in s]
            sc = [_scalars(sc_ref, h, c, dk, (EG, EGR)) for h in heads]
            qf = [q_ref[h // rep, c] for h in heads]
            kf = [k_ref[h // rep, c] for h in heads]
            qd = [(qf[h] * sc[h][0]).astype(BF16) for h in heads]
            kd = [(kf[h] * sc[h][1]).astype(BF16) for h in heads]
            vnb = [_bf(vn_ref, h, c) for h in heads]
            dob = [_bf(do_ref, h, c) for h in heads]
            dvn = [_dot(_bf(qk_ref, h, c), dob[h], TN) + _dot(kd[h], dsb[h], NT) for h in heads]
            dvnb = [x.astype(BF16) for x in dvn]
            dqd = [_dot(dob[h], sb[h], NN) for h in heads]
            dkd = [_dot(vnb[h], dsb[h], NN) for h in heads]
            for h in heads:
                du_ref[h, c] = dvn[h]
                dw_ref[h, c] = -_dot(dvnb[h], sb[h], NN)
                dqk_ref[h, c] = _dot(dob[h], vnb[h], NT)
                dq_ref[h, c] = dqd[h] * sc[h][0]
                dk_ref[h, c] = dkd[h] * sc[h][1]
                dsc_ref[h, c] = _scalar_rows([None, _token_sums(dqd[h] * qf[h]), _token_sums(dkd[h] * kf[h])])
                dgl_ref[h, c] = jnp.sum(ds[h] * s[h], axis=0, keepdims=True)
            for h in heads:
                dst[h] = ds[h] * gl_ref[h, c] + _dot(dob[h], qd[h], TN) - _dot(dvnb[h], _bf(w_ref, h, c), TN)

        _chunk_loop(nb, descending, step)

    ins = (w, qk, q, k, scal, gl, states, vnew, do)
    outs = (jax.ShapeDtypeStruct(vnew.shape, F32), jax.ShapeDtypeStruct(w.shape, F32),
            jax.ShapeDtypeStruct(qk.shape, F32), jax.ShapeDtypeStruct((hh,) + q.shape[1:], F32),
            jax.ShapeDtypeStruct((hh,) + k.shape[1:], F32), jax.ShapeDtypeStruct(scal.shape, F32),
            jax.ShapeDtypeStruct(gl.shape, F32))
    return _rec_call(body, ins, outs, _rec_plan(n, nc, anti, True), (dv, dk), "gdn_rec_bwd")


@functools.partial(jax.custom_vjp, nondiff_argnums=(0, 1))
def _gdn_rec(anti, nc, u, w, qk, q, k, scal, gl):
    return _gdn_rec_fwd(anti, nc, u, w, qk, q, k, scal, gl)[0]


def _gdn_rec_vjp_fwd(anti, nc, u, w, qk, q, k, scal, gl):
    o, states, vnew = _gdn_rec_fwd(anti, nc, u, w, qk, q, k, scal, gl)
    return o, (w, qk, q, k, scal, gl, states, vnew)


def _gdn_rec_vjp_bwd(anti, nc, res, do):
    du, dw, dqk, dq, dk, dsc, dgl = _gdn_rec_bwd(anti, nc, *res, do)
    return du, dw, dqk, _add_shared_heads(dq, res[2].shape[0]), _add_shared_heads(dk, res[3].shape[0]), dsc, dgl


_gdn_rec.defvjp(_gdn_rec_vjp_fwd, _gdn_rec_vjp_bwd)


def _split(a):
    hi = a.astype(BF16)
    return hi, (a - hi.astype(F32)).astype(BF16)


def _dot3(a, b, dims=NN):
    ah, al = _split(a)
    bh, bl = _split(b)
    return _dot(ah, bh, dims) + (_dot(ah, bl, dims) + _dot(al, bh, dims))


INV_BASE = 8


def _inv_unit_triangular(mats):
    size = mats[0].shape[0]
    row = lax.broadcasted_iota(jnp.int32, (size, size), 0)
    col = lax.broadcasted_iota(jnp.int32, (size, size), 1)
    eye = jnp.where(row == col, 1.0, 0.0).astype(F32)

    def same_block(width):
        shift = width.bit_length() - 1
        return jnp.right_shift(row, shift) == jnp.right_shift(col, shift)

    base = [jnp.where(same_block(INV_BASE), m, 0.0) for m in mats]
    sq = [_dot3(b, b) for b in base]
    inv = [_dot3(eye - b, eye + s) for b, s in zip(base, sq)]
    width = 4
    while width < INV_BASE:
        sq = [_dot3(s, s) for s in sq]
        inv = [_dot3(i, eye + s) for i, s in zip(inv, sq)]
        width *= 2
    while width < size:
        outer = same_block(2 * width) & jnp.logical_not(same_block(width))
        part = [_dot3(i, jnp.where(outer, m, 0.0)) for i, m in zip(inv, mats)]
        inv = [i - _dot3(p, i) for i, p in zip(inv, part)]
        width *= 2
    return inv


PREP_CHUNKS_FWD = 12
PREP_CHUNKS_BWD = 6


def _prep_call(body, ins, outs, name, chunks):
    heads = outs[0].shape[0]
    nb = _div_le(ins[0].shape[1], chunks)

    def spec(shape):
        rep = heads // shape[0]
        return pl.BlockSpec((1, nb) + tuple(shape[2:]), lambda h, g: (h // rep, g, 0, 0))

    return pl.pallas_call(
        functools.partial(body, nb),
        out_shape=outs,
        grid=(heads, ins[0].shape[1] // nb),
        in_specs=[spec(t.shape) for t in ins],
        out_specs=[spec(t.shape) for t in outs],
        compiler_params=_params("parallel", "parallel"),
        name=name,
    )(*ins)


def _lane_broadcast(row, width):
    size = row.shape[1]
    r = lax.broadcasted_iota(jnp.int32, (size, size), 0)
    c = lax.broadcasted_iota(jnp.int32, (size, size), 1)
    hi, lo = _split(jnp.where(r == c, jnp.broadcast_to(row, (size, size)), 0.0))
    ones = jnp.ones((size, width), BF16)
    return _dot(hi, ones, NN) + _dot(lo, ones, NN)


def _token_sums(p):
    hi, lo = _split(p)
    ones = jnp.ones((SUBLANE, p.shape[1]), BF16)
    return _dot(ones, hi, NT) + _dot(ones, lo, NT)


def _scalar_rows(rows):
    shape = next(r.shape for r in rows if r is not None)
    index = lax.broadcasted_iota(jnp.int32, shape, 0)
    out = jnp.zeros(shape, F32)
    for j, r in enumerate(rows):
        if r is not None:
            out = jnp.where(index == j, r, out)
    return out


BETA, EG, EGR = 0, 1, 2


def _scalars(sc_ref, h, c, width, which):
    rows = sc_ref[h, c]
    return [_lane_broadcast(rows[j:j + 1], width) for j in which]


def _off_diagonal(x):
    row = lax.broadcasted_iota(jnp.int32, x.shape, 0)
    col = lax.broadcasted_iota(jnp.int32, x.shape, 1)
    return jnp.where(row == col, 0.0, x)


def _gdn_prep_fwd(q, k, v, decay, scal):
    dk, dv = k.shape[-1], v.shape[-1]

    def body(nb, q_ref, k_ref, v_ref, dec_ref, sc_ref, u_ref, w_ref, qk_ref, t_ref):
        chunks = range(nb)
        sc = [_scalars(sc_ref, 0, c, dk, (BETA, EG)) for c in chunks]
        kf = [k_ref[0, c] for c in chunks]
        kc = [x.astype(BF16) for x in kf]
        kb = [kf[c] * sc[c][0] for c in chunks]
        dec = [dec_ref[0, c] for c in chunks]
        a = [_off_diagonal(_dot(kb[c].astype(BF16), kc[c], NT) * dec[c]) for c in chunks]
        t = _inv_unit_triangular(a)
        for c in chunks:
            tb = t[c].astype(BF16)
            t_ref[0, c] = t[c]
            u_ref[0, c] = _dot(tb, (v_ref[0, c] * sc[c][0][:, :dv]).astype(BF16), NN)
            w_ref[0, c] = _dot(tb, (kb[c] * sc[c][1]).astype(BF16), NN)
            qk_ref[0, c] = _dot(_bf(q_ref, 0, c), kc[c], NT) * dec[c]

    ins = (q, k, v, decay, scal)
    outs = (jax.ShapeDtypeStruct(v.shape, F32), jax.ShapeDtypeStruct(v.shape[:3] + (dk,), F32),
            jax.ShapeDtypeStruct(decay.shape, F32), jax.ShapeDtypeStruct(decay.shape, F32))
    return _prep_call(body, ins, outs, "gdn_prep_fwd", PREP_CHUNKS_FWD)


def _gdn_prep_bwd(q, k, v, decay, scal, t, du, dw, dqk):
    dk = k.shape[-1]

    def body(nb, q_ref, k_ref, v_ref, dec_ref, sc_ref, t_ref, du_ref, dw_ref, dqk_ref,
             dq_ref, dk_ref, dv_ref, ddec_ref, dsc_ref):
        chunks = range(nb)
        sc = [_scalars(sc_ref, 0, c, dk, (BETA, EG)) for c in chunks]
        beta = [s[0] for s in sc]
        eg = [s[1] for s in sc]
        kf = [k_ref[0, c] for c in chunks]
        vf = [v_ref[0, c] for c in chunks]
        kc = [x.astype(BF16) for x in kf]
        qc = [_bf(q_ref, 0, c) for c in chunks]
        kb = [kf[c] * beta[c] for c in chunks]
        kbb = [x.astype(BF16) for x in kb]
        tb = [_bf(t_ref, 0, c) for c in chunks]
        dub = [_bf(du_ref, 0, c) for c in chunks]
        dwb = [_bf(dw_ref, 0, c) for c in chunks]
        dvb = [_dot(tb[c], dub[c], TN) for c in chunks]
        dkbg = [_dot(tb[c], dwb[c], TN) for c in chunks]
        dt = [_dot(dub[c], (vf[c] * beta[c]).astype(BF16), NT) + _dot(dwb[c], (kb[c] * eg[c]).astype(BF16), NT)
              for c in chunks]
        left = [_dot(tb[c], dt[c].astype(BF16), TN).astype(BF16) for c in chunks]
        da = [-_off_diagonal(_dot(left[c], tb[c], NT)) for c in chunks]
        dqk = [dqk_ref[0, c] for c in chunks]
        for c in chunks:
            ddec_ref[0, c] = da[c] * _dot(kbb[c], kc[c], NT) + dqk[c] * _dot(qc[c], kc[c], NT)
        dkk = [(da[c] * dec_ref[0, c]).astype(BF16) for c in chunks]
        dqkd = [(dqk[c] * dec_ref[0, c]).astype(BF16) for c in chunks]
        dkb = [_dot(dkk[c], kc[c], NN) + dkbg[c] * eg[c] for c in chunks]
        for c in chunks:
            dq_ref[0, c] = _dot(dqkd[c], kc[c], NN)
            dk_ref[0, c] = _dot(dkk[c], kbb[c], TN) + _dot(dqkd[c], qc[c], TN) + dkb[c] * beta[c]
            dv_ref[0, c] = dvb[c] * beta[c]
            dsc_ref[0, c] = _scalar_rows([_token_sums(dkb[c] * kf[c] + dvb[c] * vf[c]),
                                          _token_sums(dkbg[c] * kb[c])])

    ins = (q, k, v, decay, scal, t, du, dw, dqk)
    heads = v.shape[0]
    outs = (jax.ShapeDtypeStruct((heads,) + q.shape[1:], F32), jax.ShapeDtypeStruct((heads,) + k.shape[1:], F32),
            jax.ShapeDtypeStruct(v.shape, F32), jax.ShapeDtypeStruct(decay.shape, F32),
            jax.ShapeDtypeStruct(scal.shape, F32))
    return _prep_call(body, ins, outs, "gdn_prep_bwd", PREP_CHUNKS_BWD)


def _add_shared_heads(d, key_heads):
    return d.reshape((key_heads, d.shape[0] // key_heads) + d.shape[1:]).sum(axis=1)


@jax.custom_vjp
def _gdn_prep(q, k, v, decay, scal):
    return _gdn_prep_fwd(q, k, v, decay, scal)[:3]


def _gdn_prep_vjp_fwd(q, k, v, decay, scal):
    u, w, qk, t = _gdn_prep_fwd(q, k, v, decay, scal)
    return (u, w, qk), (q, k, v, decay, scal, t)


def _gdn_prep_vjp_bwd(res, cot):
    dq, dk, dv, ddec, dsc = _gdn_prep_bwd(*res, *cot)
    return _add_shared_heads(dq, res[0].shape[0]), _add_shared_heads(dk, res[1].shape[0]), dv, ddec, dsc


_gdn_prep.defvjp(_gdn_prep_vjp_fwd, _gdn_prep_vjp_bwd)


def _loss_head(y, target):
    rows, d = y.shape
    tr = _tile(rows, 256, SUBLANE)

    def body(y_ref, t_ref, dy_ref, sq_ref):
        diff = y_ref[...] - t_ref[...]
        dy_ref[...] = diff * (1.0 / d)
        part = jnp.sum((diff * diff).reshape(tr // SUBLANE, SUBLANE, d), axis=0)

        @pl.when(pl.program_id(0) == 0)
        def _():
            sq_ref[...] = part

        @pl.when(pl.program_id(0) > 0)
        def _():
            sq_ref[...] += part

    row = pl.BlockSpec((tr, d), lambda i: (i, 0))
    return pl.pallas_call(
        body,
        out_shape=(jax.ShapeDtypeStruct((rows, d), F32), jax.ShapeDtypeStruct((SUBLANE, d), F32)),
        grid=(rows // tr,),
        in_specs=[row, row],
        out_specs=[row, pl.BlockSpec((SUBLANE, d), lambda i: (0, 0))],
        compiler_params=_params("arbitrary"),
        name="loss_head",
    )(y, target)


def _add_half(g, r1, my_c):
    nblk, half, cols = r1.shape
    tr = _tile(half, 128, 2 * SUBLANE)
    per_half = half // tr

    def body(c_ref, g_ref, r_ref, o_ref):
        o_ref[...] = (g_ref[...] + r_ref[...]).astype(BF16)

    grid_spec = pltpu.PrefetchScalarGridSpec(
        num_scalar_prefetch=1,
        grid=(nblk, per_half),
        in_specs=[
            pl.BlockSpec((1, tr, cols), lambda b, i, c_ref: (b, c_ref[0] * per_half + i, 0)),
            pl.BlockSpec((1, tr, cols), lambda b, i, c_ref: (b, i, 0)),
        ],
        out_specs=pl.BlockSpec((1, tr, cols), lambda b, i, c_ref: (b, i, 0)),
    )
    return pl.pallas_call(
        body,
        out_shape=jax.ShapeDtypeStruct(r1.shape, BF16),
        grid_spec=grid_spec,
        compiler_params=_params("parallel", "parallel"),
        name="grad_add_half",
    )(my_c.reshape(1).astype(jnp.int32), g, r1)


def _sum_chips(own, slots, place):
    _, half, cols = own.shape
    tr = _tile(half, 128, 2 * SUBLANE)
    per_half = half // tr

    def body(p_ref, a_ref, b_ref, c_ref, d_ref, o_ref):
        f32 = lambda ref: ref[0].astype(F32)
        o_ref[...] = (f32(a_ref) + f32(b_ref)) + (f32(c_ref) + f32(d_ref))

    def block(k):
        return pl.BlockSpec((1, tr, cols), lambda i, p_ref: (p_ref[k], i, 0))

    grid_spec = pltpu.PrefetchScalarGridSpec(
        num_scalar_prefetch=1,
        grid=(per_half,),
        in_specs=[block(0), block(1), block(2), block(3)],
        out_specs=pl.BlockSpec((tr, cols), lambda i, p_ref: (p_ref[4] * per_half + i, 0)),
    )
    return pl.pallas_call(
        body,
        out_shape=jax.ShapeDtypeStruct((2 * half, cols), F32),
        grid_spec=grid_spec,
        compiler_params=_params("parallel"),
        name="grad_sum_chips",
    )(place, own, slots, slots, slots)


def _cast_into_block(w, chip):
    rows, cols = w.shape
    tr = _tile(rows, 128, 2 * SUBLANE)

    def body(j_ref, w_ref, o_ref):
        o_ref[0] = w_ref[...].astype(BF16)

    grid_spec = pltpu.PrefetchScalarGridSpec(
        num_scalar_prefetch=1,
        grid=(rows // tr,),
        in_specs=[pl.BlockSpec((tr, cols), lambda i, j_ref: (i, 0))],
        out_specs=pl.BlockSpec((1, tr, cols), lambda i, j_ref: (j_ref[0], i, 0)),
    )
    return pl.pallas_call(
        body,
        out_shape=jax.ShapeDtypeStruct((N_CHIP, rows, cols), BF16),
        grid_spec=grid_spec,
        compiler_params=_params("parallel"),
        name="cast_weight_shard",
    )(chip.reshape(1).astype(jnp.int32), w)


def _sum_leading(x, name):
    k, rows, cols = x.shape
    tr = _tile(rows, 128, SUBLANE)

    def body(x_ref, o_ref):
        parts = [x_ref[i] for i in range(k)]
        while len(parts) > 1:
            parts = [parts[i] + parts[i + 1] for i in range(0, len(parts), 2)]
        o_ref[...] = parts[0]

    return pl.pallas_call(
        body,
        out_shape=jax.ShapeDtypeStruct((rows, cols), F32),
        grid=(rows // tr,),
        in_specs=[pl.BlockSpec((k, tr, cols), lambda i: (0, i, 0))],
        out_specs=pl.BlockSpec((tr, cols), lambda i: (i, 0)),
        compiler_params=_params("parallel"),
        name=name,
    )(x)


def _adamw(w, g, m, v, name):
    rows, cols = w.shape
    tr = _tile(rows, 128, SUBLANE)
    m_scale = 1.0 / (1.0 - ADAM_B1 ** ADAM_STEP)
    v_scale = 1.0 / (1.0 - ADAM_B2 ** ADAM_STEP)

    def body(w_ref, g_ref, m_ref, v_ref, d_ref, nm_ref, nv_ref):
        gg = g_ref[...]
        nm = ADAM_B1 * m_ref[...] + (1.0 - ADAM_B1) * gg
        nv = ADAM_B2 * v_ref[...] + (1.0 - ADAM_B2) * (gg * gg)
        nm_ref[...] = nm
        nv_ref[...] = nv
        d_ref[...] = -ADAM_LR * ((nm * m_scale) / (jnp.sqrt(nv * v_scale) + ADAM_EPS) + ADAM_WD * w_ref[...])

    spec = pl.BlockSpec((tr, cols), lambda i: (i, 0))
    out = jax.ShapeDtypeStruct((rows, cols), F32)
    return pl.pallas_call(
        body,
        out_shape=(out, out, out),
        grid=(rows // tr,),
        in_specs=[spec] * 4,
        out_specs=[spec] * 3,
        compiler_params=_params("parallel"),
        name=name,
    )(w, g, m, v)


def _ada_fwd(a16, w):
    layers, d, n = w.shape
    tn = _tile(n, 512)

    def body(a_ref, w_ref, o_ref):
        o_ref[0] = _dot(a_ref[...].astype(BF16), w_ref[0].astype(BF16), NN)

    return pl.pallas_call(
        body,
        out_shape=jax.ShapeDtypeStruct((layers, a16.shape[0], n), F32),
        grid=(layers, n // tn),
        in_specs=[pl.BlockSpec(a16.shape, lambda l, j: (0, 0)), pl.BlockSpec((1, d, tn), lambda l, j: (l, 0, j))],
        out_specs=pl.BlockSpec((1, a16.shape[0], tn), lambda l, j: (l, 0, j)),
        compiler_params=_params("parallel", "parallel"),
        name="ada_fwd",
    )(a16, w)


def _ada_bwd(a16, dm, w):
    layers, d, n = w.shape
    tn = _tile(n, 512)
    rows = a16.shape[0]

    def body(a_ref, dm_ref, w_ref, gw_ref, ga_ref):
        dmb = dm_ref[0].astype(BF16)
        gw_ref[0] = _dot(a_ref[...].astype(BF16), dmb, TN)
        part = _dot(dmb, w_ref[0].astype(BF16), NT)
        first = (pl.program_id(0) == 0) & (pl.program_id(1) == 0)

        @pl.when(first)
        def _():
            ga_ref[...] = part

        @pl.when(jnp.logical_not(first))
        def _():
            ga_ref[...] += part

    return pl.pallas_call(
        body,
        out_shape=(jax.ShapeDtypeStruct(w.shape, F32), jax.ShapeDtypeStruct((rows, d), F32)),
        grid=(layers, n // tn),
        in_specs=[
            pl.BlockSpec((rows, d), lambda l, j: (0, 0)),
            pl.BlockSpec((1, rows, tn), lambda l, j: (l, 0, j)),
            pl.BlockSpec((1, d, tn), lambda l, j: (l, 0, j)),
        ],
        out_specs=[pl.BlockSpec((1, d, tn), lambda l, j: (l, 0, j)), pl.BlockSpec((rows, d), lambda l, j: (0, 0))],
        compiler_params=_params("arbitrary", "arbitrary"),
        name="ada_bwd",
    )(a16, dm, w)


def _place():
    ix, iy, ic = lax.axis_index("x"), lax.axis_index("y"), lax.axis_index("c")
    return ix, iy, ic


def _flip(coord, bit):
    return 1 - coord if bit else coord


def _remote(src, dst, send_sem, recv_sem, to):
    return pltpu.make_async_remote_copy(
        src_ref=src, dst_ref=dst, send_sem=send_sem, recv_sem=recv_sem, device_id=to, device_id_type=MESH)


def _all_gather8(x, name):
    rows, cols = x.shape

    def body(x_ref, o_ref, send_sems, recv_sems):
        ix, iy, ic = _place()
        me = 4 * ix + 2 * iy + ic
        o_ref[me] = x_ref[...]
        peers = []
        for p in range(1, N_DEV):
            to = (_flip(ix, p & 4), _flip(iy, p & 2), _flip(ic, p & 1))
            peers.append((to, 4 * to[0] + 2 * to[1] + to[2]))
        sends = [_remote(x_ref, o_ref.at[me], send_sems.at[p], recv_sems.at[p], to)
                 for p, (to, _) in enumerate(peers)]
        for cp in sends:
            cp.start()
        for p, (to, slot) in enumerate(peers):
            _remote(x_ref, o_ref.at[slot], send_sems.at[p], recv_sems.at[p], to).wait_recv()
        for cp in sends:
            cp.wait_send()

    return pl.pallas_call(
        body,
        out_shape=jax.ShapeDtypeStruct((N_DEV, rows, cols), F32),
        in_specs=[pl.BlockSpec(memory_space=pltpu.VMEM)],
        out_specs=pl.BlockSpec(memory_space=pltpu.VMEM),
        scratch_shapes=[pltpu.SemaphoreType.DMA((N_DEV - 1,)), pltpu.SemaphoreType.DMA((N_DEV - 1,))],
        compiler_params=pltpu.CompilerParams(vmem_limit_bytes=VMEM_LIMIT),
        name=name,
    )(x)


def _other_chips(ix, iy):
    chips = [(1 - ix, iy), (ix, 1 - iy), (1 - ix, 1 - iy)]
    return [(cx, cy, 2 * cx + cy) for cx, cy in chips]


def _hbm_call(body, ins, out_shapes, scratch, name, in_place=False):
    hbm = pl.BlockSpec(memory_space=pltpu.HBM)
    return pl.pallas_call(
        body,
        out_shape=out_shapes,
        in_specs=[hbm] * len(ins),
        out_specs=[hbm] * len(out_shapes),
        scratch_shapes=scratch,
        input_output_aliases={t: t for t in range(len(ins))} if in_place else {},
        compiler_params=pltpu.CompilerParams(vmem_limit_bytes=VMEM_LIMIT),
        name=name,
    )(*ins)


def _gather_weights(blocks):
    nt = len(blocks)

    def body(*refs):
        ins, outs = refs[:nt], refs[nt:2 * nt]
        ici_send, ici_recv, d2d_send, d2d_recv = refs[2 * nt:]
        ix, iy, ic = _place()
        j = 2 * ix + iy
        sibling = (ix, iy, 1 - ic)
        chips = _other_chips(ix, iy)
        sends, waits = [], []
        for t in range(nt):
            half = ins[t].shape[1] // 2
            mine = pl.ds(ic * half, half)
            for q, (cx, cy, _) in enumerate(chips):
                cp = _remote(ins[t].at[j, mine], outs[t].at[j, mine], ici_send.at[3 * t + q], ici_recv.at[3 * t + q],
                             (cx, cy, ic))
                cp.start()
                sends.append(cp)
        for q, (cx, cy, jq) in enumerate(chips):
            for t in range(nt):
                half = ins[t].shape[1] // 2
                mine = pl.ds(ic * half, half)
                theirs = pl.ds((1 - ic) * half, half)
                k = 3 * t + q
                _remote(ins[t].at[j, mine], outs[t].at[jq, mine], ici_send.at[k], ici_recv.at[k],
                        (cx, cy, ic)).wait_recv()
                cp = _remote(outs[t].at[jq, mine], outs[t].at[jq, mine], d2d_send.at[k], d2d_recv.at[k], sibling)
                cp.start()
                sends.append(cp)
                waits.append(_remote(outs[t].at[jq, theirs], outs[t].at[jq, theirs], d2d_send.at[k], d2d_recv.at[k],
                                     sibling))
        for cp in waits:
            cp.wait_recv()
        for cp in sends:
            cp.wait_send()

    dma = pltpu.SemaphoreType.DMA
    outs = tuple(jax.ShapeDtypeStruct(b.shape, b.dtype) for b in blocks)
    return _hbm_call(body, blocks, outs, [dma((3 * nt,))] * 4, "gather_weights", in_place=True)


def _exchange_halves(grads):
    nt = len(grads)

    def body(*refs):
        ins, outs = refs[:nt], refs[nt:2 * nt]
        send_sems, recv_sems = refs[2 * nt:]
        ix, iy, ic = _place()
        sibling = (ix, iy, 1 - ic)
        copies = []
        for t in range(nt):
            half = ins[t].shape[1] // 2
            theirs = pl.ds((1 - ic) * half, half)
            cp = _remote(ins[t].at[:, theirs, :], outs[t], send_sems.at[t], recv_sems.at[t], sibling)
            cp.start()
            copies.append(cp)
        for cp in copies:
            cp.wait_recv()
        for cp in copies:
            cp.wait_send()

    dma = pltpu.SemaphoreType.DMA
    outs = tuple(jax.ShapeDtypeStruct((g.shape[0], g.shape[1] // 2, g.shape[2]), F32) for g in grads)
    return _hbm_call(body, grads, outs, [dma((nt,)), dma((nt,))], "grad_exchange_halves")


def _exchange_blocks(parts):
    nt = len(parts)

    def body(*refs):
        ins, outs = refs[:nt], refs[nt:2 * nt]
        send_sems, recv_sems = refs[2 * nt:]
        ix, iy, ic = _place()
        j = 2 * ix + iy
        chips = _other_chips(ix, iy)
        sends, waits = [], []
        for t in range(nt):
            for q, (cx, cy, jq) in enumerate(chips):
                k = 3 * t + q
                cp = _remote(ins[t].at[jq], outs[t].at[j], send_sems.at[k], recv_sems.at[k], (cx, cy, ic))
                cp.start()
                sends.append(cp)
                waits.append(_remote(ins[t].at[jq], outs[t].at[jq], send_sems.at[k], recv_sems.at[k], (cx, cy, ic)))
        for cp in waits:
            cp.wait_recv()
        for cp in sends:
            cp.wait_send()

    dma = pltpu.SemaphoreType.DMA
    outs = tuple(jax.ShapeDtypeStruct(p.shape, p.dtype) for p in parts)
    return _hbm_call(body, parts, outs, [dma((3 * nt,)), dma((3 * nt,))], "grad_exchange_blocks")


def _share_halves(blocks):
    nt = len(blocks)

    def body(*refs):
        ins, outs = refs[:nt], refs[nt:2 * nt]
        send_sems, recv_sems = refs[2 * nt:]
        ix, iy, ic = _place()
        sibling = (ix, iy, 1 - ic)
        sends, waits = [], []
        for t in range(nt):
            half = ins[t].shape[0] // 2
            mine = pl.ds(ic * half, half)
            theirs = pl.ds((1 - ic) * half, half)
            cp = _remote(ins[t].at[mine], outs[t].at[mine], send_sems.at[t], recv_sems.at[t], sibling)
            cp.start()
            sends.append(cp)
            waits.append(_remote(ins[t].at[theirs], outs[t].at[theirs], send_sems.at[t], recv_sems.at[t], sibling))
        for cp in waits:
            cp.wait_recv()
        for cp in sends:
            cp.wait_send()

    dma = pltpu.SemaphoreType.DMA
    outs = tuple(jax.ShapeDtypeStruct(b.shape, F32) for b in blocks)
    return _hbm_call(body, blocks, outs, [dma((nt,)), dma((nt,))], "grad_share_halves", in_place=True)


def _reduce_grads(grads, ix, iy, ic):
    recv = _exchange_halves(grads)
    chip_sums = [_add_half(g, r1, ic) for g, r1 in zip(grads, recv)]
    slots = _exchange_blocks(chip_sums)
    place = jnp.stack([2 * ix + iy] + [jq for _, _, jq in _other_chips(ix, iy)] + [ic]).astype(jnp.int32)
    return _share_halves([_sum_chips(own, got, place) for own, got in zip(chip_sums, slots)])


def _rmsnorm(x, w):
    return x * lax.rsqrt(jnp.mean(x * x, axis=-1, keepdims=True) + EPS) * w


def _l2norm(x):
    return x * lax.rsqrt(jnp.sum(x * x, axis=-1, keepdims=True) + EPS)


def _heads(t, d):
    return t.reshape(t.shape[:-1] + (t.shape[-1] // d, d))


def _chunks(t, chunk):
    n = t.shape[0] // chunk
    return t.reshape(n, chunk, t.shape[1], t.shape[2]).transpose(2, 0, 1, 3)


def _unchunks(t):
    hh, n, chunk, d = t.shape
    return t.transpose(1, 2, 0, 3).reshape(n * chunk, hh, d)


def _triangle(size, anti):
    ones = jnp.ones((size, size), bool)
    return jnp.triu(ones) if anti else jnp.tril(ones)


def _hgrn2_scan(q, k, v, log_f, n_ctx, anti):
    q, k, v, log_f = (_chunks(t, A_CHUNK) for t in (q, k, v, log_f))
    b = lax.cumsum(log_f, axis=2, reverse=anti)
    end = 0 if anti else A_CHUNK - 1
    mid = A_CHUNK // 2 if anti else A_CHUNK // 2 - 1
    b_last = b[:, :, end:end + 1, :]
    b_mid = b[:, :, mid:mid + 1, :]
    return _unchunks(_hgrn_rec(anti, n_ctx // A_CHUNK, q * jnp.exp(b), k * jnp.exp(b_last - b), v, jnp.exp(b_last),
                               q * jnp.exp(b - b_mid), k * jnp.exp(b_mid - b)))


def _gdn_scan(q, k, v, g, beta, n_ctx, anti):
    g = _chunks(g[..., None], C_CHUNK)[..., 0]
    beta = _chunks(beta[..., None], C_CHUNK)[..., 0]
    gc = lax.cumsum(g, axis=2, reverse=anti)
    end = 0 if anti else C_CHUNK - 1
    causal = _triangle(C_CHUNK, anti)
    diff = gc[..., :, None] - gc[..., None, :]
    decay = jnp.where(causal, jnp.exp(jnp.where(causal, diff, 0.0)), 0.0)
    rows = [beta, jnp.exp(gc), jnp.exp(gc[..., end:end + 1] - gc)]
    scal = jnp.stack(rows + [jnp.zeros_like(beta)] * (SUBLANE - len(rows)), axis=2)
    u, w, qk = _gdn_prep(q, k, v, decay, scal)
    g_last = jnp.exp(gc[..., end])
    gl = jnp.broadcast_to(g_last[..., None, None], g_last.shape + (1, q.shape[-1]))
    return _unchunks(_gdn_rec(anti, n_ctx // C_CHUNK, u, w, qk, q, k, scal, gl))


def _bidirectional(scan_fn, fwd_args, bwd_args, n_ctx):
    return scan_fn(*fwd_args, n_ctx, False) + scan_fn(*bwd_args, n_ctx, True)


def _multiscale_pool(u, pool_w, pool_scale):
    rows, length, width = u.shape
    groups = len(POOL_WINDOWS)
    gdim = width // groups
    uf = u.reshape(rows, length, groups, gdim)
    cs = jnp.concatenate([jnp.zeros_like(uf[:, :1]), jnp.cumsum(uf, axis=1)], axis=1)
    pos = np.arange(length)
    mixed = []
    for gi, win in enumerate(POOL_WINDOWS):
        below, above = win // 2, win - win // 2
        lo = np.clip(pos - below, 0, length - 1)
        hi = np.clip(pos + win - 1 - below, 0, length - 1)
        cnt = jnp.asarray((hi - lo + 1).astype(np.float32))[None, :, None]
        csg = cs[:, :, gi, :]
        upper = jnp.concatenate([csg[:, above:]] + [csg[:, length:]] * (above - 1), axis=1)
        lower = jnp.concatenate([jnp.zeros_like(csg[:, :below]), csg[:, :length - below]], axis=1)
        mixed.append((upper - lower) / cnt - uf[:, :, gi, :])
    d = jnp.stack(mixed, axis=2)
    y = jnp.einsum("rlgc,gcd->rlgd", d, pool_w)
    return y.reshape(rows, length, width) * pool_scale


CONV_LEFT = C_CONV // 2
CONV_PAD = SUBLANE


def _conv_window_taps(win, r0, rows, n_ctx, reverse):
    tg = r0 + lax.broadcasted_iota(jnp.int32, (rows, win.shape[1]), 0)
    taps = []
    for j in range(C_CONV):
        off = (CONV_LEFT - j) if reverse else (j - CONV_LEFT)
        if off == 0:
            taps.append(win[CONV_PAD:CONV_PAD + rows])
            continue
        moved = pltpu.roll(win, (-off) % win.shape[0], 0)[CONV_PAD:CONV_PAD + rows]
        lo, hi = (n_ctx, n_ctx - off) if off < 0 else (n_ctx - off, n_ctx)
        taps.append(jnp.where(tg >= lo, jnp.where(tg < hi, 0.0, moved), moved))
    return taps


def _fill_padded(pad, src, total):
    zeros = jnp.zeros((CONV_PAD, pad.shape[1]), F32)
    pad[0:CONV_PAD] = zeros
    pad[total + CONV_PAD:total + 2 * CONV_PAD] = zeros
    if src is not None:
        pad[CONV_PAD:total + CONV_PAD] = src[...]


def _conv_call(body, ins, outs, out_blocks, scratch, name):
    total, width = ins[0].shape
    tc = LANE
    col = pl.BlockSpec((total, tc), lambda j: (0, j))
    return pl.pallas_call(
        body,
        out_shape=outs,
        grid=(width // tc,),
        in_specs=[col if t.shape[0] == total else pl.BlockSpec((t.shape[0], tc), lambda j: (0, j)) for t in ins],
        out_specs=[pl.BlockSpec((rows, tc), lambda j: (0, j)) for rows in out_blocks],
        scratch_shapes=[pltpu.VMEM((total + 2 * CONV_PAD, tc), F32)] * scratch,
        compiler_params=_params("parallel"),
        name=name,
    )(*ins)


def _conv_silu_fwd(n_ctx, x, w):
    total = x.shape[0]
    rows = _tile(total, 128, SUBLANE)

    def body(x_ref, w_ref, o_ref, pad):
        _fill_padded(pad, x_ref, total)
        wt = w_ref[...]

        def chunk(i, carry):
            r0 = pl.multiple_of(i * rows, SUBLANE)
            taps = _conv_window_taps(pad[pl.ds(r0, rows + 2 * CONV_PAD), :], r0, rows, n_ctx, False)
            pre = sum(t * wt[j:j + 1] for j, t in enumerate(taps))
            o_ref[pl.ds(r0, rows), :] = pre / (1.0 + jnp.exp(-pre))
            return carry

        lax.fori_loop(0, total // rows, chunk, 0)

    return _conv_call(body, (x, w), (jax.ShapeDtypeStruct(x.shape, F32),), (total,), 1, "conv_silu_fwd")[0]


def _conv_silu_bwd(n_ctx, x, w, dact):
    total = x.shape[0]
    rows = _tile(total, 128, SUBLANE)

    def body(x_ref, w_ref, da_ref, dx_ref, dw_ref, xpad, gpad):
        _fill_padded(xpad, x_ref, total)
        _fill_padded(gpad, None, total)
        wt = w_ref[...]
        dw_ref[...] = jnp.zeros_like(dw_ref)

        def grad_pre(i, carry):
            r0 = pl.multiple_of(i * rows, SUBLANE)
            taps = _conv_window_taps(xpad[pl.ds(r0, rows + 2 * CONV_PAD), :], r0, rows, n_ctx, False)
            pre = sum(t * wt[j:j + 1] for j, t in enumerate(taps))
            sig = 1.0 / (1.0 + jnp.exp(-pre))
            dpre = da_ref[pl.ds(r0, rows), :] * (sig * (1.0 + pre * (1.0 - sig)))
            gpad[pl.ds(r0 + CONV_PAD, rows), :] = dpre
            for j, t in enumerate(taps):
                part = jnp.sum((dpre * t).reshape(rows // SUBLANE, SUBLANE, t.shape[1]), axis=0)
                dw_ref[SUBLANE * j:SUBLANE * (j + 1)] += part
            return carry

        lax.fori_loop(0, total // rows, grad_pre, 0)

        def grad_x(i, carry):
            r0 = pl.multiple_of(i * rows, SUBLANE)
            taps = _conv_window_taps(gpad[pl.ds(r0, rows + 2 * CONV_PAD), :], r0, rows, n_ctx, True)
            dx_ref[pl.ds(r0, rows), :] = sum(t * wt[j:j + 1] for j, t in enumerate(taps))
            return carry

        lax.fori_loop(0, total // rows, grad_x, 0)

    outs = (jax.ShapeDtypeStruct(x.shape, F32), jax.ShapeDtypeStruct((C_CONV * SUBLANE, x.shape[1]), F32))
    return _conv_call(body, (x, w, dact), outs, (total, C_CONV * SUBLANE), 2, "conv_silu_bwd")


@functools.partial(jax.custom_vjp, nondiff_argnums=(0,))
def _conv_silu(n_ctx, x, w):
    return _conv_silu_fwd(n_ctx, x, w)


def _conv_silu_vjp_fwd(n_ctx, x, w):
    return _conv_silu_fwd(n_ctx, x, w), (x, w)


def _conv_silu_vjp_bwd(n_ctx, res, dact):
    x, w = res
    dx, dw = _conv_silu_bwd(n_ctx, x, w, dact)
    return dx, dw.reshape(C_CONV, SUBLANE, -1).sum(axis=1)


_conv_silu.defvjp(_conv_silu_vjp_fwd, _conv_silu_vjp_bwd)


def _hgrn2_gates(pre_f, lb):
    log_f = jnp.log(lb + (1.0 - lb) * jax.nn.sigmoid(pre_f))
    k = (1.0 - lb) * jax.nn.sigmoid(-pre_f)
    return _heads(k, A_HEAD_DIM), _heads(log_f, A_HEAD_DIM)


def _even_mixer(h, n_ctx, lb, w_in, w_in_hook, a_norm, pool_w, pool_scale, w_out, w_out_hook):
    a_width = w_out.shape[0] // 2
    p = _dense(h, w_in, w_in_hook)
    q, f_f, f_b, i, g, u = jnp.split(p, [a_width * s for s in range(1, 6)], axis=-1)
    q = _heads(jax.nn.silu(q), A_HEAD_DIM)
    i = _heads(i, A_HEAD_DIM)
    k_f, logf_f = _hgrn2_gates(f_f, lb[0])
    k_b, logf_b = _hgrn2_gates(f_b, lb[1])
    o = _bidirectional(_hgrn2_scan, (q, k_f, i, logf_f), (q, k_b, i, logf_b), n_ctx)
    b_width = u.shape[-1]
    u_l = u[n_ctx:].reshape(-1, GRID_W, b_width)
    pooled = jnp.concatenate([
        _multiscale_pool(u[None, :n_ctx], pool_w, pool_scale)[0],
        _multiscale_pool(u_l, pool_w, pool_scale).reshape(-1, b_width),
    ], axis=0)
    a_out = _rmsnorm(o, a_norm) * jax.nn.silu(_heads(g, A_HEAD_DIM))
    a_out = a_out.reshape(a_out.shape[0], a_width)
    return _dense(jnp.concatenate([a_out, pooled], axis=-1), w_out, w_out_hook)


def _odd_mixer(h, n_ctx, big, hooks, conv_w, a_log, dt_bias, norm_w):
    w_out, w_out_hook = big["od_w_out"], hooks["od_w_out"]
    value_width = w_out.shape[0]
    key_width = value_width // 2
    a_rate = jnp.exp(a_log)
    z = _dense(h, big["od_w_z"], hooks["od_w_z"])
    gates = _dense(h, big["od_w_gate"], hooks["od_w_gate"])
    qkv = _conv_silu(n_ctx, _dense(h, big["od_w_qkv"], hooks["od_w_qkv"]), conv_w)
    q, k, v = jnp.split(qkv, [key_width, 2 * key_width], axis=-1)
    q = _chunks(_l2norm(_heads(q, C_HEAD_DIM)) * C_HEAD_DIM ** -0.5, C_CHUNK)
    k = _chunks(_l2norm(_heads(k, C_HEAD_DIM)), C_CHUNK)
    v = _chunks(_heads(v, C_HEAD_DIM), C_CHUNK)
    a_f, a_b, b_f, b_b = jnp.split(gates, 4, axis=-1)
    g_f = -a_rate[0] * jax.nn.softplus(a_f + dt_bias[0])
    g_b = -a_rate[1] * jax.nn.softplus(a_b + dt_bias[1])
    o = _bidirectional(_gdn_scan, (q, k, v, g_f, jax.nn.sigmoid(b_f)), (q, k, v, g_b, jax.nn.sigmoid(b_b)), n_ctx)
    y = _rmsnorm(o, norm_w) * jax.nn.silu(_heads(z, C_HEAD_DIM))
    return _dense(y.reshape(y.shape[0], value_width), w_out, w_out_hook)


def _swiglu(h, w13, w13_hook, w2, w2_hook):
    gate, up = jnp.split(_dense(h, w13, w13_hook), 2, axis=-1)
    return _dense(jax.nn.silu(gate) * up, w2, w2_hook)


def _forward(x, ctx, mods, big, hooks, small):
    n_ctx = ctx.shape[0]
    d = x.shape[-1]
    stream = jnp.concatenate([ctx, x], axis=0)
    is_ctx = (jnp.arange(stream.shape[0]) < n_ctx)[:, None]
    lb_all = jnp.cumsum(jax.nn.softmax(small["ev_lb"], axis=1), axis=1)
    for layer in range(2):
        m = [jnp.where(is_ctx, mods[layer, 1, s * d:(s + 1) * d][None, :], mods[layer, 0, s * d:(s + 1) * d][None, :])
             for s in range(6)]
        nw = small["norm_w"][layer]
        h = _rmsnorm(stream, nw[0]) * (1.0 + m[1]) + m[0]
        if layer == 0:
            y = _even_mixer(h, n_ctx, lb_all[:, layer], big["ev_w_in"], hooks["ev_w_in"], small["ev_a_norm"][0],
                            small["ev_pool_w"][0], small["ev_pool_scale"][0], big["ev_w_out"], hooks["ev_w_out"])
        else:
            y = _odd_mixer(h, n_ctx, big, hooks, small["od_conv"][0], small["od_A_log"][0], small["od_dt_bias"][0],
                           small["od_norm"][0])
        stream = stream + m[2] * _rmsnorm(y, nw[1])
        h = _rmsnorm(stream, nw[2]) * (1.0 + m[4]) + m[3]
        f = _swiglu(h, big["ffn_w13_%d" % layer], hooks["ffn_w13_%d" % layer],
                    big["ffn_w2_%d" % layer], hooks["ffn_w2_%d" % layer])
        stream = stream + m[5] * _rmsnorm(f, nw[3])
    return stream[n_ctx:]


def _pack(arrays):
    flat = jnp.concatenate([a.reshape(-1) for a in arrays])
    pad = (-flat.shape[0]) % (SUBLANE * LANE)
    return jnp.pad(flat, (0, pad)).reshape(-1, LANE)


def _unpack(packed, shapes, lead=()):
    flat = packed.reshape(lead + (-1,))
    out, at = [], 0
    for shape in shapes:
        size = int(np.prod(shape))
        out.append(flat[..., at:at + size].reshape(lead + tuple(shape)))
        at += size
    return out


def _from_chips(gathered, axis):
    return jnp.concatenate([gathered[2 * j] for j in range(N_CHIP)], axis=axis)


def _col_natural(blocks):
    nblk, k, n = blocks.shape
    return blocks.transpose(1, 0, 2).reshape(k, nblk * n)


def _col_blocked(nat):
    k, n4 = nat.shape
    return nat.reshape(k, N_CHIP, n4 // N_CHIP).transpose(1, 0, 2)


def kernel(x, c, ctx, c_ctx, w_ada, b_ada, norm_w, ev_w_in, ev_lb, ev_a_norm, ev_pool_w, ev_pool_scale, ev_w_out, od_w_in, od_conv, od_A_log, od_dt_bias, od_norm, od_w_out, ffn_w13, ffn_w2, loss_target, m_c_ctx, m_w_ada, m_b_ada, m_norm_w, m_ev_w_in, m_ev_lb, m_ev_a_norm, m_ev_pool_w, m_ev_pool_scale, m_ev_w_out, m_od_w_in, m_od_conv, m_od_A_log, m_od_dt_bias, m_od_norm, m_od_w_out, m_ffn_w13, m_ffn_w2, v_c_ctx, v_w_ada, v_b_ada, v_norm_w, v_ev_w_in, v_ev_lb, v_ev_a_norm, v_ev_pool_w, v_ev_pool_scale, v_ev_w_out, v_od_w_in, v_od_conv, v_od_A_log, v_od_dt_bias, v_od_norm, v_od_w_out, v_ffn_w13, v_ffn_w2):
    ix, iy, ic = _place()
    me = 4 * ix + 2 * iy + ic
    chip = 2 * ix + iy
    d = x.shape[-1]
    layers = w_ada.shape[0]
    n_ada = w_ada.shape[-1]

    pre_parts = [c[0], norm_w, ev_lb, ev_pool_w, od_conv]
    pre = _all_gather8(_pack(pre_parts), "gather_small_inputs")
    c_all, norm_w_g, ev_lb_g, pool_w_g, od_conv_g = _unpack(pre, [p.shape for p in pre_parts], lead=(N_DEV,))
    small = {
        "norm_w": _from_chips(norm_w_g, 2),
        "ev_lb": _from_chips(ev_lb_g, 2),
        "ev_a_norm": ev_a_norm,
        "ev_pool_w": _from_chips(pool_w_g, 2),
        "ev_pool_scale": ev_pool_scale,
        "od_conv": _from_chips(od_conv_g, 2),
        "od_A_log": od_A_log,
        "od_dt_bias": od_dt_bias,
        "od_norm": od_norm,
    }

    silu_cc, silu_cc_vjp = jax.vjp(jax.nn.silu, c_ctx)
    a16 = jnp.concatenate([jax.nn.silu(c_all), silu_cc[None], jnp.zeros((2 * SUBLANE - N_DEV - 1, d), F32)], axis=0)
    mods_local = _ada_fwd(a16, w_ada)
    mods_g = _all_gather8(mods_local.reshape(-1, n_ada), "gather_modulation")
    mods_g = mods_g.reshape(N_DEV, layers, 2 * SUBLANE, n_ada)
    mods_full = _from_chips(mods_g, 2) + b_ada[:, None, :]
    mods = jnp.stack([lax.dynamic_index_in_dim(mods_full, me, axis=1, keepdims=False), mods_full[:, N_DEV]], axis=1)

    ffn13_rows = ffn_w13.shape[1]
    ffn2_rows = ffn_w2.shape[1]
    shards = [ev_w_in[0], ev_w_out[0], od_w_in[0], od_w_out[0],
              ffn_w13.reshape(-1, ffn_w13.shape[-1]), ffn_w2.reshape(-1, ffn_w2.shape[-1])]
    g_ev_in, g_ev_out, g_od_in, g_od_out, g_w13, g_w2 = _gather_weights([_cast_into_block(s, chip) for s in shards])
    od_nat = _col_natural(g_od_in)
    n_qkv = 2 * od_w_out.shape[1] * N_CHIP
    n_main = n_qkv + n_qkv // 2
    big = {
        "ev_w_in": _col_natural(g_ev_in),
        "ev_w_out": g_ev_out.reshape(-1, g_ev_out.shape[-1]),
        "od_w_qkv": od_nat[:, :n_qkv],
        "od_w_z": od_nat[:, n_qkv:n_main],
        "od_w_gate": od_nat[:, n_main:],
        "od_w_out": g_od_out.reshape(-1, g_od_out.shape[-1]),
    }
    for layer in range(layers):
        big["ffn_w13_%d" % layer] = _col_natural(g_w13[:, layer * ffn13_rows:(layer + 1) * ffn13_rows])
        big["ffn_w2_%d" % layer] = g_w2[:, layer * ffn2_rows:(layer + 1) * ffn2_rows].reshape(-1, g_w2.shape[-1])
    hooks = {name: jnp.zeros(w.shape, F32) for name, w in big.items()}

    def local_forward(x_, mods_, hooks_, small_):
        return _forward(x_, ctx[0], mods_, big, hooks_, small_)

    y, pullback = jax.vjp(local_forward, x[0], mods, hooks, small)
    dy, sq = _loss_head(y, loss_target[0])
    loss = lax.psum(jnp.sum(sq) * (0.5 / d), ("x", "y", "c"))
    grad_x, d_mods, d_big, d_small = pullback(dy)

    d_od_in = jnp.concatenate([d_big["od_w_qkv"], d_big["od_w_z"], d_big["od_w_gate"]], axis=1)
    blocked = [
        _col_blocked(d_big["ev_w_in"]),
        d_big["ev_w_out"].reshape(N_CHIP, -1, d_big["ev_w_out"].shape[-1]),
        _col_blocked(d_od_in),
        d_big["od_w_out"].reshape(N_CHIP, -1, d_big["od_w_out"].shape[-1]),
        jnp.concatenate([_col_blocked(d_big["ffn_w13_%d" % layer]) for layer in range(layers)], axis=1),
        jnp.concatenate([d_big["ffn_w2_%d" % layer].reshape(N_CHIP, -1, d_big["ffn_w2_%d" % layer].shape[-1])
                         for layer in range(layers)], axis=1),
    ]
    r_ev_in, r_ev_out, r_od_in, r_od_out, r_w13, r_w2 = _reduce_grads(blocked, ix, iy, ic)

    small_names = ["norm_w", "ev_lb", "ev_a_norm", "ev_pool_w", "ev_pool_scale", "od_conv", "od_A_log",
                   "od_dt_bias", "od_norm"]
    post_parts = [d_mods[:, 0], d_mods[:, 1]] + [d_small[n] for n in small_names]
    post = _all_gather8(_pack(post_parts), "gather_small_grads")
    post_sum = _sum_leading(post, "sum_small_grads")
    dm_l_all = _unpack(post, [post_parts[0].shape], lead=(N_DEV,))[0]
    summed = _unpack(post_sum, [p.shape for p in post_parts])
    dm_l_sum, dm_c_sum = summed[0], summed[1]
    g_small = dict(zip(small_names, summed[2:]))
    grad_b_ada = dm_l_sum + dm_c_sum

    def my_cols(full, width):
        return lax.dynamic_slice_in_dim(full, chip * width, width, axis=full.ndim - 1)

    dm_rows = jnp.concatenate([
        my_cols(dm_l_all, n_ada).transpose(1, 0, 2),
        my_cols(dm_c_sum, n_ada)[:, None, :],
        jnp.zeros((layers, 2 * SUBLANE - N_DEV - 1, n_ada), F32),
    ], axis=1)
    grad_w_ada, ga = _ada_bwd(a16, dm_rows, w_ada)
    ga_g = _all_gather8(jnp.pad(ga[N_DEV][None], ((0, SUBLANE - 1), (0, 0))), "gather_c_ctx_grad")
    g_silu_cc = _sum_leading(jnp.stack([ga_g[2 * j] for j in range(N_CHIP)]), "sum_c_ctx_grad")[0]
    grad_c_ctx = silu_cc_vjp(g_silu_cc)[0]

    grads = {
        "c_ctx": grad_c_ctx,
        "w_ada": grad_w_ada,
        "b_ada": grad_b_ada,
        "norm_w": my_cols(g_small["norm_w"], norm_w.shape[-1]),
        "ev_w_in": r_ev_in[None],
        "ev_lb": my_cols(g_small["ev_lb"], ev_lb.shape[-1]),
        "ev_a_norm": g_small["ev_a_norm"],
        "ev_pool_w": lax.dynamic_slice_in_dim(g_small["ev_pool_w"], chip * ev_pool_w.shape[2], ev_pool_w.shape[2], axis=2),
        "ev_pool_scale": g_small["ev_pool_scale"],
        "ev_w_out": r_ev_out[None],
        "od_w_in": r_od_in[None],
        "od_conv": my_cols(g_small["od_conv"], od_conv.shape[-1]),
        "od_A_log": g_small["od_A_log"],
        "od_dt_bias": g_small["od_dt_bias"],
        "od_norm": g_small["od_norm"],
        "od_w_out": r_od_out[None],
        "ffn_w13": r_w13.reshape(ffn_w13.shape),
        "ffn_w2": r_w2.reshape(ffn_w2.shape),
    }

    weights = dict(c_ctx=c_ctx, w_ada=w_ada, b_ada=b_ada, norm_w=norm_w, ev_w_in=ev_w_in, ev_lb=ev_lb,
                   ev_a_norm=ev_a_norm, ev_pool_w=ev_pool_w, ev_pool_scale=ev_pool_scale, ev_w_out=ev_w_out,
                   od_w_in=od_w_in, od_conv=od_conv, od_A_log=od_A_log, od_dt_bias=od_dt_bias, od_norm=od_norm,
                   od_w_out=od_w_out, ffn_w13=ffn_w13, ffn_w2=ffn_w2)
    first = dict(c_ctx=m_c_ctx, w_ada=m_w_ada, b_ada=m_b_ada, norm_w=m_norm_w, ev_w_in=m_ev_w_in, ev_lb=m_ev_lb,
                 ev_a_norm=m_ev_a_norm, ev_pool_w=m_ev_pool_w, ev_pool_scale=m_ev_pool_scale, ev_w_out=m_ev_w_out,
                 od_w_in=m_od_w_in, od_conv=m_od_conv, od_A_log=m_od_A_log, od_dt_bias=m_od_dt_bias,
                 od_norm=m_od_norm, od_w_out=m_od_w_out, ffn_w13=m_ffn_w13, ffn_w2=m_ffn_w2)
    second = dict(c_ctx=v_c_ctx, w_ada=v_w_ada, b_ada=v_b_ada, norm_w=v_norm_w, ev_w_in=v_ev_w_in, ev_lb=v_ev_lb,
                  ev_a_norm=v_ev_a_norm, ev_pool_w=v_ev_pool_w, ev_pool_scale=v_ev_pool_scale, ev_w_out=v_ev_w_out,
                  od_w_in=v_od_w_in, od_conv=v_od_conv, od_A_log=v_od_A_log, od_dt_bias=v_od_dt_bias,
                  od_norm=v_od_norm, od_w_out=v_od_w_out, ffn_w13=v_ffn_w13, ffn_w2=v_ffn_w2)
    names = list(weights)
    large = ["w_ada", "ev_w_in", "ev_w_out", "od_w_in", "od_w_out", "ffn_w13", "ffn_w2"]
    little = [n for n in names if n not in large]
    delta, new_m, new_v = {}, {}, {}
    for n in large:
        as2d = lambda t: t.reshape(-1, t.shape[-1])
        out = _adamw(as2d(weights[n]), as2d(grads[n]), as2d(first[n]), as2d(second[n]), "adamw_" + n)
        delta[n], new_m[n], new_v[n] = (t.reshape(weights[n].shape) for t in out)
    shapes = [weights[n].shape for n in little]
    out = _adamw(*(_pack([src[n] for n in little]) for src in (weights, grads, first, second)), "adamw_small")
    for res, packed in zip((delta, new_m, new_v), out):
        res.update(zip(little, _unpack(packed, shapes)))

    return (loss, grad_x[None], *[grads[n] for n in names], *[delta[n] for n in names],
            *[new_m[n] for n in names], *[new_v[n] for n in names])
```

```python
import functools

import numpy as np
import jax
import jax.numpy as jnp
from jax import lax
from jax.experimental import pallas as pl
from jax.experimental.pallas import tpu as pltpu

F32 = jnp.float32
BF16 = jnp.bfloat16
MESH = pl.DeviceIdType.MESH

EPS = 1e-6
GRID_W = 64
A_HEAD_DIM = 128
A_CHUNK = 32
POOL_WINDOWS = (2, 4, 8, 16)
C_HEAD_DIM = 128
C_CONV = 4
C_CHUNK = 64

ADAM_LR = 0.001
ADAM_B1 = 0.9
ADAM_B2 = 0.999
ADAM_EPS = 1e-08
ADAM_WD = 0.01
ADAM_STEP = 10

N_DEV = 8
N_CHIP = 4
LANE = 128
SUBLANE = 8
VMEM_LIMIT = 48 * 1024 * 1024

NN = (((1,), (0,)), ((), ()))
NT = (((1,), (1,)), ((), ()))
TN = (((0,), (0,)), ((), ()))


def _tile(n, cap, mult=LANE):
    best = 0
    for d in range(mult, min(n, cap) + 1, mult):
        if n % d == 0:
            best = d
    return best or n


def _div_le(n, cap):
    return max(d for d in range(1, cap + 1) if n % d == 0)


def _params(*sem):
    return pltpu.CompilerParams(dimension_semantics=sem or None, vmem_limit_bytes=VMEM_LIMIT)


def _dot(a, b, dims):
    return lax.dot_general(a, b, dims, preferred_element_type=F32)


def _matmul(a, b, mode, name):
    if mode == "nn":
        (m, k), n = a.shape, b.shape[1]
    elif mode == "nt":
        (m, k), n = a.shape, b.shape[0]
    else:
        (k, m), n = a.shape, b.shape[1]
    tm, tn, tk = _tile(m, 1024), _tile(n, 1024), _tile(k, 2816 if mode == "tn" else 4096)
    nk = k // tk
    dims = {"nn": NN, "nt": NT, "tn": TN}[mode]
    if mode == "tn":
        a_spec = pl.BlockSpec((tk, tm), lambda i, j, kk: (kk, i))
    else:
        a_spec = pl.BlockSpec((tm, tk), lambda i, j, kk: (i, kk))
    if mode == "nt":
        b_spec = pl.BlockSpec((tn, tk), lambda i, j, kk: (j, kk))
    else:
        b_spec = pl.BlockSpec((tk, tn), lambda i, j, kk: (kk, j))

    def body(a_ref, b_ref, o_ref):
        part = _dot(a_ref[...], b_ref[...], dims)
        if nk == 1:
            o_ref[...] = part
        else:
            kk = pl.program_id(2)

            @pl.when(kk == 0)
            def _():
                o_ref[...] = part

            @pl.when(kk > 0)
            def _():
                o_ref[...] += part

    return pl.pallas_call(
        body,
        out_shape=jax.ShapeDtypeStruct((m, n), F32),
        grid=(m // tm, n // tn, nk),
        in_specs=[a_spec, b_spec],
        out_specs=pl.BlockSpec((tm, tn), lambda i, j, kk: (i, j)),
        compiler_params=_params("parallel", "parallel", "arbitrary"),
        name=name,
    )(a, b)


@jax.custom_vjp
def _dense(h, w, hook):
    return _matmul(h.astype(BF16), w, "nn", "dense_fwd")


def _dense_fwd(h, w, hook):
    hb = h.astype(BF16)
    return _matmul(hb, w, "nn", "dense_fwd"), (hb, w)


def _dense_bwd(res, dy):
    hb, w = res
    dyb = dy.astype(BF16)
    return _matmul(dyb, w, "nt", "dense_dx"), None, _matmul(hb, dyb, "tn", "dense_dw")


_dense.defvjp(_dense_fwd, _dense_bwd)


def _rec_plan(n, nc, anti, backward):
    nb = _div_le(int(np.gcd(n, nc)), 12)
    ng, ncb = n // nb, nc // nb
    if not anti:
        order = (lambda g: ng - 1 - g) if backward else (lambda g: g)
        return nb, ng, order, backward
    if backward:
        return nb, ng, (lambda g: jnp.where(g < ng - ncb, ncb + g, g - (ng - ncb))), False
    return nb, ng, (lambda g: jnp.where(g < ncb, ncb - 1 - g, ng - 1 - (g - ncb))), True


REC_HEADS = 8


def _rec_call(body, ins, outs, plan, state_shape, name):
    nb, ng, order, descending = plan
    hh = outs[0].shape[0]
    hb = _div_le(hh, REC_HEADS)

    def spec(shape):
        shared = hh // shape[0]
        return pl.BlockSpec((hb // shared, nb) + tuple(shape[2:]), lambda h, g: (h, order(g), 0, 0))

    return pl.pallas_call(
        functools.partial(body, hb, nb, descending),
        out_shape=outs,
        grid=(hh // hb, ng),
        in_specs=[spec(t.shape) for t in ins],
        out_specs=[spec(t.shape) for t in outs],
        scratch_shapes=[pltpu.VMEM((hb,) + tuple(state_shape), F32)],
        compiler_params=_params("parallel", "arbitrary"),
        name=name,
    )(*ins)


def _chunk_loop(nb, descending, step):
    def run(i, carry):
        step(nb - 1 - i if descending else i)
        return carry

    lax.fori_loop(0, nb, run, 0)


def _reset_at_start(state):
    @pl.when(pl.program_id(1) == 0)
    def _():
        state[...] = jnp.zeros_like(state)


def _bf(ref, h, c):
    return ref[h, c].astype(BF16)


def _within_chunk(scores, anti):
    row = lax.broadcasted_iota(jnp.int32, scores.shape, 0)
    col = lax.broadcasted_iota(jnp.int32, scores.shape, 1)
    return jnp.where(row <= col if anti else row >= col, scores, 0.0)


def _hgrn_rec_fwd(anti, nc, q_in, k_out, v, decay, qs, ks):
    hh, n, _, dk = q_in.shape
    dv = v.shape[-1]

    def body(hb, nb, descending, q_ref, k_ref, v_ref, d_ref, qs_ref, ks_ref, o_ref, s_ref, st):
        _reset_at_start(st)
        heads = range(hb)

        def step(c):
            s = [st[h] for h in heads]
            vb = [_bf(v_ref, h, c) for h in heads]
            scores = [_within_chunk(_dot(_bf(qs_ref, h, c), _bf(ks_ref, h, c), NT), anti) for h in heads]
            for h in heads:
                s_ref[h, c] = s[h]
                o_ref[h, c] = (_dot(_bf(q_ref, h, c), s[h].astype(BF16), NT)
                               + _dot(scores[h].astype(BF16), vb[h], NN))
            for h in heads:
                st[h] = s[h] * d_ref[h, c] + _dot(vb[h], _bf(k_ref, h, c), TN)

        _chunk_loop(nb, descending, step)

    ins = (q_in, k_out, v, decay, qs, ks)
    outs = (jax.ShapeDtypeStruct(v.shape, F32), jax.ShapeDtypeStruct((hh, n, dv, dk), F32))
    return _rec_call(body, ins, outs, _rec_plan(n, nc, anti, False), (dv, dk), "hgrn_rec_fwd")


def _hgrn_rec_bwd(anti, nc, q_in, k_out, v, decay, qs, ks, states, do):
    dk = q_in.shape[-1]
    n, dv = v.shape[1], v.shape[-1]

    def body(hb, nb, descending, q_ref, k_ref, v_ref, d_ref, qs_ref, ks_ref, s_ref, do_ref,
             dq_ref, dk_ref, dv_ref, dd_ref, dqs_ref, dks_ref, dst):
        _reset_at_start(dst)
        heads = range(hb)

        def step(c):
            ds = [dst[h] for h in heads]
            dsb = [x.astype(BF16) for x in ds]
            s = [s_ref[h, c] for h in heads]
            dob = [_bf(do_ref, h, c) for h in heads]
            vb = [_bf(v_ref, h, c) for h in heads]
            qsb = [_bf(qs_ref, h, c) for h in heads]
            ksb = [_bf(ks_ref, h, c) for h in heads]
            scores = [_within_chunk(_dot(qsb[h], ksb[h], NT), anti).astype(BF16) for h in heads]
            dscores = [_within_chunk(_dot(dob[h], vb[h], NT), anti).astype(BF16) for h in heads]
            for h in heads:
                dqs_ref[h, c] = _dot(dscores[h], ksb[h], NN)
                dks_ref[h, c] = _dot(dscores[h], qsb[h], TN)
                dq_ref[h, c] = _dot(dob[h], s[h].astype(BF16), NN)
                dk_ref[h, c] = _dot(vb[h], dsb[h], NN)
                dv_ref[h, c] = _dot(_bf(k_ref, h, c), dsb[h], NT) + _dot(scores[h], dob[h], TN)
                dd_ref[h, c] = jnp.sum(ds[h] * s[h], axis=0, keepdims=True)
            for h in heads:
                dst[h] = ds[h] * d_ref[h, c] + _dot(dob[h], _bf(q_ref, h, c), TN)

        _chunk_loop(nb, descending, step)

    ins = (q_in, k_out, v, decay, qs, ks, states, do)
    outs = tuple(jax.ShapeDtypeStruct(t.shape, F32) for t in (q_in, k_out, v, decay, qs, ks))
    return _rec_call(body, ins, outs, _rec_plan(n, nc, anti, True), (dv, dk), "hgrn_rec_bwd")


@functools.partial(jax.custom_vjp, nondiff_argnums=(0, 1))
def _hgrn_rec(anti, nc, q_in, k_out, v, decay, qs, ks):
    return _hgrn_rec_fwd(anti, nc, q_in, k_out, v, decay, qs, ks)[0]


def _hgrn_rec_vjp_fwd(anti, nc, q_in, k_out, v, decay, qs, ks):
    o, states = _hgrn_rec_fwd(anti, nc, q_in, k_out, v, decay, qs, ks)
    return o, (q_in, k_out, v, decay, qs, ks, states)


def _hgrn_rec_vjp_bwd(anti, nc, res, do):
    return _hgrn_rec_bwd(anti, nc, *res, do)


_hgrn_rec.defvjp(_hgrn_rec_vjp_fwd, _hgrn_rec_vjp_bwd)


def _gdn_rec_fwd(anti, nc, u, w, qk, q, k, scal, gl):
    hh, n, _, dk = w.shape
    dv = u.shape[-1]
    rep = hh // q.shape[0]

    def body(hb, nb, descending, u_ref, w_ref, qk_ref, q_ref, k_ref, sc_ref, gl_ref, o_ref, s_ref, vn_ref, st):
        _reset_at_start(st)
        heads = range(hb)

        def step(c):
            s = [st[h] for h in heads]
            sb = [x.astype(BF16) for x in s]
            sc = [_scalars(sc_ref, h, c, dk, (EG, EGR)) for h in heads]
            qd = [(q_ref[h // rep, c] * sc[h][0]).astype(BF16) for h in heads]
            kd = [(k_ref[h // rep, c] * sc[h][1]).astype(BF16) for h in heads]
            vn = [u_ref[h, c] - _dot(_bf(w_ref, h, c), sb[h], NT) for h in heads]
            vnb = [x.astype(BF16) for x in vn]
            for h in heads:
                s_ref[h, c] = s[h]
                vn_ref[h, c] = vn[h]
                o_ref[h, c] = _dot(qd[h], sb[h], NT) + _dot(_bf(qk_ref, h, c), vnb[h], NN)
            for h in heads:
                st[h] = s[h] * gl_ref[h, c] + _dot(vnb[h], kd[h], TN)

        _chunk_loop(nb, descending, step)

    ins = (u, w, qk, q, k, scal, gl)
    outs = (
        jax.ShapeDtypeStruct(u.shape, F32),
        jax.ShapeDtypeStruct((hh, n, dv, dk), F32),
        jax.ShapeDtypeStruct(u.shape, F32),
    )
    return _rec_call(body, ins, outs, _rec_plan(n, nc, anti, False), (dv, dk), "gdn_rec_fwd")


def _gdn_rec_bwd(anti, nc, w, qk, q, k, scal, gl, states, vnew, do):
    hh, n, _, dk = w.shape
    dv = vnew.shape[-1]
    rep = hh // q.shape[0]

    def body(hb, nb, descending, w_ref, qk_ref, q_ref, k_ref, sc_ref, gl_ref, s_ref, vn_ref, do_ref,
             du_ref, dw_ref, dqk_ref, dq_ref, dk_ref, dsc_ref, dgl_ref, dst):
        _reset_at_start(dst)
        heads = range(hb)

        def step(c):
            ds = [dst[h] for h in heads]
            dsb = [x.astype(BF16) for x in ds]
            s = [s_ref[h, c] for h in heads]
            sb = [x.astype(BF16) for x in s]
            sc = [_scalars(sc_ref, h, c, dk, (EG, EGR)) for h in heads]
            qf = [q_ref[h // rep, c] for h in heads]
            kf = [k_ref[h // rep, c] for h in heads]
            qd = [(qf[h] * sc[h][0]).astype(BF16) for h in heads]
            kd = [(kf[h] * sc[h][1]).astype(BF16) for h in heads]
            vnb = [_bf(vn_ref, h, c) for h in heads]
            dob = [_bf(do_ref, h, c) for h in heads]
            dvn = [_dot(_bf(qk_ref, h, c), dob[h], TN) + _dot(kd[h], dsb[h], NT) for h in heads]
            dvnb = [x.astype(BF16) for x in dvn]
            dqd = [_dot(dob[h], sb[h], NN) for h in heads]
            dkd = [_dot(vnb[h], dsb[h], NN) for h in heads]
            for h in heads:
                du_ref[h, c] = dvn[h]
                dw_ref[h, c] = -_dot(dvnb[h], sb[h], NN)
                dqk_ref[h, c] = _dot(dob[h], vnb[h], NT)
                dq_ref[h, c] = dqd[h] * sc[h][0]
                dk_ref[h, c] = dkd[h] * sc[h][1]
                dsc_ref[h, c] = _scalar_rows([None, _token_sums(dqd[h] * qf[h]), _token_sums(dkd[h] * kf[h])])
                dgl_ref[h, c] = jnp.sum(ds[h] * s[h], axis=0, keepdims=True)
            for h in heads:
                dst[h] = ds[h] * gl_ref[h, c] + _dot(dob[h], qd[h], TN) - _dot(dvnb[h], _bf(w_ref, h, c), TN)

        _chunk_loop(nb, descending, step)

    ins = (w, qk, q, k, scal, gl, states, vnew, do)
    outs = (jax.ShapeDtypeStruct(vnew.shape, F32), jax.ShapeDtypeStruct(w.shape, F32),
            jax.ShapeDtypeStruct(qk.shape, F32), jax.ShapeDtypeStruct((hh,) + q.shape[1:], F32),
            jax.ShapeDtypeStruct((hh,) + k.shape[1:], F32), jax.ShapeDtypeStruct(scal.shape, F32),
            jax.ShapeDtypeStruct(gl.shape, F32))
    return _rec_call(body, ins, outs, _rec_plan(n, nc, anti, True), (dv, dk), "gdn_rec_bwd")


@functools.partial(jax.custom_vjp, nondiff_argnums=(0, 1))
def _gdn_rec(anti, nc, u, w, qk, q, k, scal, gl):
    return _gdn_rec_fwd(anti, nc, u, w, qk, q, k, scal, gl)[0]


def _gdn_rec_vjp_fwd(anti, nc, u, w, qk, q, k, scal, gl):
    o, states, vnew = _gdn_rec_fwd(anti, nc, u, w, qk, q, k, scal, gl)
    return o, (w, qk, q, k, scal, gl, states, vnew)


def _gdn_rec_vjp_bwd(anti, nc, res, do):
    du, dw, dqk, dq, dk, dsc, dgl = _gdn_rec_bwd(anti, nc, *res, do)
    return du, dw, dqk, _add_shared_heads(dq, res[2].shape[0]), _add_shared_heads(dk, res[3].shape[0]), dsc, dgl


_gdn_rec.defvjp(_gdn_rec_vjp_fwd, _gdn_rec_vjp_bwd)


def _split(a):
    hi = a.astype(BF16)
    return hi, (a - hi.astype(F32)).astype(BF16)


def _dot3(a, b, dims=NN):
    ah, al = _split(a)
    bh, bl = _split(b)
    return _dot(ah, bh, dims) + (_dot(ah, bl, dims) + _dot(al, bh, dims))


INV_BASE = 8


def _inv_unit_triangular(mats):
    size = mats[0].shape[0]
    row = lax.broadcasted_iota(jnp.int32, (size, size), 0)
    col = lax.broadcasted_iota(jnp.int32, (size, size), 1)
    eye = jnp.where(row == col, 1.0, 0.0).astype(F32)

    def same_block(width):
        shift = width.bit_length() - 1
        return jnp.right_shift(row, shift) == jnp.right_shift(col, shift)

    base = [jnp.where(same_block(INV_BASE), m, 0.0) for m in mats]
    sq = [_dot3(b, b) for b in base]
    inv = [_dot3(eye - b, eye + s) for b, s in zip(base, sq)]
    width = 4
    while width < INV_BASE:
        sq = [_dot3(s, s) for s in sq]
        inv = [_dot3(i, eye + s) for i, s in zip(inv, sq)]
        width *= 2
    while width < size:
        outer = same_block(2 * width) & jnp.logical_not(same_block(width))
        part = [_dot3(i, jnp.where(outer, m, 0.0)) for i, m in zip(inv, mats)]
        inv = [i - _dot3(p, i) for i, p in zip(inv, part)]
        width *= 2
    return inv


PREP_CHUNKS_FWD = 12
PREP_CHUNKS_BWD = 6


def _prep_call(body, ins, outs, name, chunks):
    heads = outs[0].shape[0]
    nb = _div_le(ins[0].shape[1], chunks)

    def spec(shape):
        rep = heads // shape[0]
        return pl.BlockSpec((1, nb) + tuple(shape[2:]), lambda h, g: (h // rep, g, 0, 0))

    return pl.pallas_call(
        functools.partial(body, nb),
        out_shape=outs,
        grid=(heads, ins[0].shape[1] // nb),
        in_specs=[spec(t.shape) for t in ins],
        out_specs=[spec(t.shape) for t in outs],
        compiler_params=_params("parallel", "parallel"),
        name=name,
    )(*ins)


def _lane_broadcast(row, width):
    size = row.shape[1]
    r = lax.broadcasted_iota(jnp.int32, (size, size), 0)
    c = lax.broadcasted_iota(jnp.int32, (size, size), 1)
    hi, lo = _split(jnp.where(r == c, jnp.broadcast_to(row, (size, size)), 0.0))
    ones = jnp.ones((size, width), BF16)
    return _dot(hi, ones, NN) + _dot(lo, ones, NN)


def _token_sums(p):
    hi, lo = _split(p)
    ones = jnp.ones((SUBLANE, p.shape[1]), BF16)
    return _dot(ones, hi, NT) + _dot(ones, lo, NT)


def _scalar_rows(rows):
    shape = next(r.shape for r in rows if r is not None)
    index = lax.broadcasted_iota(jnp.int32, shape, 0)
    out = jnp.zeros(shape, F32)
    for j, r in enumerate(rows):
        if r is not None:
            out = jnp.where(index == j, r, out)
    return out


BETA, EG, EGR = 0, 1, 2


def _scalars(sc_ref, h, c, width, which):
    rows = sc_ref[h, c]
    return [_lane_broadcast(rows[j:j + 1], width) for j in which]


def _off_diagonal(x):
    row = lax.broadcasted_iota(jnp.int32, x.shape, 0)
    col = lax.broadcasted_iota(jnp.int32, x.shape, 1)
    return jnp.where(row == col, 0.0, x)


def _gdn_prep_fwd(q, k, v, decay, scal):
    dk, dv = k.shape[-1], v.shape[-1]

    def body(nb, q_ref, k_ref, v_ref, dec_ref, sc_ref, u_ref, w_ref, qk_ref, t_ref):
        chunks = range(nb)
        sc = [_scalars(sc_ref, 0, c, dk, (BETA, EG)) for c in chunks]
        kf = [k_ref[0, c] for c in chunks]
        kc = [x.astype(BF16) for x in kf]
        kb = [kf[c] * sc[c][0] for c in chunks]
        dec = [dec_ref[0, c] for c in chunks]
        a = [_off_diagonal(_dot(kb[c].astype(BF16), kc[c], NT) * dec[c]) for c in chunks]
        t = _inv_unit_triangular(a)
        for c in chunks:
            tb = t[c].astype(BF16)
            t_ref[0, c] = t[c]
            u_ref[0, c] = _dot(tb, (v_ref[0, c] * sc[c][0][:, :dv]).astype(BF16), NN)
            w_ref[0, c] = _dot(tb, (kb[c] * sc[c][1]).astype(BF16), NN)
            qk_ref[0, c] = _dot(_bf(q_ref, 0, c), kc[c], NT) * dec[c]

    ins = (q, k, v, decay, scal)
    outs = (jax.ShapeDtypeStruct(v.shape, F32), jax.ShapeDtypeStruct(v.shape[:3] + (dk,), F32),
            jax.ShapeDtypeStruct(decay.shape, F32), jax.ShapeDtypeStruct(decay.shape, F32))
    return _prep_call(body, ins, outs, "gdn_prep_fwd", PREP_CHUNKS_FWD)


def _gdn_prep_bwd(q, k, v, decay, scal, t, du, dw, dqk):
    dk = k.shape[-1]

    def body(nb, q_ref, k_ref, v_ref, dec_ref, sc_ref, t_ref, du_ref, dw_ref, dqk_ref,
             dq_ref, dk_ref, dv_ref, ddec_ref, dsc_ref):
        chunks = range(nb)
        sc = [_scalars(sc_ref, 0, c, dk, (BETA, EG)) for c in chunks]
        beta = [s[0] for s in sc]
        eg = [s[1] for s in sc]
        kf = [k_ref[0, c] for c in chunks]
        vf = [v_ref[0, c] for c in chunks]
        kc = [x.astype(BF16) for x in kf]
        qc = [_bf(q_ref, 0, c) for c in chunks]
        kb = [kf[c] * beta[c] for c in chunks]
        kbb = [x.astype(BF16) for x in kb]
        tb = [_bf(t_ref, 0, c) for c in chunks]
        dub = [_bf(du_ref, 0, c) for c in chunks]
        dwb = [_bf(dw_ref, 0, c) for c in chunks]
        dvb = [_dot(tb[c], dub[c], TN) for c in chunks]
        dkbg = [_dot(tb[c], dwb[c], TN) for c in chunks]
        dt = [_dot(dub[c], (vf[c] * beta[c]).astype(BF16), NT) + _dot(dwb[c], (kb[c] * eg[c]).astype(BF16), NT)
              for c in chunks]
        left = [_dot(tb[c], dt[c].astype(BF16), TN).astype(BF16) for c in chunks]
        da = [-_off_diagonal(_dot(left[c], tb[c], NT)) for c in chunks]
        dqk = [dqk_ref[0, c] for c in chunks]
        for c in chunks:
            ddec_ref[0, c] = da[c] * _dot(kbb[c], kc[c], NT) + dqk[c] * _dot(qc[c], kc[c], NT)
        dkk = [(da[c] * dec_ref[0, c]).astype(BF16) for c in chunks]
        dqkd = [(dqk[c] * dec_ref[0, c]).astype(BF16) for c in chunks]
        dkb = [_dot(dkk[c], kc[c], NN) + dkbg[c] * eg[c] for c in chunks]
        for c in chunks:
            dq_ref[0, c] = _dot(dqkd[c], kc[c], NN)
            dk_ref[0, c] = _dot(dkk[c], kbb[c], TN) + _dot(dqkd[c], qc[c], TN) + dkb[c] * beta[c]
            dv_ref[0, c] = dvb[c] * beta[c]
            dsc_ref[0, c] = _scalar_rows([_token_sums(dkb[c] * kf[c] + dvb[c] * vf[c]),
                                          _token_sums(dkbg[c] * kb[c])])

    ins = (q, k, v, decay, scal, t, du, dw, dqk)
    heads = v.shape[0]
    outs = (jax.ShapeDtypeStruct((heads,) + q.shape[1:], F32), jax.ShapeDtypeStruct((heads,) + k.shape[1:], F32),
            jax.ShapeDtypeStruct(v.shape, F32), jax.ShapeDtypeStruct(decay.shape, F32),
            jax.ShapeDtypeStruct(scal.shape, F32))
    return _prep_call(body, ins, outs, "gdn_prep_bwd", PREP_CHUNKS_BWD)


def _add_shared_heads(d, key_heads):
    return d.reshape((key_heads, d.shape[0] // key_heads) + d.shape[1:]).sum(axis=1)


@jax.custom_vjp
def _gdn_prep(q, k, v, decay, scal):
    return _gdn_prep_fwd(q, k, v, decay, scal)[:3]


def _gdn_prep_vjp_fwd(q, k, v, decay, scal):
    u, w, qk, t = _gdn_prep_fwd(q, k, v, decay, scal)
    return (u, w, qk), (q, k, v, decay, scal, t)


def _gdn_prep_vjp_bwd(res, cot):
    dq, dk, dv, ddec, dsc = _gdn_prep_bwd(*res, *cot)
    return _add_shared_heads(dq, res[0].shape[0]), _add_shared_heads(dk, res[1].shape[0]), dv, ddec, dsc


_gdn_prep.defvjp(_gdn_prep_vjp_fwd, _gdn_prep_vjp_bwd)


def _loss_head(y, target):
    rows, d = y.shape
    tr = _tile(rows, 256, SUBLANE)

    def body(y_ref, t_ref, dy_ref, sq_ref):
        diff = y_ref[...] - t_ref[...]
        dy_ref[...] = diff * (1.0 / d)
        part = jnp.sum((diff * diff).reshape(tr // SUBLANE, SUBLANE, d), axis=0)

        @pl.when(pl.program_id(0) == 0)
        def _():
            sq_ref[...] = part

        @pl.when(pl.program_id(0) > 0)
        def _():
            sq_ref[...] += part

    row = pl.BlockSpec((tr, d), lambda i: (i, 0))
    return pl.pallas_call(
        body,
        out_shape=(jax.ShapeDtypeStruct((rows, d), F32), jax.ShapeDtypeStruct((SUBLANE, d), F32)),
        grid=(rows // tr,),
        in_specs=[row, row],
        out_specs=[row, pl.BlockSpec((SUBLANE, d), lambda i: (0, 0))],
        compiler_params=_params("arbitrary"),
        name="loss_head",
    )(y, target)


def _add_half(g, r1, my_c):
    nblk, half, cols = r1.shape
    tr = _tile(half, 128, 2 * SUBLANE)
    per_half = half // tr

    def body(c_ref, g_ref, r_ref, o_ref):
        o_ref[...] = (g_ref[...] + r_ref[...]).astype(BF16)

    grid_spec = pltpu.PrefetchScalarGridSpec(
        num_scalar_prefetch=1,
        grid=(nblk, per_half),
        in_specs=[
            pl.BlockSpec((1, tr, cols), lambda b, i, c_ref: (b, c_ref[0] * per_half + i, 0)),
            pl.BlockSpec((1, tr, cols), lambda b, i, c_ref: (b, i, 0)),
        ],
        out_specs=pl.BlockSpec((1, tr, cols), lambda b, i, c_ref: (b, i, 0)),
    )
    return pl.pallas_call(
        body,
        out_shape=jax.ShapeDtypeStruct(r1.shape, BF16),
        grid_spec=grid_spec,
        compiler_params=_params("parallel", "parallel"),
        name="grad_add_half",
    )(my_c.reshape(1).astype(jnp.int32), g, r1)


def _sum_chips(own, slots, place):
    _, half, cols = own.shape
    tr = _tile(half, 128, 2 * SUBLANE)
    per_half = half // tr

    def body(p_ref, a_ref, b_ref, c_ref, d_ref, o_ref):
        f32 = lambda ref: ref[0].astype(F32)
        o_ref[...] = (f32(a_ref) + f32(b_ref)) + (f32(c_ref) + f32(d_ref))

    def block(k):
        return pl.BlockSpec((1, tr, cols), lambda i, p_ref: (p_ref[k], i, 0))

    grid_spec = pltpu.PrefetchScalarGridSpec(
        num_scalar_prefetch=1,
        grid=(per_half,),
        in_specs=[block(0), block(1), block(2), block(3)],
        out_specs=pl.BlockSpec((tr, cols), lambda i, p_ref: (p_ref[4] * per_half + i, 0)),
    )
    return pl.pallas_call(
        body,
        out_shape=jax.ShapeDtypeStruct((2 * half, cols), F32),
        grid_spec=grid_spec,
        compiler_params=_params("parallel"),
        name="grad_sum_chips",
    )(place, own, slots, slots, slots)


def _cast_into_block(w, chip):
    rows, cols = w.shape
    tr = _tile(rows, 128, 2 * SUBLANE)

    def body(j_ref, w_ref, o_ref):
        o_ref[0] = w_ref[...].astype(BF16)

    grid_spec = pltpu.PrefetchScalarGridSpec(
        num_scalar_prefetch=1,
        grid=(rows // tr,),
        in_specs=[pl.BlockSpec((tr, cols), lambda i, j_ref: (i, 0))],
        out_specs=pl.BlockSpec((1, tr, cols), lambda i, j_ref: (j_ref[0], i, 0)),
    )
    return pl.pallas_call(
        body,
        out_shape=jax.ShapeDtypeStruct((N_CHIP, rows, cols), BF16),
        grid_spec=grid_spec,
        compiler_params=_params("parallel"),
        name="cast_weight_shard",
    )(chip.reshape(1).astype(jnp.int32), w)


def _sum_leading(x, name):
    k, rows, cols = x.shape
    tr = _tile(rows, 128, SUBLANE)

    def body(x_ref, o_ref):
        parts = [x_ref[i] for i in range(k)]
        while len(parts) > 1:
            parts = [parts[i] + parts[i + 1] for i in range(0, len(parts), 2)]
        o_ref[...] = parts[0]

    return pl.pallas_call(
        body,
        out_shape=jax.ShapeDtypeStruct((rows, cols), F32),
        grid=(rows // tr,),
        in_specs=[pl.BlockSpec((k, tr, cols), lambda i: (0, i, 0))],
        out_specs=pl.BlockSpec((tr, cols), lambda i: (i, 0)),
        compiler_params=_params("parallel"),
        name=name,
    )(x)


def _adamw(w, g, m, v, name):
    rows, cols = w.shape
    tr = _tile(rows, 128, SUBLANE)
    m_scale = 1.0 / (1.0 - ADAM_B1 ** ADAM_STEP)
    v_scale = 1.0 / (1.0 - ADAM_B2 ** ADAM_STEP)

    def body(w_ref, g_ref, m_ref, v_ref, d_ref, nm_ref, nv_ref):
        gg = g_ref[...]
        nm = ADAM_B1 * m_ref[...] + (1.0 - ADAM_B1) * gg
        nv = ADAM_B2 * v_ref[...] + (1.0 - ADAM_B2) * (gg * gg)
        nm_ref[...] = nm
        nv_ref[...] = nv
        d_ref[...] = -ADAM_LR * ((nm * m_scale) / (jnp.sqrt(nv * v_scale) + ADAM_EPS) + ADAM_WD * w_ref[...])

    spec = pl.BlockSpec((tr, cols), lambda i: (i, 0))
    out = jax.ShapeDtypeStruct((rows, cols), F32)
    return pl.pallas_call(
        body,
        out_shape=(out, out, out),
        grid=(rows // tr,),
        in_specs=[spec] * 4,
        out_specs=[spec] * 3,
        compiler_params=_params("parallel"),
        name=name,
    )(w, g, m, v)


def _ada_fwd(a16, w):
    layers, d, n = w.shape
    tn = _tile(n, 512)

    def body(a_ref, w_ref, o_ref):
        o_ref[0] = _dot(a_ref[...].astype(BF16), w_ref[0].astype(BF16), NN)

    return pl.pallas_call(
        body,
        out_shape=jax.ShapeDtypeStruct((layers, a16.shape[0], n), F32),
        grid=(layers, n // tn),
        in_specs=[pl.BlockSpec(a16.shape, lambda l, j: (0, 0)), pl.BlockSpec((1, d, tn), lambda l, j: (l, 0, j))],
        out_specs=pl.BlockSpec((1, a16.shape[0], tn), lambda l, j: (l, 0, j)),
        compiler_params=_params("parallel", "parallel"),
        name="ada_fwd",
    )(a16, w)


def _ada_bwd(a16, dm, w):
    layers, d, n = w.shape
    tn = _tile(n, 512)
    rows = a16.shape[0]

    def body(a_ref, dm_ref, w_ref, gw_ref, ga_ref):
        dmb = dm_ref[0].astype(BF16)
        gw_ref[0] = _dot(a_ref[...].astype(BF16), dmb, TN)
        part = _dot(dmb, w_ref[0].astype(BF16), NT)
        first = (pl.program_id(0) == 0) & (pl.program_id(1) == 0)

        @pl.when(first)
        def _():
            ga_ref[...] = part

        @pl.when(jnp.logical_not(first))
        def _():
            ga_ref[...] += part

    return pl.pallas_call(
        body,
        out_shape=(jax.ShapeDtypeStruct(w.shape, F32), jax.ShapeDtypeStruct((rows, d), F32)),
        grid=(layers, n // tn),
        in_specs=[
            pl.BlockSpec((rows, d), lambda l, j: (0, 0)),
            pl.BlockSpec((1, rows, tn), lambda l, j: (l, 0, j)),
            pl.BlockSpec((1, d, tn), lambda l, j: (l, 0, j)),
        ],
        out_specs=[pl.BlockSpec((1, d, tn), lambda l, j: (l, 0, j)), pl.BlockSpec((rows, d), lambda l, j: (0, 0))],
        compiler_params=_params("arbitrary", "arbitrary"),
        name="ada_bwd",
    )(a16, dm, w)


def _place():
    ix, iy, ic = lax.axis_index("x"), lax.axis_index("y"), lax.axis_index("c")
    return ix, iy, ic


def _flip(coord, bit):
    return 1 - coord if bit else coord


def _remote(src, dst, send_sem, recv_sem, to):
    return pltpu.make_async_remote_copy(
        src_ref=src, dst_ref=dst, send_sem=send_sem, recv_sem=recv_sem, device_id=to, device_id_type=MESH)


def _all_gather8(x, name):
    rows, cols = x.shape

    def body(x_ref, o_ref, send_sems, recv_sems):
        ix, iy, ic = _place()
        me = 4 * ix + 2 * iy + ic
        o_ref[me] = x_ref[...]
        peers = []
        for p in range(1, N_DEV):
            to = (_flip(ix, p & 4), _flip(iy, p & 2), _flip(ic, p & 1))
            peers.append((to, 4 * to[0] + 2 * to[1] + to[2]))
        sends = [_remote(x_ref, o_ref.at[me], send_sems.at[p], recv_sems.at[p], to)
                 for p, (to, _) in enumerate(peers)]
        for cp in sends:
            cp.start()
        for p, (to, slot) in enumerate(peers):
            _remote(x_ref, o_ref.at[slot], send_sems.at[p], recv_sems.at[p], to).wait_recv()
        for cp in sends:
            cp.wait_send()

    return pl.pallas_call(
        body,
        out_shape=jax.ShapeDtypeStruct((N_DEV, rows, cols), F32),
        in_specs=[pl.BlockSpec(memory_space=pltpu.VMEM)],
        out_specs=pl.BlockSpec(memory_space=pltpu.VMEM),
        scratch_shapes=[pltpu.SemaphoreType.DMA((N_DEV - 1,)), pltpu.SemaphoreType.DMA((N_DEV - 1,))],
        compiler_params=pltpu.CompilerParams(vmem_limit_bytes=VMEM_LIMIT),
        name=name,
    )(x)


def _other_chips(ix, iy):
    chips = [(1 - ix, iy), (ix, 1 - iy), (1 - ix, 1 - iy)]
    return [(cx, cy, 2 * cx + cy) for cx, cy in chips]


def _hbm_call(body, ins, out_shapes, scratch, name, in_place=False):
    hbm = pl.BlockSpec(memory_space=pltpu.HBM)
    return pl.pallas_call(
        body,
        out_shape=out_shapes,
        in_specs=[hbm] * len(ins),
        out_specs=[hbm] * len(out_shapes),
        scratch_shapes=scratch,
        input_output_aliases={t: t for t in range(len(ins))} if in_place else {},
        compiler_params=pltpu.CompilerParams(vmem_limit_bytes=VMEM_LIMIT),
        name=name,
    )(*ins)


def _gather_weights(blocks):
    nt = len(blocks)

    def body(*refs):
        ins, outs = refs[:nt], refs[nt:2 * nt]
        ici_send, ici_recv, d2d_send, d2d_recv = refs[2 * nt:]
        ix, iy, ic = _place()
        j = 2 * ix + iy
        sibling = (ix, iy, 1 - ic)
        chips = _other_chips(ix, iy)
        sends, waits = [], []
        for t in range(nt):
            half = ins[t].shape[1] // 2
            mine = pl.ds(ic * half, half)
            for q, (cx, cy, _) in enumerate(chips):
                cp = _remote(ins[t].at[j, mine], outs[t].at[j, mine], ici_send.at[3 * t + q], ici_recv.at[3 * t + q],
                             (cx, cy, ic))
                cp.start()
                sends.append(cp)
        for q, (cx, cy, jq) in enumerate(chips):
            for t in range(nt):
                half = ins[t].shape[1] // 2
                mine = pl.ds(ic * half, half)
                theirs = pl.ds((1 - ic) * half, half)
                k = 3 * t + q
                _remote(ins[t].at[j, mine], outs[t].at[jq, mine], ici_send.at[k], ici_recv.at[k],
                        (cx, cy, ic)).wait_recv()
                cp = _remote(outs[t].at[jq, mine], outs[t].at[jq, mine], d2d_send.at[k], d2d_recv.at[k], sibling)
                cp.start()
                sends.append(cp)
                waits.append(_remote(outs[t].at[jq, theirs], outs[t].at[jq, theirs], d2d_send.at[k], d2d_recv.at[k],
                                     sibling))
        for cp in waits:
            cp.wait_recv()
        for cp in sends:
            cp.wait_send()

    dma = pltpu.SemaphoreType.DMA
    outs = tuple(jax.ShapeDtypeStruct(b.shape, b.dtype) for b in blocks)
    return _hbm_call(body, blocks, outs, [dma((3 * nt,))] * 4, "gather_weights", in_place=True)


def _exchange_halves(grads):
    nt = len(grads)

    def body(*refs):
        ins, outs = refs[:nt], refs[nt:2 * nt]
        send_sems, recv_sems = refs[2 * nt:]
        ix, iy, ic = _place()
        sibling = (ix, iy, 1 - ic)
        copies = []
        for t in range(nt):
            half = ins[t].shape[1] // 2
            theirs = pl.ds((1 - ic) * half, half)
            cp = _remote(ins[t].at[:, theirs, :], outs[t], send_sems.at[t], recv_sems.at[t], sibling)
            cp.start()
            copies.append(cp)
        for cp in copies:
            cp.wait_recv()
        for cp in copies:
            cp.wait_send()

    dma = pltpu.SemaphoreType.DMA
    outs = tuple(jax.ShapeDtypeStruct((g.shape[0], g.shape[1] // 2, g.shape[2]), F32) for g in grads)
    return _hbm_call(body, grads, outs, [dma((nt,)), dma((nt,))], "grad_exchange_halves")


def _exchange_blocks(parts):
    nt = len(parts)

    def body(*refs):
        ins, outs = refs[:nt], refs[nt:2 * nt]
        send_sems, recv_sems = refs[2 * nt:]
        ix, iy, ic = _place()
        j = 2 * ix + iy
        chips = _other_chips(ix, iy)
        sends, waits = [], []
        for t in range(nt):
            for q, (cx, cy, jq) in enumerate(chips):
                k = 3 * t + q
                cp = _remote(ins[t].at[jq], outs[t].at[j], send_sems.at[k], recv_sems.at[k], (cx, cy, ic))
                cp.start()
                sends.append(cp)
                waits.append(_remote(ins[t].at[jq], outs[t].at[jq], send_sems.at[k], recv_sems.at[k], (cx, cy, ic)))
        for cp in waits:
            cp.wait_recv()
        for cp in sends:
            cp.wait_send()

    dma = pltpu.SemaphoreType.DMA
    outs = tuple(jax.ShapeDtypeStruct(p.shape, p.dtype) for p in parts)
    return _hbm_call(body, parts, outs, [dma((3 * nt,)), dma((3 * nt,))], "grad_exchange_blocks")


def _share_halves(blocks):
    nt = len(blocks)

    def body(*refs):
        ins, outs = refs[:nt], refs[nt:2 * nt]
        send_sems, recv_sems = refs[2 * nt:]
        ix, iy, ic = _place()
        sibling = (ix, iy, 1 - ic)
        sends, waits = [], []
        for t in range(nt):
            half = ins[t].shape[0] // 2
            mine = pl.ds(ic * half, half)
            theirs = pl.ds((1 - ic) * half, half)
            cp = _remote(ins[t].at[mine], outs[t].at[mine], send_sems.at[t], recv_sems.at[t], sibling)
            cp.start()
            sends.append(cp)
            waits.append(_remote(ins[t].at[theirs], outs[t].at[theirs], send_sems.at[t], recv_sems.at[t], sibling))
        for cp in waits:
            cp.wait_recv()
        for cp in sends:
            cp.wait_send()

    dma = pltpu.SemaphoreType.DMA
    outs = tuple(jax.ShapeDtypeStruct(b.shape, F32) for b in blocks)
    return _hbm_call(body, blocks, outs, [dma((nt,)), dma((nt,))], "grad_share_halves", in_place=True)


def _reduce_grads(grads, ix, iy, ic):
    recv = _exchange_halves(grads)
    chip_sums = [_add_half(g, r1, ic) for g, r1 in zip(grads, recv)]
    slots = _exchange_blocks(chip_sums)
    place = jnp.stack([2 * ix + iy] + [jq for _, _, jq in _other_chips(ix, iy)] + [ic]).astype(jnp.int32)
    return _share_halves([_sum_chips(own, got, place) for own, got in zip(chip_sums, slots)])


def _rmsnorm(x, w):
    return x * lax.rsqrt(jnp.mean(x * x, axis=-1, keepdims=True) + EPS) * w


def _l2norm(x):
    return x * lax.rsqrt(jnp.sum(x * x, axis=-1, keepdims=True) + EPS)


def _heads(t, d):
    return t.reshape(t.shape[:-1] + (t.shape[-1] // d, d))


def _chunks(t, chunk):
    n = t.shape[0] // chunk
    return t.reshape(n, chunk, t.shape[1], t.shape[2]).transpose(2, 0, 1, 3)


def _unchunks(t):
    hh, n, chunk, d = t.shape
    return t.transpose(1, 2, 0, 3).reshape(n * chunk, hh, d)


def _triangle(size, anti):
    ones = jnp.ones((size, size), bool)
    return jnp.triu(ones) if anti else jnp.tril(ones)


def _hgrn2_scan(q, k, v, log_f, n_ctx, anti):
    q, k, v, log_f = (_chunks(t, A_CHUNK) for t in (q, k, v, log_f))
    b = lax.cumsum(log_f, axis=2, reverse=anti)
    end = 0 if anti else A_CHUNK - 1
    mid = A_CHUNK // 2 if anti else A_CHUNK // 2 - 1
    b_last = b[:, :, end:end + 1, :]
    b_mid = b[:, :, mid:mid + 1, :]
    return _unchunks(_hgrn_rec(anti, n_ctx // A_CHUNK, q * jnp.exp(b), k * jnp.exp(b_last - b), v, jnp.exp(b_last),
                               q * jnp.exp(b - b_mid), k * jnp.exp(b_mid - b)))


def _gdn_scan(q, k, v, g, beta, n_ctx, anti):
    g = _chunks(g[..., None], C_CHUNK)[..., 0]
    beta = _chunks(beta[..., None], C_CHUNK)[..., 0]
    gc = lax.cumsum(g, axis=2, reverse=anti)
    end = 0 if anti else C_CHUNK - 1
    causal = _triangle(C_CHUNK, anti)
    diff = gc[..., :, None] - gc[..., None, :]
    decay = jnp.where(causal, jnp.exp(jnp.where(causal, diff, 0.0)), 0.0)
    rows = [beta, jnp.exp(gc), jnp.exp(gc[..., end:end + 1] - gc)]
    scal = jnp.stack(rows + [jnp.zeros_like(beta)] * (SUBLANE - len(rows)), axis=2)
    u, w, qk = _gdn_prep(q, k, v, decay, scal)
    g_last = jnp.exp(gc[..., end])
    gl = jnp.broadcast_to(g_last[..., None, None], g_last.shape + (1, q.shape[-1]))
    return _unchunks(_gdn_rec(anti, n_ctx // C_CHUNK, u, w, qk, q, k, scal, gl))


def _bidirectional(scan_fn, fwd_args, bwd_args, n_ctx):
    return scan_fn(*fwd_args, n_ctx, False) + scan_fn(*bwd_args, n_ctx, True)


def _multiscale_pool(u, pool_w, pool_scale):
    rows, length, width = u.shape
    groups = len(POOL_WINDOWS)
    gdim = width // groups
    uf = u.reshape(rows, length, groups, gdim)
    cs = jnp.concatenate([jnp.zeros_like(uf[:, :1]), jnp.cumsum(uf, axis=1)], axis=1)
    pos = np.arange(length)
    mixed = []
    for gi, win in enumerate(POOL_WINDOWS):
        below, above = win // 2, win - win // 2
        lo = np.clip(pos - below, 0, length - 1)
        hi = np.clip(pos + win - 1 - below, 0, length - 1)
        cnt = jnp.asarray((hi - lo + 1).astype(np.float32))[None, :, None]
        csg = cs[:, :, gi, :]
        upper = jnp.concatenate([csg[:, above:]] + [csg[:, length:]] * (above - 1), axis=1)
        lower = jnp.concatenate([jnp.zeros_like(csg[:, :below]), csg[:, :length - below]], axis=1)
        mixed.append((upper - lower) / cnt - uf[:, :, gi, :])
    d = jnp.stack(mixed, axis=2)
    y = jnp.einsum("rlgc,gcd->rlgd", d, pool_w)
    return y.reshape(rows, length, width) * pool_scale


CONV_LEFT = C_CONV // 2
CONV_PAD = SUBLANE


def _conv_window_taps(win, r0, rows, n_ctx, reverse):
    tg = r0 + lax.broadcasted_iota(jnp.int32, (rows, win.shape[1]), 0)
    taps = []
    for j in range(C_CONV):
        off = (CONV_LEFT - j) if reverse else (j - CONV_LEFT)
        if off == 0:
            taps.append(win[CONV_PAD:CONV_PAD + rows])
            continue
        moved = pltpu.roll(win, (-off) % win.shape[0], 0)[CONV_PAD:CONV_PAD + rows]
        lo, hi = (n_ctx, n_ctx - off) if off < 0 else (n_ctx - off, n_ctx)
        taps.append(jnp.where(tg >= lo, jnp.where(tg < hi, 0.0, moved), moved))
    return taps


def _fill_padded(pad, src, total):
    zeros = jnp.zeros((CONV_PAD, pad.shape[1]), F32)
    pad[0:CONV_PAD] = zeros
    pad[total + CONV_PAD:total + 2 * CONV_PAD] = zeros
    if src is not None:
        pad[CONV_PAD:total + CONV_PAD] = src[...]


def _conv_call(body, ins, outs, out_blocks, scratch, name):
    total, width = ins[0].shape
    tc = LANE
    col = pl.BlockSpec((total, tc), lambda j: (0, j))
    return pl.pallas_call(
        body,
        out_shape=outs,
        grid=(width // tc,),
        in_specs=[col if t.shape[0] == total else pl.BlockSpec((t.shape[0], tc), lambda j: (0, j)) for t in ins],
        out_specs=[pl.BlockSpec((rows, tc), lambda j: (0, j)) for rows in out_blocks],
        scratch_shapes=[pltpu.VMEM((total + 2 * CONV_PAD, tc), F32)] * scratch,
        compiler_params=_params("parallel"),
        name=name,
    )(*ins)


def _conv_silu_fwd(n_ctx, x, w):
    total = x.shape[0]
    rows = _tile(total, 128, SUBLANE)

    def body(x_ref, w_ref, o_ref, pad):
        _fill_padded(pad, x_ref, total)
        wt = w_ref[...]

        def chunk(i, carry):
            r0 = pl.multiple_of(i * rows, SUBLANE)
            taps = _conv_window_taps(pad[pl.ds(r0, rows + 2 * CONV_PAD), :], r0, rows, n_ctx, False)
            pre = sum(t * wt[j:j + 1] for j, t in enumerate(taps))
            o_ref[pl.ds(r0, rows), :] = pre / (1.0 + jnp.exp(-pre))
            return carry

        lax.fori_loop(0, total // rows, chunk, 0)

    return _conv_call(body, (x, w), (jax.ShapeDtypeStruct(x.shape, F32),), (total,), 1, "conv_silu_fwd")[0]


def _conv_silu_bwd(n_ctx, x, w, dact):
    total = x.shape[0]
    rows = _tile(total, 128, SUBLANE)

    def body(x_ref, w_ref, da_ref, dx_ref, dw_ref, xpad, gpad):
        _fill_padded(xpad, x_ref, total)
        _fill_padded(gpad, None, total)
        wt = w_ref[...]
        dw_ref[...] = jnp.zeros_like(dw_ref)

        def grad_pre(i, carry):
            r0 = pl.multiple_of(i * rows, SUBLANE)
            taps = _conv_window_taps(xpad[pl.ds(r0, rows + 2 * CONV_PAD), :], r0, rows, n_ctx, False)
            pre = sum(t * wt[j:j + 1] for j, t in enumerate(taps))
            sig = 1.0 / (1.0 + jnp.exp(-pre))
            dpre = da_ref[pl.ds(r0, rows), :] * (sig * (1.0 + pre * (1.0 - sig)))
            gpad[pl.ds(r0 + CONV_PAD, rows), :] = dpre
            for j, t in enumerate(taps):
                part = jnp.sum((dpre * t).reshape(rows // SUBLANE, SUBLANE, t.shape[1]), axis=0)
                dw_ref[SUBLANE * j:SUBLANE * (j + 1)] += part
            return carry

        lax.fori_loop(0, total // rows, grad_pre, 0)

        def grad_x(i, carry):
            r0 = pl.multiple_of(i * rows, SUBLANE)
            taps = _conv_window_taps(gpad[pl.ds(r0, rows + 2 * CONV_PAD), :], r0, rows, n_ctx, True)
            dx_ref[pl.ds(r0, rows), :] = sum(t * wt[j:j + 1] for j, t in enumerate(taps))
            return carry

        lax.fori_loop(0, total // rows, grad_x, 0)

    outs = (jax.ShapeDtypeStruct(x.shape, F32), jax.ShapeDtypeStruct((C_CONV * SUBLANE, x.shape[1]), F32))
    return _conv_call(body, (x, w, dact), outs, (total, C_CONV * SUBLANE), 2, "conv_silu_bwd")


@functools.partial(jax.custom_vjp, nondiff_argnums=(0,))
def _conv_silu(n_ctx, x, w):
    return _conv_silu_fwd(n_ctx, x, w)


def _conv_silu_vjp_fwd(n_ctx, x, w):
    return _conv_silu_fwd(n_ctx, x, w), (x, w)


def _conv_silu_vjp_bwd(n_ctx, res, dact):
    x, w = res
    dx, dw = _conv_silu_bwd(n_ctx, x, w, dact)
    return dx, dw.reshape(C_CONV, SUBLANE, -1).sum(axis=1)


_conv_silu.defvjp(_conv_silu_vjp_fwd, _conv_silu_vjp_bwd)


def _hgrn2_gates(pre_f, lb):
    log_f = jnp.log(lb + (1.0 - lb) * jax.nn.sigmoid(pre_f))
    k = (1.0 - lb) * jax.nn.sigmoid(-pre_f)
    return _heads(k, A_HEAD_DIM), _heads(log_f, A_HEAD_DIM)


def _even_mixer(h, n_ctx, lb, w_in, w_in_hook, a_norm, pool_w, pool_scale, w_out, w_out_hook):
    a_width = w_out.shape[0] // 2
    p = _dense(h, w_in, w_in_hook)
    q, f_f, f_b, i, g, u = jnp.split(p, [a_width * s for s in range(1, 6)], axis=-1)
    q = _heads(jax.nn.silu(q), A_HEAD_DIM)
    i = _heads(i, A_HEAD_DIM)
    k_f, logf_f = _hgrn2_gates(f_f, lb[0])
    k_b, logf_b = _hgrn2_gates(f_b, lb[1])
    o = _bidirectional(_hgrn2_scan, (q, k_f, i, logf_f), (q, k_b, i, logf_b), n_ctx)
    b_width = u.shape[-1]
    u_l = u[n_ctx:].reshape(-1, GRID_W, b_width)
    pooled = jnp.concatenate([
        _multiscale_pool(u[None, :n_ctx], pool_w, pool_scale)[0],
        _multiscale_pool(u_l, pool_w, pool_scale).reshape(-1, b_width),
    ], axis=0)
    a_out = _rmsnorm(o, a_norm) * jax.nn.silu(_heads(g, A_HEAD_DIM))
    a_out = a_out.reshape(a_out.shape[0], a_width)
    return _dense(jnp.concatenate([a_out, pooled], axis=-1), w_out, w_out_hook)


def _odd_mixer(h, n_ctx, big, hooks, conv_w, a_log, dt_bias, norm_w):
    w_out, w_out_hook = big["od_w_out"], hooks["od_w_out"]
    value_width = w_out.shape[0]
    key_width = value_width // 2
    a_rate = jnp.exp(a_log)
    z = _dense(h, big["od_w_z"], hooks["od_w_z"])
    gates = _dense(h, big["od_w_gate"], hooks["od_w_gate"])
    qkv = _conv_silu(n_ctx, _dense(h, big["od_w_qkv"], hooks["od_w_qkv"]), conv_w)
    q, k, v = jnp.split(qkv, [key_width, 2 * key_width], axis=-1)
    q = _chunks(_l2norm(_heads(q, C_HEAD_DIM)) * C_HEAD_DIM ** -0.5, C_CHUNK)
    k = _chunks(_l2norm(_heads(k, C_HEAD_DIM)), C_CHUNK)
    v = _chunks(_heads(v, C_HEAD_DIM), C_CHUNK)
    a_f, a_b, b_f, b_b = jnp.split(gates, 4, axis=-1)
    g_f = -a_rate[0] * jax.nn.softplus(a_f + dt_bias[0])
    g_b = -a_rate[1] * jax.nn.softplus(a_b + dt_bias[1])
    o = _bidirectional(_gdn_scan, (q, k, v, g_f, jax.nn.sigmoid(b_f)), (q, k, v, g_b, jax.nn.sigmoid(b_b)), n_ctx)
    y = _rmsnorm(o, norm_w) * jax.nn.silu(_heads(z, C_HEAD_DIM))
    return _dense(y.reshape(y.shape[0], value_width), w_out, w_out_hook)


def _swiglu(h, w13, w13_hook, w2, w2_hook):
    gate, up = jnp.split(_dense(h, w13, w13_hook), 2, axis=-1)
    return _dense(jax.nn.silu(gate) * up, w2, w2_hook)


def _forward(x, ctx, mods, big, hooks, small):
    n_ctx = ctx.shape[0]
    d = x.shape[-1]
    stream = jnp.concatenate([ctx, x], axis=0)
    is_ctx = (jnp.arange(stream.shape[0]) < n_ctx)[:, None]
    lb_all = jnp.cumsum(jax.nn.softmax(small["ev_lb"], axis=1), axis=1)
    for layer in range(2):
        m = [jnp.where(is_ctx, mods[layer, 1, s * d:(s + 1) * d][None, :], mods[layer, 0, s * d:(s + 1) * d][None, :])
             for s in range(6)]
        nw = small["norm_w"][layer]
        h = _rmsnorm(stream, nw[0]) * (1.0 + m[1]) + m[0]
        if layer == 0:
            y = _even_mixer(h, n_ctx, lb_all[:, layer], big["ev_w_in"], hooks["ev_w_in"], small["ev_a_norm"][0],
                            small["ev_pool_w"][0], small["ev_pool_scale"][0], big["ev_w_out"], hooks["ev_w_out"])
        else:
            y = _odd_mixer(h, n_ctx, big, hooks, small["od_conv"][0], small["od_A_log"][0], small["od_dt_bias"][0],
                           small["od_norm"][0])
        stream = stream + m[2] * _rmsnorm(y, nw[1])
        h = _rmsnorm(stream, nw[2]) * (1.0 + m[4]) + m[3]
        f = _swiglu(h, big["ffn_w13_%d" % layer], hooks["ffn_w13_%d" % layer],
                    big["ffn_w2_%d" % layer], hooks["ffn_w2_%d" % layer])
        stream = stream + m[5] * _rmsnorm(f, nw[3])
    return stream[n_ctx:]


def _pack(arrays):
    flat = jnp.concatenate([a.reshape(-1) for a in arrays])
    pad = (-flat.shape[0]) % (SUBLANE * LANE)
    return jnp.pad(flat, (0, pad)).reshape(-1, LANE)


def _unpack(packed, shapes, lead=()):
    flat = packed.reshape(lead + (-1,))
    out, at = [], 0
    for shape in shapes:
        size = int(np.prod(shape))
        out.append(flat[..., at:at + size].reshape(lead + tuple(shape)))
        at += size
    return out


def _from_chips(gathered, axis):
    return jnp.concatenate([gathered[2 * j] for j in range(N_CHIP)], axis=axis)


def _col_natural(blocks):
    nblk, k, n = blocks.shape
    return blocks.transpose(1, 0, 2).reshape(k, nblk * n)


def _col_blocked(nat):
    k, n4 = nat.shape
    return nat.reshape(k, N_CHIP, n4 // N_CHIP).transpose(1, 0, 2)


def kernel(x, c, ctx, c_ctx, w_ada, b_ada, norm_w, ev_w_in, ev_lb, ev_a_norm, ev_pool_w, ev_pool_scale, ev_w_out, od_w_in, od_conv, od_A_log, od_dt_bias, od_norm, od_w_out, ffn_w13, ffn_w2, loss_target, m_c_ctx, m_w_ada, m_b_ada, m_norm_w, m_ev_w_in, m_ev_lb, m_ev_a_norm, m_ev_pool_w, m_ev_pool_scale, m_ev_w_out, m_od_w_in, m_od_conv, m_od_A_log, m_od_dt_bias, m_od_norm, m_od_w_out, m_ffn_w13, m_ffn_w2, v_c_ctx, v_w_ada, v_b_ada, v_norm_w, v_ev_w_in, v_ev_lb, v_ev_a_norm, v_ev_pool_w, v_ev_pool_scale, v_ev_w_out, v_od_w_in, v_od_conv, v_od_A_log, v_od_dt_bias, v_od_norm, v_od_w_out, v_ffn_w13, v_ffn_w2):
    ix, iy, ic = _place()
    me = 4 * ix + 2 * iy + ic
    chip = 2 * ix + iy
    d = x.shape[-1]
    layers = w_ada.shape[0]
    n_ada = w_ada.shape[-1]

    pre_parts = [c[0], norm_w, ev_lb, ev_pool_w, od_conv]
    pre = _all_gather8(_pack(pre_parts), "gather_small_inputs")
    c_all, norm_w_g, ev_lb_g, pool_w_g, od_conv_g = _unpack(pre, [p.shape for p in pre_parts], lead=(N_DEV,))
    small = {
        "norm_w": _from_chips(norm_w_g, 2),
        "ev_lb": _from_chips(ev_lb_g, 2),
        "ev_a_norm": ev_a_norm,
        "ev_pool_w": _from_chips(pool_w_g, 2),
        "ev_pool_scale": ev_pool_scale,
        "od_conv": _from_chips(od_conv_g, 2),
        "od_A_log": od_A_log,
        "od_dt_bias": od_dt_bias,
        "od_norm": od_norm,
    }

    silu_cc, silu_cc_vjp = jax.vjp(jax.nn.silu, c_ctx)
    a16 = jnp.concatenate([jax.nn.silu(c_all), silu_cc[None], jnp.zeros((2 * SUBLANE - N_DEV - 1, d), F32)], axis=0)
    mods_local = _ada_fwd(a16, w_ada)
    mods_g = _all_gather8(mods_local.reshape(-1, n_ada), "gather_modulation")
    mods_g = mods_g.reshape(N_DEV, layers, 2 * SUBLANE, n_ada)
    mods_full = _from_chips(mods_g, 2) + b_ada[:, None, :]
    mods = jnp.stack([lax.dynamic_index_in_dim(mods_full, me, axis=1, keepdims=False), mods_full[:, N_DEV]], axis=1)

    ffn13_rows = ffn_w13.shape[1]
    ffn2_rows = ffn_w2.shape[1]
    shards = [ev_w_in[0], ev_w_out[0], od_w_in[0], od_w_out[0],
              ffn_w13.reshape(-1, ffn_w13.shape[-1]), ffn_w2.reshape(-1, ffn_w2.shape[-1])]
    g_ev_in, g_ev_out, g_od_in, g_od_out, g_w13, g_w2 = _gather_weights([_cast_into_block(s, chip) for s in shards])
    od_nat = _col_natural(g_od_in)
    n_qkv = 2 * od_w_out.shape[1] * N_CHIP
    n_main = n_qkv + n_qkv // 2
    big = {
        "ev_w_in": _col_natural(g_ev_in),
        "ev_w_out": g_ev_out.reshape(-1, g_ev_out.shape[-1]),
        "od_w_qkv": od_nat[:, :n_qkv],
        "od_w_z": od_nat[:, n_qkv:n_main],
        "od_w_gate": od_nat[:, n_main:],
        "od_w_out": g_od_out.reshape(-1, g_od_out.shape[-1]),
    }
    for layer in range(layers):
        big["ffn_w13_%d" % layer] = _col_natural(g_w13[:, layer * ffn13_rows:(layer + 1) * ffn13_rows])
        big["ffn_w2_%d" % layer] = g_w2[:, layer * ffn2_rows:(layer + 1) * ffn2_rows].reshape(-1, g_w2.shape[-1])
    hooks = {name: jnp.zeros(w.shape, F32) for name, w in big.items()}

    def local_forward(x_, mods_, hooks_, small_):
        return _forward(x_, ctx[0], mods_, big, hooks_, small_)

    y, pullback = jax.vjp(local_forward, x[0], mods, hooks, small)
    dy, sq = _loss_head(y, loss_target[0])
    loss = lax.psum(jnp.sum(sq) * (0.5 / d), ("x", "y", "c"))
    grad_x, d_mods, d_big, d_small = pullback(dy)

    d_od_in = jnp.concatenate([d_big["od_w_qkv"], d_big["od_w_z"], d_big["od_w_gate"]], axis=1)
    blocked = [
        _col_blocked(d_big["ev_w_in"]),
        d_big["ev_w_out"].reshape(N_CHIP, -1, d_big["ev_w_out"].shape[-1]),
        _col_blocked(d_od_in),
        d_big["od_w_out"].reshape(N_CHIP, -1, d_big["od_w_out"].shape[-1]),
        jnp.concatenate([_col_blocked(d_big["ffn_w13_%d" % layer]) for layer in range(layers)], axis=1),
        jnp.concatenate([d_big["ffn_w2_%d" % layer].reshape(N_CHIP, -1, d_big["ffn_w2_%d" % layer].shape[-1])
                         for layer in range(layers)], axis=1),
    ]
    r_ev_in, r_ev_out, r_od_in, r_od_out, r_w13, r_w2 = _reduce_grads(blocked, ix, iy, ic)

    small_names = ["norm_w", "ev_lb", "ev_a_norm", "ev_pool_w", "ev_pool_scale", "od_conv", "od_A_log",
                   "od_dt_bias", "od_norm"]
    post_parts = [d_mods[:, 0], d_mods[:, 1]] + [d_small[n] for n in small_names]
    post = _all_gather8(_pack(post_parts), "gather_small_grads")
    post_sum = _sum_leading(post, "sum_small_grads")
    dm_l_all = _unpack(post, [post_parts[0].shape], lead=(N_DEV,))[0]
    summed = _unpack(post_sum, [p.shape for p in post_parts])
    dm_l_sum, dm_c_sum = summed[0], summed[1]
    g_small = dict(zip(small_names, summed[2:]))
    grad_b_ada = dm_l_sum + dm_c_sum

    def my_cols(full, width):
        return lax.dynamic_slice_in_dim(full, chip * width, width, axis=full.ndim - 1)

    dm_rows = jnp.concatenate([
        my_cols(dm_l_all, n_ada).transpose(1, 0, 2),
        my_cols(dm_c_sum, n_ada)[:, None, :],
        jnp.zeros((layers, 2 * SUBLANE - N_DEV - 1, n_ada), F32),
    ], axis=1)
    grad_w_ada, ga = _ada_bwd(a16, dm_rows, w_ada)
    ga_g = _all_gather8(jnp.pad(ga[N_DEV][None], ((0, SUBLANE - 1), (0, 0))), "gather_c_ctx_grad")
    g_silu_cc = _sum_leading(jnp.stack([ga_g[2 * j] for j in range(N_CHIP)]), "sum_c_ctx_grad")[0]
    grad_c_ctx = silu_cc_vjp(g_silu_cc)[0]

    grads = {
        "c_ctx": grad_c_ctx,
        "w_ada": grad_w_ada,
        "b_ada": grad_b_ada,
        "norm_w": my_cols(g_small["norm_w"], norm_w.shape[-1]),
        "ev_w_in": r_ev_in[None],
        "ev_lb": my_cols(g_small["ev_lb"], ev_lb.shape[-1]),
        "ev_a_norm": g_small["ev_a_norm"],
        "ev_pool_w": lax.dynamic_slice_in_dim(g_small["ev_pool_w"], chip * ev_pool_w.shape[2], ev_pool_w.shape[2], axis=2),
        "ev_pool_scale": g_small["ev_pool_scale"],
        "ev_w_out": r_ev_out[None],
        "od_w_in": r_od_in[None],
        "od_conv": my_cols(g_small["od_conv"], od_conv.shape[-1]),
        "od_A_log": g_small["od_A_log"],
        "od_dt_bias": g_small["od_dt_bias"],
        "od_norm": g_small["od_norm"],
        "od_w_out": r_od_out[None],
        "ffn_w13": r_w13.reshape(ffn_w13.shape),
        "ffn_w2": r_w2.reshape(ffn_w2.shape),
    }

    weights = dict(c_ctx=c_ctx, w_ada=w_ada, b_ada=b_ada, norm_w=norm_w, ev_w_in=ev_w_in, ev_lb=ev_lb,
                   ev_a_norm=ev_a_norm, ev_pool_w=ev_pool_w, ev_pool_scale=ev_pool_scale, ev_w_out=ev_w_out,
                   od_w_in=od_w_in, od_conv=od_conv, od_A_log=od_A_log, od_dt_bias=od_dt_bias, od_norm=od_norm,
                   od_w_out=od_w_out, ffn_w13=ffn_w13, ffn_w2=ffn_w2)
    first = dict(c_ctx=m_c_ctx, w_ada=m_w_ada, b_ada=m_b_ada, norm_w=m_norm_w, ev_w_in=m_ev_w_in, ev_lb=m_ev_lb,
                 ev_a_norm=m_ev_a_norm, ev_pool_w=m_ev_pool_w, ev_pool_scale=m_ev_pool_scale, ev_w_out=m_ev_w_out,
                 od_w_in=m_od_w_in, od_conv=m_od_conv, od_A_log=m_od_A_log, od_dt_bias=m_od_dt_bias,
                 od_norm=m_od_norm, od_w_out=m_od_w_out, ffn_w13=m_ffn_w13, ffn_w2=m_ffn_w2)
    second = dict(c_ctx=v_c_ctx, w_ada=v_w_ada, b_ada=v_b_ada, norm_w=v_norm_w, ev_w_in=v_ev_w_in, ev_lb=v_ev_lb,
                  ev_a_norm=v_ev_a_norm, ev_pool_w=v_ev_pool_w, ev_pool_scale=v_ev_pool_scale, ev_w_out=v_ev_w_out,
                  od_w_in=v_od_w_in, od_conv=v_od_conv, od_A_log=v_od_A_log, od_dt_bias=v_od_dt_bias,
                  od_norm=v_od_norm, od_w_out=v_od_w_out, ffn_w13=v_ffn_w13, ffn_w2=v_ffn_w2)
    names = list(weights)
    large = ["w_ada", "ev_w_in", "ev_w_out", "od_w_in", "od_w_out", "ffn_w13", "ffn_w2"]
    little = [n for n in names if n not in large]
    delta, new_m, new_v = {}, {}, {}
    for n in large:
        as2d = lambda t: t.reshape(-1, t.shape[-1])
        out = _adamw(as2d(weights[n]), as2d(grads[n]), as2d(first[n]), as2d(second[n]), "adamw_" + n)
        delta[n], new_m[n], new_v[n] = (t.reshape(weights[n].shape) for t in out)
    shapes = [weights[n].shape for n in little]
    out = _adamw(*(_pack([src[n] for n in little]) for src in (weights, grads, first, second)), "adamw_small")
    for res, packed in zip((delta, new_m, new_v), out):
        res.update(zip(little, _unpack(packed, shapes)))

    return (loss, grad_x[None], *[grads[n] for n in names], *[delta[n] for n in names],
            *[new_m[n] for n in names], *[new_v[n] for n in names])
```
